```python
import math
import jax, jax.numpy as jnp
from jax import lax
import numpy as np

D_MODEL = 4096
BATCH = 4
SEQ = 2048
DEPTH = 1

GRID_W = 64
CTX_LEN = 256
HEAD_DIM = 128
D_ATT = D_MODEL // 2
N_Q_HEADS = D_ATT // HEAD_DIM
N_KV_HEADS = N_Q_HEADS // 4
Q_PER_KV = N_Q_HEADS // N_KV_HEADS
D_KV = N_KV_HEADS * HEAD_DIM
D_LRU = D_MODEL // 2
LRU_BLOCKS = 16
LRU_BLOCK_DIM = D_LRU // LRU_BLOCKS
CONV_WIDTH = 4
CONV_PAD = (2, 1)
LRU_C = 8.0
ROPE_THETA = 10000.0
Q_BLOCK = 128
D_MIX = D_ATT + D_LRU
D_IN = D_ATT + 2 * D_KV + D_ATT + D_LRU + D_LRU
SPLITS = (D_ATT, D_ATT + D_KV, D_ATT + 2 * D_KV, 2 * D_ATT + 2 * D_KV, 2 * D_ATT + 2 * D_KV + D_LRU)
EPS = 1e-6

kernel_name = "hymba_rglru_gqa_dit_layer"


def rmsnorm(x, w):
    xf = x.astype(jnp.float32)
    y = xf * lax.rsqrt(jnp.mean(xf * xf, axis=-1, keepdims=True) + EPS)
    return (y * w.astype(jnp.float32)).astype(x.dtype)


def rope_tables(n_tokens):
    rows = n_tokens // GRID_W
    row = jnp.repeat(jnp.arange(rows), GRID_W).astype(jnp.float32)
    col = jnp.tile(jnp.arange(GRID_W), rows).astype(jnp.float32)
    n_freq = HEAD_DIM // 4
    freqs = ROPE_THETA ** (-jnp.arange(n_freq, dtype=jnp.float32) / n_freq)
    ang = jnp.stack([row[:, None] * freqs, col[:, None] * freqs], axis=1)
    return jnp.cos(ang)[:, None], jnp.sin(ang)[:, None]


def apply_rope(x, cos, sin):
    B, S, H, _ = x.shape
    xr = x.astype(jnp.float32).reshape(B, S, H, 2, 2, HEAD_DIM // 4)
    x1, x2 = xr[..., 0, :], xr[..., 1, :]
    out = jnp.stack([x1 * cos - x2 * sin, x2 * cos + x1 * sin], axis=-2)
    return out.reshape(B, S, H, HEAD_DIM).astype(x.dtype)


def modulation(cvec, w_ada_l, b_ada_l):
    mod = jax.nn.silu(cvec) @ w_ada_l + b_ada_l
    return jnp.split(mod, 3, axis=-1)


def qkv_heads(q, k, v, qw, kw):
    B, T = q.shape[:2]
    q = rmsnorm(q.reshape(B, T, N_Q_HEADS, HEAD_DIM), qw)
    k = rmsnorm(k.reshape(B, T, N_KV_HEADS, HEAD_DIM), kw)
    v = v.reshape(B, T, N_KV_HEADS, HEAD_DIM)
    return q, k, v


def gqa_attend(q_blk, k, v):
    s = jnp.einsum('bqkgd,bskd->bkgqs', q_blk, k).astype(jnp.float32) * (HEAD_DIM ** -0.5)
    p = jax.nn.softmax(s, axis=-1).astype(v.dtype)
    return jnp.einsum('bkgqs,bskd->bqkgd', p, v)


def latent_attention(q, k_all, v_all):
    B, S = q.shape[:2]
    n_blk = S // Q_BLOCK
    qb = q.reshape(B, n_blk, Q_BLOCK, N_KV_HEADS, Q_PER_KV, HEAD_DIM).transpose(1, 0, 2, 3, 4, 5)
    out = lax.map(lambda q_blk: gqa_attend(q_blk, k_all, v_all), qb)
    return out.transpose(1, 0, 2, 3, 4, 5).reshape(B, S, D_ATT)


def short_conv(x, w, b):
    y = lax.conv_general_dilated(x, w[:, None, :].astype(x.dtype), window_strides=(1,),
                                 padding=[CONV_PAD], dimension_numbers=('NWC', 'WIO', 'NWC'),
                                 feature_group_count=x.shape[-1])
    return y + b.astype(x.dtype)


def rglru_coeffs(x, wa, ba, wx, bx, lam):
    xb = x.reshape(*x.shape[:-1], LRU_BLOCKS, LRU_BLOCK_DIM)
    r = jax.nn.sigmoid(jnp.einsum('btni,nij->btnj', xb, wa.astype(jnp.float32)).reshape(x.shape) + ba.astype(jnp.float32))
    i = jax.nn.sigmoid(jnp.einsum('btni,nij->btnj', xb, wx.astype(jnp.float32)).reshape(x.shape) + bx.astype(jnp.float32))
    log_a = -LRU_C * r * jax.nn.softplus(-lam.astype(jnp.float32))
    a = jnp.exp(log_a)
    mult = jnp.sqrt(-jnp.expm1(2.0 * log_a))
    return a, mult * (i * x)


def _lin_combine(e1, e2):
    a1, b1 = e1
    a2, b2 = e2
    return a1 * a2, a2 * b1 + b2


def linear_scan(a, b, reverse):
    return lax.associative_scan(_lin_combine, (a, b), axis=1, reverse=reverse)


def setup_inputs(seed: int = 0) -> dict:
    key = jax.random.key(seed)
    ks = jax.random.split(key, 20)
    f32 = jnp.float32
    nrm = lambda k, shape, s: jax.random.normal(k, shape, f32) * s
    a0 = jax.random.uniform(ks[16], (DEPTH, 2, D_LRU), f32, 0.9, 0.999)
    a_base = a0 ** (1.0 / LRU_C)
    lam = jnp.log(a_base) - jnp.log1p(-a_base)
    return {
        "x": nrm(ks[0], (BATCH, SEQ, D_MODEL), 1.0),
        "c": nrm(ks[1], (BATCH, D_MODEL), 1.0),
        "ctx": nrm(ks[2], (BATCH, CTX_LEN, D_MODEL), 1.0),
        "c_ctx": nrm(ks[3], (D_MODEL,), 1.0),
        "w_ada": nrm(ks[4], (DEPTH, D_MODEL, 3 * D_MODEL), D_MODEL ** -0.5),
        "b_ada": nrm(ks[5], (DEPTH, 3 * D_MODEL), 0.01),
        "norm_w": 1.0 + nrm(ks[6], (DEPTH, D_MODEL), 0.01),
        "w_in": nrm(ks[7], (DEPTH, D_MODEL, D_IN), D_MODEL ** -0.5),
        "q_norm_w": 1.0 + nrm(ks[8], (DEPTH, HEAD_DIM), 0.01),
        "k_norm_w": 1.0 + nrm(ks[9], (DEPTH, HEAD_DIM), 0.01),
        "conv_w": nrm(ks[10], (DEPTH, CONV_WIDTH, D_LRU), CONV_WIDTH ** -0.5),
        "conv_b": nrm(ks[11], (DEPTH, D_LRU), 0.01),
        "lru_wa": nrm(ks[12], (DEPTH, 2, LRU_BLOCKS, LRU_BLOCK_DIM, LRU_BLOCK_DIM), LRU_BLOCK_DIM ** -0.5),
        "lru_ba": nrm(ks[13], (DEPTH, 2, D_LRU), 0.01),
        "lru_wx": nrm(ks[14], (DEPTH, 2, LRU_BLOCKS, LRU_BLOCK_DIM, LRU_BLOCK_DIM), LRU_BLOCK_DIM ** -0.5),
        "lru_bx": nrm(ks[15], (DEPTH, 2, D_LRU), 0.01),
        "lru_lambda": lam,
        "out_norm_att": 1.0 + nrm(ks[17], (DEPTH, D_ATT), 0.01),
        "out_norm_lru": 1.0 + nrm(ks[18], (DEPTH, D_LRU), 0.01),
        "w_out": nrm(ks[19], (DEPTH, D_MIX, D_MODEL), D_MIX ** -0.5),
    }


def reference(x, c, ctx, c_ctx, w_ada, b_ada, norm_w, w_in, q_norm_w, k_norm_w, conv_w, conv_b,
              lru_wa, lru_ba, lru_wx, lru_bx, lru_lambda, out_norm_att, out_norm_lru, w_out):
    B, S, _ = x.shape
    C = ctx.shape[1]
    cos, sin = rope_tables(S)
    for l in range(DEPTH):
        last = l + 1 == DEPTH
        shift_x, scale_x, gate_x = modulation(c, w_ada[l], b_ada[l])
        shift_c, scale_c, gate_c = modulation(c_ctx, w_ada[l], b_ada[l])
        h_x = rmsnorm(x, norm_w[l]) * (1.0 + scale_x[:, None]) + shift_x[:, None]
        h_c = rmsnorm(ctx, norm_w[l]) * (1.0 + scale_c) + shift_c
        q_x, k_x, v_x, ga_x, xl_x, gl_x = jnp.split(h_x @ w_in[l], SPLITS, axis=-1)
        q_c, k_c, v_c, ga_c, xl_c, gl_c = jnp.split(h_c @ w_in[l], SPLITS, axis=-1)

        q_x, k_x, v_x = qkv_heads(q_x, k_x, v_x, q_norm_w[l], k_norm_w[l])
        q_c, k_c, v_c = qkv_heads(q_c, k_c, v_c, q_norm_w[l], k_norm_w[l])
        q_x = apply_rope(q_x, cos, sin)
        k_x = apply_rope(k_x, cos, sin)
        k_all = jnp.concatenate([k_x, k_c], axis=1)
        v_all = jnp.concatenate([v_x, v_c], axis=1)
        att_x = latent_attention(q_x, k_all, v_all)

        u_x = short_conv(xl_x, conv_w[l], conv_b[l]).astype(jnp.float32)
        u_c = short_conv(xl_c, conv_w[l], conv_b[l]).astype(jnp.float32)
        lru_x = jnp.zeros(u_x.shape, jnp.float32)
        lru_c = jnp.zeros(u_c.shape, jnp.float32)
        for d in range(2):
            rev = d == 1
            p = (lru_wa[l, d], lru_ba[l, d], lru_wx[l, d], lru_bx[l, d], lru_lambda[l, d])
            a_c, b_c = rglru_coeffs(u_c, *p)
            _, h_c_states = linear_scan(a_c, b_c, rev)
            h0 = h_c_states[:, 0] if rev else h_c_states[:, -1]
            a_l, b_l = rglru_coeffs(u_x, *p)
            a_cum, h_l = linear_scan(a_l, b_l, rev)
            lru_x = lru_x + h_l + a_cum * h0[:, None]
            lru_c = lru_c + h_c_states
        lru_x = lru_x.astype(x.dtype)

        mix_x = jnp.concatenate([rmsnorm(att_x, out_norm_att[l]) * jax.nn.silu(ga_x),
                                 rmsnorm(lru_x, out_norm_lru[l]) * jax.nn.silu(gl_x)], axis=-1)
        x_new = x + gate_x[:, None] * (mix_x @ w_out[l])

        if not last:
            qc = q_c.reshape(B, C, N_KV_HEADS, Q_PER_KV, HEAD_DIM)
            att_c = gqa_attend(qc, k_c, v_c).reshape(B, C, D_ATT)
            mix_c = jnp.concatenate([rmsnorm(att_c, out_norm_att[l]) * jax.nn.silu(ga_c),
                                     rmsnorm(lru_c.astype(ctx.dtype), out_norm_lru[l]) * jax.nn.silu(gl_c)], axis=-1)
            ctx = ctx + gate_c * (mix_c @ w_out[l])
        x = x_new
    return x
```

```python
import functools
import math

import jax
import jax.numpy as jnp
from jax import lax
from jax.experimental import pallas as pl
from jax.experimental.pallas import tpu as pltpu

F32 = jnp.float32
BF16 = jnp.bfloat16

HEAD_DIM = 128
GRID_W = 64
Q_PER_KV = 4
LRU_BLOCK_DIM = 128
CONV_WIDTH = 4
CONV_LEFT = 2
LRU_C = 8.0
ROPE_THETA = 10000.0
EPS = 1e-6

SUBLANES = 8
MOD_ROWS = 8
VMEM_LIMIT = 56 * 1024 * 1024


def _params(*sem):
    return pltpu.CompilerParams(dimension_semantics=sem, vmem_limit_bytes=VMEM_LIMIT)


def _silu(x):
    return x * jax.nn.sigmoid(x)


def _rms(x, w):
    ms = jnp.mean(x * x, axis=-1, keepdims=True)
    return x * lax.rsqrt(ms + EPS) * w


def _mod_kernel(c_ref, w_ref, b_ref, o_ref):
    s = _silu(c_ref[...]).astype(BF16)
    o_ref[...] = jnp.dot(s, w_ref[...].astype(BF16), preferred_element_type=F32) + b_ref[...]


def _modulation(cvec, w_ada, b_ada, tn=512):
    d, n = w_ada.shape
    return pl.pallas_call(
        _mod_kernel,
        grid=(n // tn,),
        in_specs=[pl.BlockSpec((MOD_ROWS, d), lambda j: (0, 0)),
                  pl.BlockSpec((d, tn), lambda j: (0, j)),
                  pl.BlockSpec((1, tn), lambda j: (0, j))],
        out_specs=pl.BlockSpec((MOD_ROWS, tn), lambda j: (0, j)),
        out_shape=jax.ShapeDtypeStruct((MOD_ROWS, n), F32),
        compiler_params=_params("parallel"),
        name="modulation",
    )(cvec, w_ada, b_ada)


def _prenorm_kernel(x_ref, c_ref, nw_ref, mod_ref, o_ref, *, d, n_x_tiles, tiles_per_batch, ctx_row):
    t = pl.program_id(0)

    def emit(src_ref, row):
        y = _rms(src_ref[...], nw_ref[...])
        shift = mod_ref[pl.ds(row, 1), 0:d]
        scale = mod_ref[pl.ds(row, 1), d:2 * d]
        o_ref[...] = (y * (1.0 + scale) + shift).astype(BF16)

    @pl.when(t < n_x_tiles)
    def _():
        emit(x_ref, t // tiles_per_batch)

    @pl.when(t >= n_x_tiles)
    def _():
        emit(c_ref, ctx_row)


def _prenorm(x2, c2, norm_w, mod, seq, tr=256):
    mx, d = x2.shape
    mc = c2.shape[0]
    nx, nc = mx // tr, mc // tr
    kern = functools.partial(_prenorm_kernel, d=d, n_x_tiles=nx, tiles_per_batch=seq // tr,
                             ctx_row=mx // seq)
    return pl.pallas_call(
        kern,
        grid=(nx + nc,),
        in_specs=[pl.BlockSpec((tr, d), lambda t: (jnp.minimum(t, nx - 1), 0)),
                  pl.BlockSpec((tr, d), lambda t: (jnp.maximum(t - nx, 0), 0)),
                  pl.BlockSpec((1, d), lambda t: (0, 0)),
                  pl.BlockSpec(mod.shape, lambda t: (0, 0))],
        out_specs=pl.BlockSpec((tr, d), lambda t: (t, 0)),
        out_shape=jax.ShapeDtypeStruct((mx + mc, d), BF16),
        compiler_params=_params("parallel"),
        name="prenorm",
    )(x2, c2, norm_w, mod)


def _matmul_kernel(a_ref, b_ref, o_ref):
    o_ref[...] = jnp.dot(a_ref[...], b_ref[...], preferred_element_type=F32)


def _in_proj(h, w, tm=512, tn=1024):
    m, k = h.shape
    n = w.shape[1]
    return pl.pallas_call(
        _matmul_kernel,
        grid=(n // tn, m // tm),
        in_specs=[pl.BlockSpec((tm, k), lambda j, i: (i, 0)),
                  pl.BlockSpec((k, tn), lambda j, i: (0, j))],
        out_specs=pl.BlockSpec((tm, tn), lambda j, i: (i, j)),
        out_shape=jax.ShapeDtypeStruct((m, n), F32),
        compiler_params=_params("parallel", "parallel"),
        name="in_proj",
    )(h, w)


def _rope(y, cos, sin_signed):
    lane = lax.broadcasted_iota(jnp.int32, y.shape, 1)
    first_half = (lane & (HEAD_DIM // 4)) == 0
    partner = jnp.where(first_half, pltpu.roll(y, HEAD_DIM - HEAD_DIM // 4, 1),
                        pltpu.roll(y, HEAD_DIM // 4, 1))
    return y * cos + partner * sin_signed


def _qkv_latent_kernel(q_ref, k_ref, v_ref, cos_ref, sin_ref, qw_ref, kw_ref, qo_ref, ko_ref, vo_ref):
    cos = cos_ref[...]
    sin = sin_ref[...]
    for h in range(q_ref.shape[1] // HEAD_DIM):
        sl = slice(h * HEAD_DIM, (h + 1) * HEAD_DIM)
        qo_ref[:, sl] = _rope(_rms(q_ref[:, sl], qw_ref[...]), cos, sin).astype(BF16)
    for h in range(k_ref.shape[1] // HEAD_DIM):
        sl = slice(h * HEAD_DIM, (h + 1) * HEAD_DIM)
        ko_ref[:, sl] = _rope(_rms(k_ref[:, sl], kw_ref[...]), cos, sin).astype(BF16)
    vo_ref[...] = v_ref[...].astype(BF16)


def _kv_ctx_kernel(k_ref, v_ref, kw_ref, ko_ref, vo_ref):
    for h in range(k_ref.shape[1] // HEAD_DIM):
        sl = slice(h * HEAD_DIM, (h + 1) * HEAD_DIM)
        ko_ref[:, sl] = _rms(k_ref[:, sl], kw_ref[...]).astype(BF16)
    vo_ref[...] = v_ref[...].astype(BF16)


def _qkv_latent(proj, cos, sin, qw, kw, n_lat, seq, d_att, d_kv, tr=256):
    kb = d_att // d_kv
    spt = seq // tr
    return pl.pallas_call(
        _qkv_latent_kernel,
        grid=(n_lat // tr,),
        in_specs=[pl.BlockSpec((tr, d_att), lambda t: (t, 0)),
                  pl.BlockSpec((tr, d_kv), lambda t: (t, kb)),
                  pl.BlockSpec((tr, d_kv), lambda t: (t, kb + 1)),
                  pl.BlockSpec((tr, HEAD_DIM), lambda t: (t % spt, 0)),
                  pl.BlockSpec((tr, HEAD_DIM), lambda t: (t % spt, 0)),
                  pl.BlockSpec((1, HEAD_DIM), lambda t: (0, 0)),
                  pl.BlockSpec((1, HEAD_DIM), lambda t: (0, 0))],
        out_specs=[pl.BlockSpec((tr, d_att), lambda t: (t, 0)),
                   pl.BlockSpec((tr, d_kv), lambda t: (t, 0)),
                   pl.BlockSpec((tr, d_kv), lambda t: (t, 0))],
        out_shape=[jax.ShapeDtypeStruct((n_lat, d_att), BF16),
                   jax.ShapeDtypeStruct((n_lat, d_kv), BF16),
                   jax.ShapeDtypeStruct((n_lat, d_kv), BF16)],
        compiler_params=_params("parallel"),
        name="qkv_latent",
    )(proj, proj, proj, cos, sin, qw, kw)


def _kv_ctx(proj, kw, n_lat, n_ctx, d_att, d_kv, tr=256):
    kb = d_att // d_kv
    r0 = n_lat // tr
    return pl.pallas_call(
        _kv_ctx_kernel,
        grid=(n_ctx // tr,),
        in_specs=[pl.BlockSpec((tr, d_kv), lambda t: (r0 + t, kb)),
                  pl.BlockSpec((tr, d_kv), lambda t: (r0 + t, kb + 1)),
                  pl.BlockSpec((1, HEAD_DIM), lambda t: (0, 0))],
        out_specs=[pl.BlockSpec((tr, d_kv), lambda t: (t, 0)),
                   pl.BlockSpec((tr, d_kv), lambda t: (t, 0))],
        out_shape=[jax.ShapeDtypeStruct((n_ctx, d_kv), BF16),
                   jax.ShapeDtypeStruct((n_ctx, d_kv), BF16)],
        compiler_params=_params("parallel"),
        name="kv_ctx",
    )(proj, proj, kw)


def _attn_kernel(q_ref, kl_ref, vl_ref, kc_ref, vc_ref, o_ref):
    tq = q_ref.shape[0]
    scale = HEAD_DIM ** -0.5
    nt = (((1,), (1,)), ((), ()))
    for kv in range(kl_ref.shape[1] // HEAD_DIM):
        ks = slice(kv * HEAD_DIM, (kv + 1) * HEAD_DIM)
        heads = [slice((kv * Q_PER_KV + g) * HEAD_DIM, (kv * Q_PER_KV + g + 1) * HEAD_DIM)
                 for g in range(Q_PER_KV)]
        q4 = jnp.concatenate([q_ref[:, hs] for hs in heads], axis=0)
        s_l = lax.dot_general(q4, kl_ref[:, ks], nt, preferred_element_type=F32) * scale
        s_c = lax.dot_general(q4, kc_ref[:, ks], nt, preferred_element_type=F32) * scale
        m = jnp.maximum(jnp.max(s_l, axis=-1, keepdims=True), jnp.max(s_c, axis=-1, keepdims=True))
        p_l = jnp.exp(s_l - m)
        p_c = jnp.exp(s_c - m)
        denom = jnp.sum(p_l, axis=-1, keepdims=True) + jnp.sum(p_c, axis=-1, keepdims=True)
        o = jnp.dot(p_l.astype(BF16), vl_ref[:, ks], preferred_element_type=F32)
        o = o + jnp.dot(p_c.astype(BF16), vc_ref[:, ks], preferred_element_type=F32)
        o = o / denom
        for g, hs in enumerate(heads):
            o_ref[:, hs] = o[g * tq:(g + 1) * tq, :]


def _attention(qn, kl, vl, kc, vc, tq=128):
    n_lat, d_att = qn.shape
    b, seq, d_kv = kl.shape
    n_ctx = kc.shape[1]
    qt = seq // tq
    return pl.pallas_call(
        _attn_kernel,
        grid=(b, qt),
        in_specs=[pl.BlockSpec((tq, d_att), lambda bi, i: (bi * qt + i, 0)),
                  pl.BlockSpec((None, seq, d_kv), lambda bi, i: (bi, 0, 0)),
                  pl.BlockSpec((None, seq, d_kv), lambda bi, i: (bi, 0, 0)),
                  pl.BlockSpec((None, n_ctx, d_kv), lambda bi, i: (bi, 0, 0)),
                  pl.BlockSpec((None, n_ctx, d_kv), lambda bi, i: (bi, 0, 0))],
        out_specs=pl.BlockSpec((tq, d_att), lambda bi, i: (bi * qt + i, 0)),
        out_shape=jax.ShapeDtypeStruct((n_lat, d_att), F32),
        compiler_params=_params("parallel", "parallel"),
        name="attention",
    )(qn, kl, vl, kc, vc)


LRU_CHUNK = 256
CONV_PAD_ROWS = 8


def _softplus(z):
    return jnp.maximum(z, 0.0) + jnp.log1p(jnp.exp(-jnp.abs(z)))


def _lru_kernel(xl_ref, xc_ref, cw_ref, cb_ref, w_ref, bias_ref, lam_ref, o_ref,
                xp_ref, u_ref, a_ref, b_ref, *, seq, n_ctx):
    cg = o_ref.shape[1]
    n_blk = cg // LRU_BLOCK_DIM
    zeros_pad = jnp.zeros((CONV_PAD_ROWS, cg), F32)

    def conv_into(src_ref, n_rows, u_row0):
        xp_ref[0:CONV_PAD_ROWS, :] = zeros_pad
        xp_ref[CONV_PAD_ROWS:CONV_PAD_ROWS + n_rows, :] = src_ref[...]
        xp_ref[CONV_PAD_ROWS + n_rows:2 * CONV_PAD_ROWS + n_rows, :] = zeros_pad
        for c in range(n_rows // LRU_CHUNK):
            acc = jnp.zeros((LRU_CHUNK, cg), F32) + cb_ref[...]
            for j in range(CONV_WIDTH):
                r0 = CONV_PAD_ROWS + c * LRU_CHUNK + j - CONV_LEFT
                acc = acc + cw_ref[j:j + 1, :] * xp_ref[r0:r0 + LRU_CHUNK, :]
            u_ref[u_row0 + c * LRU_CHUNK:u_row0 + (c + 1) * LRU_CHUNK, :] = acc

    conv_into(xc_ref, n_ctx, 0)
    conv_into(xl_ref, seq, n_ctx)

    def gates(u_row0, d):
        u = u_ref[pl.ds(pl.multiple_of(u_row0, SUBLANES), LRU_CHUNK), :]
        ub = u.astype(BF16)
        for n in range(n_blk):
            sl = slice(n * LRU_BLOCK_DIM, (n + 1) * LRU_BLOCK_DIM)
            pre = jnp.dot(ub[:, sl], w_ref[d, n], preferred_element_type=F32) + bias_ref[d, n]
            r = jax.nn.sigmoid(pre[:, :LRU_BLOCK_DIM])
            i = jax.nn.sigmoid(pre[:, LRU_BLOCK_DIM:])
            neg_log_a = LRU_C * r * _softplus(-lam_ref[d:d + 1, sl])
            a = jnp.exp(-neg_log_a)
            a_ref[:, sl] = a
            b_ref[:, sl] = jnp.sqrt(jnp.tanh(neg_log_a) * (1.0 + a * a)) * (i * u[:, sl])

    row = lax.broadcasted_iota(jnp.int32, (SUBLANES, cg), 0)

    def scan_chunk(carry, reverse, out_row0, accumulate):
        groups = LRU_CHUNK // SUBLANES

        def body(g, carry):
            gi = (groups - 1 - g) if reverse else g
            r0 = pl.multiple_of(gi * SUBLANES, SUBLANES)
            a = a_ref[pl.ds(r0, SUBLANES), :]
            bv = b_ref[pl.ds(r0, SUBLANES), :]
            for s in (1, 2, 4):
                if reverse:
                    keep = row < SUBLANES - s
                    shift = SUBLANES - s
                else:
                    keep = row >= s
                    shift = s
                a_prev = jnp.where(keep, pltpu.roll(a, shift, 0), 1.0)
                b_prev = jnp.where(keep, pltpu.roll(bv, shift, 0), 0.0)
                bv = bv + a * b_prev
                a = a * a_prev
            h = bv + a * carry
            if out_row0 is not None:
                orow = pl.multiple_of(out_row0 + r0, SUBLANES)
                if accumulate:
                    o_ref[pl.ds(orow, SUBLANES), :] += h
                else:
                    o_ref[pl.ds(orow, SUBLANES), :] = h
            return h[0:1, :] if reverse else h[SUBLANES - 1:SUBLANES, :]

        return lax.fori_loop(0, groups, body, carry)

    n_chunks = seq // LRU_CHUNK
    for d in range(2):
        reverse = d == 1
        gates(0, d)
        carry = scan_chunk(jnp.zeros((1, cg), F32), reverse, None, False)

        def chunk_body(c, carry, d=d, reverse=reverse):
            ci = (n_chunks - 1 - c) if reverse else c
            lat0 = pl.multiple_of(ci * LRU_CHUNK, LRU_CHUNK)
            gates(n_ctx + lat0, d)
            return scan_chunk(carry, reverse, lat0, reverse)

        lax.fori_loop(0, n_chunks, chunk_body, carry)


def _lru(proj, conv_w, conv_b, w_cat, bias_cat, lam, b, seq, n_ctx, xl_col0, d_lru, cg=512):
    n_lat = b * seq
    col_blk0 = xl_col0 // cg
    n_blk = cg // LRU_BLOCK_DIM
    kern = functools.partial(_lru_kernel, seq=seq, n_ctx=n_ctx)
    ctx_blk0 = n_lat // n_ctx
    return pl.pallas_call(
        kern,
        grid=(b, d_lru // cg),
        in_specs=[pl.BlockSpec((seq, cg), lambda bi, g: (bi, col_blk0 + g)),
                  pl.BlockSpec((n_ctx, cg), lambda bi, g: (ctx_blk0 + bi, col_blk0 + g)),
                  pl.BlockSpec((CONV_WIDTH, cg), lambda bi, g: (0, g)),
                  pl.BlockSpec((1, cg), lambda bi, g: (0, g)),
                  pl.BlockSpec((2, n_blk, LRU_BLOCK_DIM, 2 * LRU_BLOCK_DIM), lambda bi, g: (0, g, 0, 0)),
                  pl.BlockSpec((2, n_blk, 1, 2 * LRU_BLOCK_DIM), lambda bi, g: (0, g, 0, 0)),
                  pl.BlockSpec((2, cg), lambda bi, g: (0, g))],
        out_specs=pl.BlockSpec((seq, cg), lambda bi, g: (bi, g)),
        out_shape=jax.ShapeDtypeStruct((n_lat, d_lru), F32),
        scratch_shapes=[pltpu.VMEM((seq + 2 * CONV_PAD_ROWS, cg), F32),
                        pltpu.VMEM((n_ctx + seq, cg), F32),
                        pltpu.VMEM((LRU_CHUNK, cg), F32),
                        pltpu.VMEM((LRU_CHUNK, cg), F32)],
        compiler_params=_params("parallel", "parallel"),
        name="rglru",
    )(proj, proj, conv_w, conv_b, w_cat, bias_cat, lam)


def _merge_kernel(att_ref, lru_ref, ga0_ref, ga1_ref, gl0_ref, gl1_ref, wa_ref, wl_ref, o_ref):
    d_att = att_ref.shape[1]
    half = ga0_ref.shape[1]
    ya = _rms(att_ref[...], wa_ref[...])
    yl = _rms(lru_ref[...], wl_ref[...])
    o_ref[:, 0:half] = (ya[:, 0:half] * _silu(ga0_ref[...])).astype(BF16)
    o_ref[:, half:d_att] = (ya[:, half:] * _silu(ga1_ref[...])).astype(BF16)
    o_ref[:, d_att:d_att + half] = (yl[:, 0:half] * _silu(gl0_ref[...])).astype(BF16)
    o_ref[:, d_att + half:] = (yl[:, half:] * _silu(gl1_ref[...])).astype(BF16)


def _merge(att, lru, proj, wa, wl, ga_col0, gl_col0, tr=256):
    n_lat, d_att = att.shape
    d_lru = lru.shape[1]
    half = d_att // 2
    ga_b, gl_b = ga_col0 // half, gl_col0 // half
    return pl.pallas_call(
        _merge_kernel,
        grid=(n_lat // tr,),
        in_specs=[pl.BlockSpec((tr, d_att), lambda t: (t, 0)),
                  pl.BlockSpec((tr, d_lru), lambda t: (t, 0)),
                  pl.BlockSpec((tr, half), lambda t: (t, ga_b)),
                  pl.BlockSpec((tr, half), lambda t: (t, ga_b + 1)),
                  pl.BlockSpec((tr, half), lambda t: (t, gl_b)),
                  pl.BlockSpec((tr, half), lambda t: (t, gl_b + 1)),
                  pl.BlockSpec((1, d_att), lambda t: (0, 0)),
                  pl.BlockSpec((1, d_lru), lambda t: (0, 0))],
        out_specs=pl.BlockSpec((tr, d_att + d_lru), lambda t: (t, 0)),
        out_shape=jax.ShapeDtypeStruct((n_lat, d_att + d_lru), BF16),
        compiler_params=_params("parallel"),
        name="merge",
    )(att, lru, proj, proj, proj, proj, wa, wl)


def _out_kernel(mix_ref, w_ref, x_ref, gate_ref, o_ref, *, tiles_per_batch):
    bi = pl.program_id(1) // tiles_per_batch
    gate = gate_ref[pl.ds(bi, 1), :]
    o_ref[...] = x_ref[...] + gate * jnp.dot(mix_ref[...], w_ref[...], preferred_element_type=F32)


def _out_proj(mix, w, x2, mod, seq, gate_col0, tm=512, tn=1024):
    m, k = mix.shape
    n = w.shape[1]
    gb = gate_col0 // tn
    kern = functools.partial(_out_kernel, tiles_per_batch=seq // tm)
    return pl.pallas_call(
        kern,
        grid=(n // tn, m // tm),
        in_specs=[pl.BlockSpec((tm, k), lambda j, i: (i, 0)),
                  pl.BlockSpec((k, tn), lambda j, i: (0, j)),
                  pl.BlockSpec((tm, tn), lambda j, i: (i, j)),
                  pl.BlockSpec((MOD_ROWS, tn), lambda j, i: (0, gb + j))],
        out_specs=pl.BlockSpec((tm, tn), lambda j, i: (i, j)),
        out_shape=jax.ShapeDtypeStruct((m, n), F32),
        compiler_params=_params("parallel", "parallel"),
        name="out_proj",
    )(mix, w, x2, mod)


def _rope_tables(seq):
    pos = jnp.arange(seq)
    row = (pos // GRID_W).astype(F32)
    col = (pos % GRID_W).astype(F32)
    n_freq = HEAD_DIM // 4
    freqs = ROPE_THETA ** (-jnp.arange(n_freq, dtype=F32) / n_freq)
    ang_r = row[:, None] * freqs
    ang_c = col[:, None] * freqs
    cos = jnp.concatenate([jnp.cos(ang_r), jnp.cos(ang_r), jnp.cos(ang_c), jnp.cos(ang_c)], axis=1)
    sin = jnp.concatenate([-jnp.sin(ang_r), jnp.sin(ang_r), -jnp.sin(ang_c), jnp.sin(ang_c)], axis=1)
    return cos, sin


def kernel(x, c, ctx, c_ctx, w_ada, b_ada, norm_w, w_in, q_norm_w, k_norm_w, conv_w, conv_b,
           lru_wa, lru_ba, lru_wx, lru_bx, lru_lambda, out_norm_att, out_norm_lru, w_out):
    assert w_ada.shape[0] == 1, "single-layer kernel: only the latent stream is produced"
    b, seq, d = x.shape
    n_ctx = ctx.shape[1]
    d_att = out_norm_att.shape[1]
    d_lru = out_norm_lru.shape[1]
    d_kv = (w_in.shape[2] - 2 * d_att - 2 * d_lru) // 2
    n_lat = b * seq
    assert b + 1 <= MOD_ROWS
    ga_col0 = d_att + 2 * d_kv
    xl_col0 = ga_col0 + d_att
    gl_col0 = xl_col0 + d_lru

    x2 = x.reshape(n_lat, d)
    c2 = ctx.reshape(b * n_ctx, d)

    cvec = jnp.concatenate([c, c_ctx[None, :], jnp.zeros((MOD_ROWS - b - 1, d), F32)], axis=0)
    mod = _modulation(cvec, w_ada[0], b_ada)

    h = _prenorm(x2, c2, norm_w, mod, seq)
    proj = _in_proj(h, w_in[0].astype(BF16))

    cos, sin = _rope_tables(seq)
    qn, kl, vl = _qkv_latent(proj, cos, sin, q_norm_w, k_norm_w, n_lat, seq, d_att, d_kv)
    kc, vc = _kv_ctx(proj, k_norm_w, n_lat, b * n_ctx, d_att, d_kv)
    att = _attention(qn, kl.reshape(b, seq, d_kv), vl.reshape(b, seq, d_kv),
                     kc.reshape(b, n_ctx, d_kv), vc.reshape(b, n_ctx, d_kv))

    n_blocks = d_lru // LRU_BLOCK_DIM
    w_cat = jnp.concatenate([lru_wa[0], lru_wx[0]], axis=-1).astype(BF16)
    bias_cat = jnp.concatenate([lru_ba[0].reshape(2, n_blocks, 1, LRU_BLOCK_DIM),
                                lru_bx[0].reshape(2, n_blocks, 1, LRU_BLOCK_DIM)], axis=-1)
    lru = _lru(proj, conv_w[0], conv_b, w_cat, bias_cat, lru_lambda[0], b, seq, n_ctx, xl_col0, d_lru)

    mix = _merge(att, lru, proj, out_norm_att, out_norm_lru, ga_col0, gl_col0)
    out = _out_proj(mix, w_out[0].astype(BF16), x2, mod, seq, 2 * d)
    return out.reshape(b, seq, d)
```

```python
import functools
import math

import jax
import jax.numpy as jnp
from jax import lax
from jax.experimental import pallas as pl
from jax.experimental.pallas import tpu as pltpu

F32 = jnp.float32
BF16 = jnp.bfloat16

HEAD_DIM = 128
GRID_W = 64
Q_PER_KV = 4
LRU_BLOCK_DIM = 128
CONV_WIDTH = 4
CONV_LEFT = 2
LRU_C = 8.0
ROPE_THETA = 10000.0
EPS = 1e-6

SUBLANES = 8
MOD_ROWS = 8
VMEM_LIMIT = 56 * 1024 * 1024


def _params(*sem):
    return pltpu.CompilerParams(dimension_semantics=sem, vmem_limit_bytes=VMEM_LIMIT)


def _silu(x):
    return x * jax.nn.sigmoid(x)


def _rms(x, w):
    ms = jnp.mean(x * x, axis=-1, keepdims=True)
    return x * lax.rsqrt(ms + EPS) * w


def _mod_kernel(c_ref, w_ref, b_ref, o_ref):
    s = _silu(c_ref[...]).astype(BF16)
    o_ref[...] = jnp.dot(s, w_ref[...].astype(BF16), preferred_element_type=F32) + b_ref[...]


def _modulation(cvec, w_ada, b_ada, tn=512):
    d, n = w_ada.shape
    return pl.pallas_call(
        _mod_kernel,
        grid=(n // tn,),
        in_specs=[pl.BlockSpec((MOD_ROWS, d), lambda j: (0, 0)),
                  pl.BlockSpec((d, tn), lambda j: (0, j)),
                  pl.BlockSpec((1, tn), lambda j: (0, j))],
        out_specs=pl.BlockSpec((MOD_ROWS, tn), lambda j: (0, j)),
        out_shape=jax.ShapeDtypeStruct((MOD_ROWS, n), F32),
        compiler_params=_params("parallel"),
        name="modulation",
    )(cvec, w_ada, b_ada)


def _prenorm_kernel(x_ref, c_ref, nw_ref, mod_ref, o_ref, *, d, n_x_tiles, tiles_per_batch, ctx_row):
    t = pl.program_id(0)

    def emit(src_ref, row):
        y = _rms(src_ref[...], nw_ref[...])
        shift = mod_ref[pl.ds(row, 1), 0:d]
        scale = mod_ref[pl.ds(row, 1), d:2 * d]
        o_ref[...] = (y * (1.0 + scale) + shift).astype(BF16)

    @pl.when(t < n_x_tiles)
    def _():
        emit(x_ref, t // tiles_per_batch)

    @pl.when(t >= n_x_tiles)
    def _():
        emit(c_ref, ctx_row)


def _prenorm(x2, c2, norm_w, mod, seq, tr=256):
    mx, d = x2.shape
    mc = c2.shape[0]
    nx, nc = mx // tr, mc // tr
    kern = functools.partial(_prenorm_kernel, d=d, n_x_tiles=nx, tiles_per_batch=seq // tr,
                             ctx_row=mx // seq)
    return pl.pallas_call(
        kern,
        grid=(nx + nc,),
        in_specs=[pl.BlockSpec((tr, d), lambda t: (jnp.minimum(t, nx - 1), 0)),
                  pl.BlockSpec((tr, d), lambda t: (jnp.maximum(t - nx, 0), 0)),
                  pl.BlockSpec((1, d), lambda t: (0, 0)),
                  pl.BlockSpec(mod.shape, lambda t: (0, 0))],
        out_specs=pl.BlockSpec((tr, d), lambda t: (t, 0)),
        out_shape=jax.ShapeDtypeStruct((mx + mc, d), BF16),
        compiler_params=_params("parallel"),
        name="prenorm",
    )(x2, c2, norm_w, mod)


def _matmul_kernel(a_ref, b_ref, o_ref):
    o_ref[...] = jnp.dot(a_ref[...], b_ref[...], preferred_element_type=F32)


def _in_proj(h, w, tm=512, tn=1024):
    m, k = h.shape
    n = w.shape[1]
    return pl.pallas_call(
        _matmul_kernel,
        grid=(n // tn, m // tm),
        in_specs=[pl.BlockSpec((tm, k), lambda j, i: (i, 0)),
                  pl.BlockSpec((k, tn), lambda j, i: (0, j))],
        out_specs=pl.BlockSpec((tm, tn), lambda j, i: (i, j)),
        out_shape=jax.ShapeDtypeStruct((m, n), F32),
        compiler_params=_params("parallel", "parallel"),
        name="in_proj",
    )(h, w)


def _rope(y, cos, sin_signed):
    lane = lax.broadcasted_iota(jnp.int32, y.shape, 1)
    first_half = (lane & (HEAD_DIM // 4)) == 0
    partner = jnp.where(first_half, pltpu.roll(y, HEAD_DIM - HEAD_DIM // 4, 1),
                        pltpu.roll(y, HEAD_DIM // 4, 1))
    return y * cos + partner * sin_signed


def _qkv_latent_kernel(q_ref, k_ref, v_ref, cos_ref, sin_ref, qw_ref, kw_ref, qo_ref, ko_ref, vo_ref):
    cos = cos_ref[...]
    sin = sin_ref[...]
    for h in range(q_ref.shape[1] // HEAD_DIM):
        sl = slice(h * HEAD_DIM, (h + 1) * HEAD_DIM)
        qo_ref[:, sl] = _rope(_rms(q_ref[:, sl], qw_ref[...]), cos, sin).astype(BF16)
    for h in range(k_ref.shape[1] // HEAD_DIM):
        sl = slice(h * HEAD_DIM, (h + 1) * HEAD_DIM)
        ko_ref[:, sl] = _rope(_rms(k_ref[:, sl], kw_ref[...]), cos, sin).astype(BF16)
    vo_ref[...] = v_ref[...].astype(BF16)


def _kv_ctx_kernel(k_ref, v_ref, kw_ref, ko_ref, vo_ref):
    for h in range(k_ref.shape[1] // HEAD_DIM):
        sl = slice(h * HEAD_DIM, (h + 1) * HEAD_DIM)
        ko_ref[:, sl] = _rms(k_ref[:, sl], kw_ref[...]).astype(BF16)
    vo_ref[...] = v_ref[...].astype(BF16)


def _qkv_latent(proj, cos, sin, qw, kw, n_lat, seq, d_att, d_kv, tr=256):
    kb = d_att // d_kv
    spt = seq // tr
    return pl.pallas_call(
        _qkv_latent_kernel,
        grid=(n_lat // tr,),
        in_specs=[pl.BlockSpec((tr, d_att), lambda t: (t, 0)),
                  pl.BlockSpec((tr, d_kv), lambda t: (t, kb)),
                  pl.BlockSpec((tr, d_kv), lambda t: (t, kb + 1)),
                  pl.BlockSpec((tr, HEAD_DIM), lambda t: (t % spt, 0)),
                  pl.BlockSpec((tr, HEAD_DIM), lambda t: (t % spt, 0)),
                  pl.BlockSpec((1, HEAD_DIM), lambda t: (0, 0)),
                  pl.BlockSpec((1, HEAD_DIM), lambda t: (0, 0))],
        out_specs=[pl.BlockSpec((tr, d_att), lambda t: (t, 0)),
                   pl.BlockSpec((tr, d_kv), lambda t: (t, 0)),
                   pl.BlockSpec((tr, d_kv), lambda t: (t, 0))],
        out_shape=[jax.ShapeDtypeStruct((n_lat, d_att), BF16),
                   jax.ShapeDtypeStruct((n_lat, d_kv), BF16),
                   jax.ShapeDtypeStruct((n_lat, d_kv), BF16)],
        compiler_params=_params("parallel"),
        name="qkv_latent",
    )(proj, proj, proj, cos, sin, qw, kw)


def _kv_ctx(proj, kw, n_lat, n_ctx, d_att, d_kv, tr=256):
    kb = d_att // d_kv
    r0 = n_lat // tr
    return pl.pallas_call(
        _kv_ctx_kernel,
        grid=(n_ctx // tr,),
        in_specs=[pl.BlockSpec((tr, d_kv), lambda t: (r0 + t, kb)),
                  pl.BlockSpec((tr, d_kv), lambda t: (r0 + t, kb + 1)),
                  pl.BlockSpec((1, HEAD_DIM), lambda t: (0, 0))],
        out_specs=[pl.BlockSpec((tr, d_kv), lambda t: (t, 0)),
                   pl.BlockSpec((tr, d_kv), lambda t: (t, 0))],
        out_shape=[jax.ShapeDtypeStruct((n_ctx, d_kv), BF16),
                   jax.ShapeDtypeStruct((n_ctx, d_kv), BF16)],
        compiler_params=_params("parallel"),
        name="kv_ctx",
    )(proj, proj, kw)


def _attn_kernel(q_ref, kl_ref, vl_ref, kc_ref, vc_ref, o_ref):
    tq = q_ref.shape[0]
    scale = HEAD_DIM ** -0.5
    nt = (((1,), (1,)), ((), ()))
    for kv in range(kl_ref.shape[1] // HEAD_DIM):
        ks = slice(kv * HEAD_DIM, (kv + 1) * HEAD_DIM)
        heads = [slice((kv * Q_PER_KV + g) * HEAD_DIM, (kv * Q_PER_KV + g + 1) * HEAD_DIM)
                 for g in range(Q_PER_KV)]
        q4 = jnp.concatenate([q_ref[:, hs] for hs in heads], axis=0)
        s_l = lax.dot_general(q4, kl_ref[:, ks], nt, preferred_element_type=F32) * scale
        s_c = lax.dot_general(q4, kc_ref[:, ks], nt, preferred_element_type=F32) * scale
        m = jnp.maximum(jnp.max(s_l, axis=-1, keepdims=True), jnp.max(s_c, axis=-1, keepdims=True))
        p_l = jnp.exp(s_l - m)
        p_c = jnp.exp(s_c - m)
        denom = jnp.sum(p_l, axis=-1, keepdims=True) + jnp.sum(p_c, axis=-1, keepdims=True)
        o = jnp.dot(p_l.astype(BF16), vl_ref[:, ks], preferred_element_type=F32)
        o = o + jnp.dot(p_c.astype(BF16), vc_ref[:, ks], preferred_element_type=F32)
        o = o / denom
        for g, hs in enumerate(heads):
            o_ref[:, hs] = o[g * tq:(g + 1) * tq, :]


def _attention(qn, kl, vl, kc, vc, tq=128):
    n_lat, d_att = qn.shape
    b, seq, d_kv = kl.shape
    n_ctx = kc.shape[1]
    qt = seq // tq
    return pl.pallas_call(
        _attn_kernel,
        grid=(b, qt),
        in_specs=[pl.BlockSpec((tq, d_att), lambda bi, i: (bi * qt + i, 0)),
                  pl.BlockSpec((None, seq, d_kv), lambda bi, i: (bi, 0, 0)),
                  pl.BlockSpec((None, seq, d_kv), lambda bi, i: (bi, 0, 0)),
                  pl.BlockSpec((None, n_ctx, d_kv), lambda bi, i: (bi, 0, 0)),
                  pl.BlockSpec((None, n_ctx, d_kv), lambda bi, i: (bi, 0, 0))],
        out_specs=pl.BlockSpec((tq, d_att), lambda bi, i: (bi * qt + i, 0)),
        out_shape=jax.ShapeDtypeStruct((n_lat, d_att), F32),
        compiler_params=_params("parallel", "parallel"),
        name="attention",
    )(qn, kl, vl, kc, vc)


CONV_PAD_ROWS = 8
CONV_PIECE = 32
LRU_SEGS = SUBLANES
GATE_ROWS = 256
SCAN_STEPS = 4
LOG2_E = 1.4426950408889634


def _softplus(z):
    return jnp.maximum(z, 0.0) + jnp.log1p(jnp.exp(-jnp.abs(z)))


def _segment_len(total):
    seg = -(-total // LRU_SEGS)
    return seg + (SUBLANES // 2 - seg) % SUBLANES


def _scan8(a, bv, reverse):
    row = lax.broadcasted_iota(jnp.int32, a.shape, 0)
    for s in (1, 2, 4):
        if reverse:
            keep = row < SUBLANES - s
            shift = SUBLANES - s
        else:
            keep = row >= s
            shift = s
        a_prev = jnp.where(keep, pltpu.roll(a, shift, 0), 1.0)
        b_prev = jnp.where(keep, pltpu.roll(bv, shift, 0), 0.0)
        bv = bv + a * b_prev
        a = a * a_prev
    return a, bv


def _lru_kernel(xl_ref, xc_ref, cw_ref, cb_ref, w_ref, bias_ref, lam_ref, o_ref,
                xp_ref, u_ref, a_ref, b_ref, hs_ref, *, seq, n_ctx, seg_len):
    cg = o_ref.shape[1]
    n_blk = cg // LRU_BLOCK_DIM
    total = n_ctx + seq
    padded = LRU_SEGS * seg_len
    for n in range(n_blk):
        a_ref[n, total:padded, :] = jnp.ones((padded - total, LRU_BLOCK_DIM), F32)
        b_ref[n, total:padded, :] = jnp.zeros((padded - total, LRU_BLOCK_DIM), F32)
    zeros_pad = jnp.zeros((CONV_PAD_ROWS, cg), F32)
    halo = CONV_PIECE + 2 * CONV_PAD_ROWS
    half_w = [0.5 * cw_ref[j:j + 1, :] for j in range(CONV_WIDTH)]
    half_b = 0.5 * cb_ref[...]

    def conv_into(src_ref, n_rows, u_rows):
        xp_ref[0:CONV_PAD_ROWS, :] = zeros_pad
        xp_ref[CONV_PAD_ROWS:CONV_PAD_ROWS + n_rows, :] = src_ref[...]
        xp_ref[CONV_PAD_ROWS + n_rows:2 * CONV_PAD_ROWS + n_rows, :] = zeros_pad
        for p in range(n_rows // CONV_PIECE):
            window = xp_ref[p * CONV_PIECE:p * CONV_PIECE + halo, :]
            acc = half_b
            for j in range(CONV_WIDTH):
                off = j - CONV_LEFT
                shifted = window if off == 0 else pltpu.roll(window, (-off) % halo, 0)
                acc = acc + half_w[j] * shifted[CONV_PAD_ROWS:CONV_PAD_ROWS + CONV_PIECE, :]
            for u0 in u_rows:
                u_ref[u0 + p * CONV_PIECE:u0 + (p + 1) * CONV_PIECE, :] = acc

    conv_into(xc_ref, n_ctx, (0, n_ctx + seq))
    conv_into(xl_ref, seq, (n_ctx,))

    for d in range(2):
        reverse = d == 1
        seq_row0 = n_ctx if reverse else 0
        lat_row0 = 0 if reverse else n_ctx
        half_decay = [0.5 * LRU_C * _softplus(-lam_ref[d:d + 1, n * LRU_BLOCK_DIM:(n + 1) * LRU_BLOCK_DIM])
                      for n in range(n_blk)]
        exp2_scale = [-LOG2_E * hd for hd in half_decay]

        def gate_body(c, _, d=d, seq_row0=seq_row0, half_decay=half_decay, exp2_scale=exp2_scale):
            s0 = pl.multiple_of(c * GATE_ROWS, GATE_ROWS)
            hu = u_ref[pl.ds(pl.multiple_of(seq_row0 + s0, SUBLANES), GATE_ROWS), :]
            hub = hu.astype(BF16)
            for n in range(n_blk):
                sl = slice(n * LRU_BLOCK_DIM, (n + 1) * LRU_BLOCK_DIM)
                pre = jnp.dot(hub[:, sl], w_ref[d, n], preferred_element_type=F32) + bias_ref[d, n]
                t_r = jnp.tanh(pre[:, :LRU_BLOCK_DIM])
                t_i = jnp.tanh(pre[:, LRU_BLOCK_DIM:])
                neg_log_a = t_r * half_decay[n] + half_decay[n]
                a = jnp.exp2(t_r * exp2_scale[n] + exp2_scale[n])
                a_ref[n, pl.ds(s0, GATE_ROWS), :] = a
                w = jnp.tanh(neg_log_a) * (1.0 + a * a)
                mult = jnp.where(w > 0.0, w * lax.rsqrt(w), 0.0)
                b_ref[n, pl.ds(s0, GATE_ROWS), :] = mult * ((t_i + 1.0) * hu[:, sl])
            return 0

        lax.fori_loop(0, total // GATE_ROWS, gate_body, 0)

        def load_steps(tb, reverse=reverse):
            rows = []
            base = (seg_len // SCAN_STEPS - 1 - tb) * SCAN_STEPS if reverse else tb * SCAN_STEPS
            for k in range(SCAN_STEPS):
                i = base + (SCAN_STEPS - 1 - k if reverse else k)
                idx = pl.ds(i, LRU_SEGS, stride=seg_len)
                rows.append((idx, [a_ref[n, idx, :] for n in range(n_blk)],
                             [b_ref[n, idx, :] for n in range(n_blk)]))
            return rows

        def local_body(tb, carry):
            hs, decs = list(carry[0]), list(carry[1])
            for _, a, bv in load_steps(tb):
                for n in range(n_blk):
                    hs[n] = a[n] * hs[n] + bv[n]
                    decs[n] = decs[n] * a[n]
            return tuple(hs), tuple(decs)

        zero = tuple(jnp.zeros((LRU_SEGS, LRU_BLOCK_DIM), F32) for _ in range(n_blk))
        one = tuple(jnp.ones((LRU_SEGS, LRU_BLOCK_DIM), F32) for _ in range(n_blk))
        h_end, dec_end = lax.fori_loop(0, seg_len // SCAN_STEPS, local_body, (zero, one))

        starts = []
        for n in range(n_blk):
            _, state = _scan8(dec_end[n], h_end[n], reverse)
            row = lax.broadcasted_iota(jnp.int32, state.shape, 0)
            if reverse:
                starts.append(jnp.where(row < LRU_SEGS - 1, pltpu.roll(state, LRU_SEGS - 1, 0), 0.0))
            else:
                starts.append(jnp.where(row >= 1, pltpu.roll(state, 1, 0), 0.0))

        def final_body(tb, hs):
            hs = list(hs)
            for idx, a, bv in load_steps(tb):
                for n in range(n_blk):
                    hs[n] = a[n] * hs[n] + bv[n]
                    hs_ref[n, idx, :] = hs[n]
            return tuple(hs)

        lax.fori_loop(0, seg_len // SCAN_STEPS, final_body, tuple(starts))

        for n in range(n_blk):
            sl = slice(n * LRU_BLOCK_DIM, (n + 1) * LRU_BLOCK_DIM)
            for r0 in range(0, seq, GATE_ROWS):
                piece = hs_ref[n, lat_row0 + r0:lat_row0 + r0 + GATE_ROWS, :]
                if reverse:
                    o_ref[r0:r0 + GATE_ROWS, sl] += piece
                else:
                    o_ref[r0:r0 + GATE_ROWS, sl] = piece


def _lru(proj, conv_w, conv_b, w_cat, bias_cat, lam, b, seq, n_ctx, xl_col0, d_lru, cg=512):
    n_lat = b * seq
    col_blk0 = xl_col0 // cg
    n_blk = cg // LRU_BLOCK_DIM
    seg_len = _segment_len(n_ctx + seq)
    assert n_ctx % GATE_ROWS == 0 and seq % GATE_ROWS == 0 and GATE_ROWS % CONV_PIECE == 0
    assert seg_len % SCAN_STEPS == 0
    kern = functools.partial(_lru_kernel, seq=seq, n_ctx=n_ctx, seg_len=seg_len)
    ctx_blk0 = n_lat // n_ctx
    return pl.pallas_call(
        kern,
        grid=(b, d_lru // cg),
        in_specs=[pl.BlockSpec((seq, cg), lambda bi, g: (bi, col_blk0 + g)),
                  pl.BlockSpec((n_ctx, cg), lambda bi, g: (ctx_blk0 + bi, col_blk0 + g)),
                  pl.BlockSpec((CONV_WIDTH, cg), lambda bi, g: (0, g)),
                  pl.BlockSpec((1, cg), lambda bi, g: (0, g)),
                  pl.BlockSpec((2, n_blk, LRU_BLOCK_DIM, 2 * LRU_BLOCK_DIM), lambda bi, g: (0, g, 0, 0)),
                  pl.BlockSpec((2, n_blk, 1, 2 * LRU_BLOCK_DIM), lambda bi, g: (0, g, 0, 0)),
                  pl.BlockSpec((2, cg), lambda bi, g: (0, g))],
        out_specs=pl.BlockSpec((seq, cg), lambda bi, g: (bi, g)),
        out_shape=jax.ShapeDtypeStruct((n_lat, d_lru), F32),
        scratch_shapes=[pltpu.VMEM((seq + 2 * CONV_PAD_ROWS, cg), F32),
                        pltpu.VMEM((2 * n_ctx + seq, cg), F32),
                        pltpu.VMEM((n_blk, LRU_SEGS * seg_len, LRU_BLOCK_DIM), F32),
                        pltpu.VMEM((n_blk, LRU_SEGS * seg_len, LRU_BLOCK_DIM), F32),
                        pltpu.VMEM((n_blk, LRU_SEGS * seg_len, LRU_BLOCK_DIM), F32)],
        compiler_params=_params("parallel", "parallel"),
        name="rglru",
    )(proj, proj, conv_w, conv_b, w_cat, bias_cat, lam)


def _merge_kernel(att_ref, lru_ref, ga0_ref, ga1_ref, gl0_ref, gl1_ref, wa_ref, wl_ref, o_ref):
    d_att = att_ref.shape[1]
    half = ga0_ref.shape[1]
    ya = _rms(att_ref[...], wa_ref[...])
    yl = _rms(lru_ref[...], wl_ref[...])
    o_ref[:, 0:half] = (ya[:, 0:half] * _silu(ga0_ref[...])).astype(BF16)
    o_ref[:, half:d_att] = (ya[:, half:] * _silu(ga1_ref[...])).astype(BF16)
    o_ref[:, d_att:d_att + half] = (yl[:, 0:half] * _silu(gl0_ref[...])).astype(BF16)
    o_ref[:, d_att + half:] = (yl[:, half:] * _silu(gl1_ref[...])).astype(BF16)


def _merge(att, lru, proj, wa, wl, ga_col0, gl_col0, tr=256):
    n_lat, d_att = att.shape
    d_lru = lru.shape[1]
    half = d_att // 2
    ga_b, gl_b = ga_col0 // half, gl_col0 // half
    return pl.pallas_call(
        _merge_kernel,
        grid=(n_lat // tr,),
        in_specs=[pl.BlockSpec((tr, d_att), lambda t: (t, 0)),
                  pl.BlockSpec((tr, d_lru), lambda t: (t, 0)),
                  pl.BlockSpec((tr, half), lambda t: (t, ga_b)),
                  pl.BlockSpec((tr, half), lambda t: (t, ga_b + 1)),
                  pl.BlockSpec((tr, half), lambda t: (t, gl_b)),
                  pl.BlockSpec((tr, half), lambda t: (t, gl_b + 1)),
                  pl.BlockSpec((1, d_att), lambda t: (0, 0)),
                  pl.BlockSpec((1, d_lru), lambda t: (0, 0))],
        out_specs=pl.BlockSpec((tr, d_att + d_lru), lambda t: (t, 0)),
        out_shape=jax.ShapeDtypeStruct((n_lat, d_att + d_lru), BF16),
        compiler_params=_params("parallel"),
        name="merge",
    )(att, lru, proj, proj, proj, proj, wa, wl)


def _out_kernel(mix_ref, w_ref, x_ref, gate_ref, o_ref, *, tiles_per_batch):
    bi = pl.program_id(1) // tiles_per_batch
    gate = gate_ref[pl.ds(bi, 1), :]
    o_ref[...] = x_ref[...] + gate * jnp.dot(mix_ref[...], w_ref[...], preferred_element_type=F32)


def _out_proj(mix, w, x2, mod, seq, gate_col0, tm=512, tn=1024):
    m, k = mix.shape
    n = w.shape[1]
    gb = gate_col0 // tn
    kern = functools.partial(_out_kernel, tiles_per_batch=seq // tm)
    return pl.pallas_call(
        kern,
        grid=(n // tn, m // tm),
        in_specs=[pl.BlockSpec((tm, k), lambda j, i: (i, 0)),
                  pl.BlockSpec((k, tn), lambda j, i: (0, j)),
                  pl.BlockSpec((tm, tn), lambda j, i: (i, j)),
                  pl.BlockSpec((MOD_ROWS, tn), lambda j, i: (0, gb + j))],
        out_specs=pl.BlockSpec((tm, tn), lambda j, i: (i, j)),
        out_shape=jax.ShapeDtypeStruct((m, n), F32),
        compiler_params=_params("parallel", "parallel"),
        name="out_proj",
    )(mix, w, x2, mod)


def _rope_tables(seq):
    pos = jnp.arange(seq)
    row = (pos // GRID_W).astype(F32)
    col = (pos % GRID_W).astype(F32)
    n_freq = HEAD_DIM // 4
    freqs = ROPE_THETA ** (-jnp.arange(n_freq, dtype=F32) / n_freq)
    ang_r = row[:, None] * freqs
    ang_c = col[:, None] * freqs
    cos = jnp.concatenate([jnp.cos(ang_r), jnp.cos(ang_r), jnp.cos(ang_c), jnp.cos(ang_c)], axis=1)
    sin = jnp.concatenate([-jnp.sin(ang_r), jnp.sin(ang_r), -jnp.sin(ang_c), jnp.sin(ang_c)], axis=1)
    return cos, sin


def kernel(x, c, ctx, c_ctx, w_ada, b_ada, norm_w, w_in, q_norm_w, k_norm_w, conv_w, conv_b,
           lru_wa, lru_ba, lru_wx, lru_bx, lru_lambda, out_norm_att, out_norm_lru, w_out):
    assert w_ada.shape[0] == 1, "single-layer kernel: only the latent stream is produced"
    b, seq, d = x.shape
    n_ctx = ctx.shape[1]
    d_att = out_norm_att.shape[1]
    d_lru = out_norm_lru.shape[1]
    d_kv = (w_in.shape[2] - 2 * d_att - 2 * d_lru) // 2
    n_lat = b * seq
    assert b + 1 <= MOD_ROWS
    ga_col0 = d_att + 2 * d_kv
    xl_col0 = ga_col0 + d_att
    gl_col0 = xl_col0 + d_lru

    x2 = x.reshape(n_lat, d)
    c2 = ctx.reshape(b * n_ctx, d)

    cvec = jnp.concatenate([c, c_ctx[None, :], jnp.zeros((MOD_ROWS - b - 1, d), F32)], axis=0)
    mod = _modulation(cvec, w_ada[0], b_ada)

    h = _prenorm(x2, c2, norm_w, mod, seq)
    proj = _in_proj(h, w_in[0].astype(BF16))

    cos, sin = _rope_tables(seq)
    qn, kl, vl = _qkv_latent(proj, cos, sin, q_norm_w, k_norm_w, n_lat, seq, d_att, d_kv)
    kc, vc = _kv_ctx(proj, k_norm_w, n_lat, b * n_ctx, d_att, d_kv)
    att = _attention(qn, kl.reshape(b, seq, d_kv), vl.reshape(b, seq, d_kv),
                     kc.reshape(b, n_ctx, d_kv), vc.reshape(b, n_ctx, d_kv))

    n_blocks = d_lru // LRU_BLOCK_DIM
    w_cat = jnp.concatenate([lru_wa[0], lru_wx[0]], axis=-1).astype(BF16)
    bias_cat = 0.5 * jnp.concatenate([lru_ba[0].reshape(2, n_blocks, 1, LRU_BLOCK_DIM),
                                      lru_bx[0].reshape(2, n_blocks, 1, LRU_BLOCK_DIM)], axis=-1)
    lru = _lru(proj, conv_w[0], conv_b, w_cat, bias_cat, lru_lambda[0], b, seq, n_ctx, xl_col0, d_lru)

    mix = _merge(att, lru, proj, out_norm_att, out_norm_lru, ga_col0, gl_col0)
    out = _out_proj(mix, w_out[0].astype(BF16), x2, mod, seq, 2 * d)
    return out.reshape(b, seq, d)
```

```python
import functools
import math

import jax
import jax.numpy as jnp
from jax import lax
from jax.experimental import pallas as pl
from jax.experimental.pallas import tpu as pltpu

F32 = jnp.float32
BF16 = jnp.bfloat16

HEAD_DIM = 128
GRID_W = 64
Q_PER_KV = 4
LRU_BLOCK_DIM = 128
CONV_WIDTH = 4
CONV_LEFT = 2
LRU_C = 8.0
ROPE_THETA = 10000.0
EPS = 1e-6
LOG2_E = 1.4426950408889634

LANES = 128
SUBLANES = 8
MXU_WIDTH = 256
MOD_ROWS = 8
VMEM_LIMIT = 56 * 1024 * 1024


def _params(*sem):
    return pltpu.CompilerParams(dimension_semantics=sem, vmem_limit_bytes=VMEM_LIMIT)


def _silu(x):
    return x * jax.nn.sigmoid(x)


def _rms(x, w):
    ms = jnp.mean(x * x, axis=-1, keepdims=True)
    return x * lax.rsqrt(ms + EPS) * w


def _mod_kernel(c_ref, w_ref, b_ref, o_ref):
    s = _silu(c_ref[...]).astype(BF16)
    o_ref[...] = jnp.dot(s, w_ref[...].astype(BF16), preferred_element_type=F32) + b_ref[...]


def _modulation(cvec, w_ada, b_ada, tn=512):
    d, n = w_ada.shape
    return pl.pallas_call(
        _mod_kernel,
        grid=(n // tn,),
        in_specs=[pl.BlockSpec((MOD_ROWS, d), lambda j: (0, 0)),
                  pl.BlockSpec((d, tn), lambda j: (0, j)),
                  pl.BlockSpec((1, tn), lambda j: (0, j))],
        out_specs=pl.BlockSpec((MOD_ROWS, tn), lambda j: (0, j)),
        out_shape=jax.ShapeDtypeStruct((MOD_ROWS, n), F32),
        compiler_params=_params("parallel"),
        name="modulation",
    )(cvec, w_ada, b_ada)


def _prenorm_kernel(x_ref, c_ref, nw_ref, mod_ref, o_ref, *, d, n_x_tiles, tiles_per_batch, ctx_row):
    t = pl.program_id(0)

    def emit(src_ref, row):
        y = _rms(src_ref[...], nw_ref[...])
        shift = mod_ref[pl.ds(row, 1), 0:d]
        scale = mod_ref[pl.ds(row, 1), d:2 * d]
        o_ref[...] = (y * (1.0 + scale) + shift).astype(BF16)

    @pl.when(t < n_x_tiles)
    def _():
        emit(x_ref, t // tiles_per_batch)

    @pl.when(t >= n_x_tiles)
    def _():
        emit(c_ref, ctx_row)


def _prenorm(x2, c2, norm_w, mod, seq, tr=256):
    mx, d = x2.shape
    mc = c2.shape[0]
    nx, nc = mx // tr, mc // tr
    kern = functools.partial(_prenorm_kernel, d=d, n_x_tiles=nx, tiles_per_batch=seq // tr,
                             ctx_row=mx // seq)
    return pl.pallas_call(
        kern,
        grid=(nx + nc,),
        in_specs=[pl.BlockSpec((tr, d), lambda t: (jnp.minimum(t, nx - 1), 0)),
                  pl.BlockSpec((tr, d), lambda t: (jnp.maximum(t - nx, 0), 0)),
                  pl.BlockSpec((1, d), lambda t: (0, 0)),
                  pl.BlockSpec(mod.shape, lambda t: (0, 0))],
        out_specs=pl.BlockSpec((tr, d), lambda t: (t, 0)),
        out_shape=jax.ShapeDtypeStruct((mx + mc, d), BF16),
        compiler_params=_params("parallel"),
        name="prenorm",
    )(x2, c2, norm_w, mod)


PROJ_TM = 1024
PROJ_TN = 512


def _cast_weights_once(w_ref, wb_ref):
    @pl.when(pl.program_id(1) == 0)
    def _():
        wb_ref[...] = w_ref[...].astype(BF16)


def _rope(y, cos, sin_signed):
    lane = lax.broadcasted_iota(jnp.int32, y.shape, 1)
    first_half = (lane & (HEAD_DIM // 4)) == 0
    partner = jnp.where(first_half, pltpu.roll(y, HEAD_DIM - HEAD_DIM // 4, 1),
                        pltpu.roll(y, HEAD_DIM // 4, 1))
    return y * cos + partner * sin_signed


def _proj_heads_kernel(h_ref, w_ref, cos_ref, sin_ref, nw_ref, o_ref, wb_ref, acc0_ref, acc1_ref,
                       *, n_row_tiles, rope_tiles, out_scale):
    i = pl.program_id(1)
    _cast_weights_once(w_ref, wb_ref)
    accs = (acc0_ref, acc1_ref)

    def matmul_into(acc_ref):
        acc_ref[...] = jnp.dot(h_ref[...], wb_ref[...], preferred_element_type=F32)

    def finish_from(acc_ref, rope):
        norm_w = nw_ref[...] * out_scale
        for hh in range(o_ref.shape[1] // HEAD_DIM):
            sl = slice(hh * HEAD_DIM, (hh + 1) * HEAD_DIM)
            y = _rms(acc_ref[:, sl], norm_w)
            if rope:
                y = _rope(y, cos_ref[...], sin_ref[...])
            o_ref[:, sl] = y.astype(BF16)

    @pl.when(i == 0)
    def _():
        matmul_into(accs[0])

    for parity in range(2):
        @pl.when((i >= 1) & (i < n_row_tiles) & (i % 2 == parity))
        def _(parity=parity):
            matmul_into(accs[parity])
            finish_from(accs[1 - parity], True)

    @pl.when(i == n_row_tiles)
    def _():
        finish_from(accs[(n_row_tiles - 1) % 2], n_row_tiles - 1 < rope_tiles)


def _proj_heads(name, h, w_in, n_rows, col0, n_cols, cos, sin, norm_w, seq, rope_tiles, out_scale):
    k = h.shape[1]
    n_row_tiles = n_rows // PROJ_TM
    assert n_row_tiles - 2 < rope_tiles
    spt = seq // PROJ_TM
    kern = functools.partial(_proj_heads_kernel, n_row_tiles=n_row_tiles, rope_tiles=rope_tiles,
                             out_scale=out_scale)
    done = lambda i: jnp.maximum(i - 1, 0)
    return pl.pallas_call(
        kern,
        grid=(n_cols // PROJ_TN, n_row_tiles + 1),
        in_specs=[pl.BlockSpec((PROJ_TM, k), lambda j, i: (jnp.minimum(i, n_row_tiles - 1), 0)),
                  pl.BlockSpec((k, PROJ_TN), lambda j, i: (0, col0 // PROJ_TN + j)),
                  pl.BlockSpec((PROJ_TM, HEAD_DIM), lambda j, i: (done(i) % spt, 0)),
                  pl.BlockSpec((PROJ_TM, HEAD_DIM), lambda j, i: (done(i) % spt, 0)),
                  pl.BlockSpec((1, HEAD_DIM), lambda j, i: (0, 0))],
        out_specs=pl.BlockSpec((PROJ_TM, PROJ_TN), lambda j, i: (done(i), j)),
        out_shape=jax.ShapeDtypeStruct((n_rows, n_cols), BF16),
        scratch_shapes=[pltpu.VMEM((k, PROJ_TN), BF16),
                        pltpu.VMEM((PROJ_TM, PROJ_TN), F32),
                        pltpu.VMEM((PROJ_TM, PROJ_TN), F32)],
        compiler_params=_params("parallel", "arbitrary"),
        name=name,
    )(h, w_in, cos, sin, norm_w)


def _proj_plain_kernel(h_ref, w_ref, o_ref, wb_ref):
    _cast_weights_once(w_ref, wb_ref)
    o_ref[...] = jnp.dot(h_ref[...], wb_ref[...], preferred_element_type=F32).astype(o_ref.dtype)


def _proj_plain(name, h, w_in, n_rows, col_map, n_col_tiles, out_dtype):
    k = h.shape[1]
    return pl.pallas_call(
        _proj_plain_kernel,
        grid=(n_col_tiles, n_rows // PROJ_TM),
        in_specs=[pl.BlockSpec((PROJ_TM, k), lambda j, i: (i, 0)),
                  pl.BlockSpec((k, PROJ_TN), lambda j, i: (0, col_map(j)))],
        out_specs=pl.BlockSpec((PROJ_TM, PROJ_TN), lambda j, i: (i, j)),
        out_shape=jax.ShapeDtypeStruct((n_rows, n_col_tiles * PROJ_TN), out_dtype),
        scratch_shapes=[pltpu.VMEM((k, PROJ_TN), BF16)],
        compiler_params=_params("parallel", "arbitrary"),
        name=name,
    )(h, w_in)


V_EXTRA_ROWS = 16


def _attn_kernel(q_ref, kl_ref, vl_ref, kc_ref, vc_ref, ga_ref, nw_ref, o_ref, k_ref, vt_ref, st_ref, att_ref):
    tq = q_ref.shape[0]
    seq, n_ctx = kl_ref.shape[0], kc_ref.shape[0]
    n_kv = kl_ref.shape[1] // HEAD_DIM
    nt = (((1,), (1,)), ((), ()))

    n_keys = seq + n_ctx

    @pl.when(pl.program_id(1) == 0)
    def _():
        k_ref[0:seq, :] = kl_ref[...]
        k_ref[seq:n_keys, :] = kc_ref[...]
        row = lax.broadcasted_iota(jnp.int32, (V_EXTRA_ROWS, n_keys), 0)
        tail = jnp.where(row == 0, 1.0, 0.0).astype(BF16)
        for kv in range(n_kv):
            ks = slice(kv * HEAD_DIM, (kv + 1) * HEAD_DIM)
            vt_ref[kv, 0:HEAD_DIM, 0:seq] = vl_ref[:, ks].T
            vt_ref[kv, 0:HEAD_DIM, seq:n_keys] = vc_ref[:, ks].T
            vt_ref[kv, HEAD_DIM:HEAD_DIM + V_EXTRA_ROWS, :] = tail

    def head_slices(kv):
        return [slice((kv * Q_PER_KV + g) * HEAD_DIM, (kv * Q_PER_KV + g + 1) * HEAD_DIM)
                for g in range(Q_PER_KV)]

    def scores(kv, slot):
        q4 = jnp.concatenate([q_ref[:, hs] for hs in head_slices(kv)], axis=0)
        st_ref[slot] = lax.dot_general(k_ref[:, kv * HEAD_DIM:(kv + 1) * HEAD_DIM], q4, nt,
                                       preferred_element_type=F32)

    def softmax_values(kv, slot):
        m = jnp.max(st_ref[slot], axis=0, keepdims=True)
        pt = jnp.exp2(st_ref[slot] - m).astype(BF16)
        res = jnp.dot(vt_ref[kv], pt, preferred_element_type=F32)
        ot = res[:HEAD_DIM, :] / res[HEAD_DIM:HEAD_DIM + 1, :]
        for g, hs in enumerate(head_slices(kv)):
            att_ref[:, hs] = ot[:, g * tq:(g + 1) * tq].T

    scores(0, 0)
    for kv in range(n_kv):
        if kv + 1 < n_kv:
            scores(kv + 1, (kv + 1) % 2)
        softmax_values(kv, kv % 2)
    o_ref[...] = (_rms(att_ref[...], nw_ref[...]) * _silu(ga_ref[...])).astype(BF16)


def _attention(qn, kn, vb, gates, norm_w, b, seq, n_ctx, tq=128):
    n_lat, d_att = qn.shape
    d_kv = kn.shape[1]
    qt = seq // tq
    ctx_blk0 = n_lat // n_ctx
    return pl.pallas_call(
        _attn_kernel,
        grid=(b, qt),
        in_specs=[pl.BlockSpec((tq, d_att), lambda bi, i: (bi * qt + i, 0)),
                  pl.BlockSpec((seq, d_kv), lambda bi, i: (bi, 0)),
                  pl.BlockSpec((seq, d_kv), lambda bi, i: (bi, 0)),
                  pl.BlockSpec((n_ctx, d_kv), lambda bi, i: (ctx_blk0 + bi, 0)),
                  pl.BlockSpec((n_ctx, d_kv), lambda bi, i: (ctx_blk0 + bi, 0)),
                  pl.BlockSpec((tq, d_att), lambda bi, i: (bi * qt + i, 0)),
                  pl.BlockSpec((1, d_att), lambda bi, i: (0, 0))],
        out_specs=pl.BlockSpec((tq, d_att), lambda bi, i: (bi * qt + i, 0)),
        out_shape=jax.ShapeDtypeStruct((n_lat, d_att), BF16),
        scratch_shapes=[pltpu.VMEM((seq + n_ctx, d_kv), BF16),
                        pltpu.VMEM((d_kv // HEAD_DIM, HEAD_DIM + V_EXTRA_ROWS, seq + n_ctx), BF16),
                        pltpu.VMEM((2, seq + n_ctx, Q_PER_KV * tq), F32),
                        pltpu.VMEM((tq, d_att), F32)],
        compiler_params=_params("parallel", "arbitrary"),
        name="attention",
    )(qn, kn, vb, kn, vb, gates, norm_w)


CONV_PAD_ROWS = 8
CONV_PIECE = 32
LRU_SEGS = SUBLANES
GATE_ROWS = 256
SCAN_STEPS = 4


def _softplus(z):
    return jnp.maximum(z, 0.0) + jnp.log1p(jnp.exp(-jnp.abs(z)))


def _segment_len(total):
    seg = -(-total // LRU_SEGS)
    return seg + (SUBLANES // 2 - seg) % SUBLANES


def _scan8(a, bv, reverse):
    row = lax.broadcasted_iota(jnp.int32, a.shape, 0)
    for s in (1, 2, 4):
        if reverse:
            keep = row < SUBLANES - s
            shift = SUBLANES - s
        else:
            keep = row >= s
            shift = s
        a_prev = jnp.where(keep, pltpu.roll(a, shift, 0), 1.0)
        b_prev = jnp.where(keep, pltpu.roll(bv, shift, 0), 0.0)
        bv = bv + a * b_prev
        a = a * a_prev
    return a, bv


def _lru_kernel(xl_ref, xc_ref, cw_ref, cb_ref, w_ref, bias_ref, lam_ref, gl_ref, nw_ref, y_ref, ss_ref,
                xp_ref, u_ref, a_ref, b_ref, hs_ref, *, seq, n_ctx, seg_len):
    cg = y_ref.shape[1]
    n_blk = cg // LRU_BLOCK_DIM
    total = n_ctx + seq
    padded = LRU_SEGS * seg_len
    for n in range(n_blk):
        a_ref[n, total:padded, :] = jnp.ones((padded - total, LRU_BLOCK_DIM), F32)
        b_ref[n, total:padded, :] = jnp.zeros((padded - total, LRU_BLOCK_DIM), F32)
    zeros_pad = jnp.zeros((CONV_PAD_ROWS, cg), F32)
    halo = CONV_PIECE + 2 * CONV_PAD_ROWS
    half_w = [0.5 * cw_ref[j:j + 1, :] for j in range(CONV_WIDTH)]
    half_b = 0.5 * cb_ref[...]

    def conv_into(src_ref, n_rows, u_rows):
        xp_ref[0:CONV_PAD_ROWS, :] = zeros_pad
        xp_ref[CONV_PAD_ROWS:CONV_PAD_ROWS + n_rows, :] = src_ref[...]
        xp_ref[CONV_PAD_ROWS + n_rows:2 * CONV_PAD_ROWS + n_rows, :] = zeros_pad
        for p in range(n_rows // CONV_PIECE):
            window = xp_ref[p * CONV_PIECE:p * CONV_PIECE + halo, :]
            acc = half_b
            for j in range(CONV_WIDTH):
                off = j - CONV_LEFT
                shifted = window if off == 0 else pltpu.roll(window, (-off) % halo, 0)
                acc = acc + half_w[j] * shifted[CONV_PAD_ROWS:CONV_PAD_ROWS + CONV_PIECE, :]
            for u0 in u_rows:
                u_ref[u0 + p * CONV_PIECE:u0 + (p + 1) * CONV_PIECE, :] = acc

    conv_into(xc_ref, n_ctx, (0, n_ctx + seq))
    conv_into(xl_ref, seq, (n_ctx,))

    for d in range(2):
        reverse = d == 1
        seq_row0 = n_ctx if reverse else 0
        lat_row0 = 0 if reverse else n_ctx
        half_decay = [0.5 * LRU_C * _softplus(-lam_ref[d:d + 1, n * LRU_BLOCK_DIM:(n + 1) * LRU_BLOCK_DIM])
                      for n in range(n_blk)]
        exp2_scale = [-LOG2_E * hd for hd in half_decay]

        def gate_body(c, _, d=d, seq_row0=seq_row0, half_decay=half_decay, exp2_scale=exp2_scale):
            s0 = pl.multiple_of(c * GATE_ROWS, GATE_ROWS)
            hu = u_ref[pl.ds(pl.multiple_of(seq_row0 + s0, SUBLANES), GATE_ROWS), :]
            hub = hu.astype(BF16)
            for n in range(n_blk):
                sl = slice(n * LRU_BLOCK_DIM, (n + 1) * LRU_BLOCK_DIM)
                pre = jnp.dot(hub[:, sl], w_ref[d, n], preferred_element_type=F32) + bias_ref[d, n]
                t_r = jnp.tanh(pre[:, :LRU_BLOCK_DIM])
                t_i = jnp.tanh(pre[:, LRU_BLOCK_DIM:])
                neg_log_a = t_r * half_decay[n] + half_decay[n]
                a = jnp.exp2(t_r * exp2_scale[n] + exp2_scale[n])
                a_ref[n, pl.ds(s0, GATE_ROWS), :] = a
                w = jnp.tanh(neg_log_a) * (1.0 + a * a)
                mult = jnp.where(w > 0.0, w * lax.rsqrt(w), 0.0)
                b_ref[n, pl.ds(s0, GATE_ROWS), :] = mult * ((t_i + 1.0) * hu[:, sl])
            return 0

        lax.fori_loop(0, total // GATE_ROWS, gate_body, 0)

        def load_steps(tb, reverse=reverse):
            rows = []
            base = (seg_len // SCAN_STEPS - 1 - tb) * SCAN_STEPS if reverse else tb * SCAN_STEPS
            for k in range(SCAN_STEPS):
                i = base + (SCAN_STEPS - 1 - k if reverse else k)
                idx = pl.ds(i, LRU_SEGS, stride=seg_len)
                rows.append((idx, [a_ref[n, idx, :] for n in range(n_blk)],
                             [b_ref[n, idx, :] for n in range(n_blk)]))
            return rows

        def local_body(tb, carry):
            hs, decs = list(carry[0]), list(carry[1])
            for _, a, bv in load_steps(tb):
                for n in range(n_blk):
                    hs[n] = a[n] * hs[n] + bv[n]
                    decs[n] = decs[n] * a[n]
            return tuple(hs), tuple(decs)

        zero = tuple(jnp.zeros((LRU_SEGS, LRU_BLOCK_DIM), F32) for _ in range(n_blk))
        one = tuple(jnp.ones((LRU_SEGS, LRU_BLOCK_DIM), F32) for _ in range(n_blk))
        h_end, dec_end = lax.fori_loop(0, seg_len // SCAN_STEPS, local_body, (zero, one))

        starts = []
        for n in range(n_blk):
            _, state = _scan8(dec_end[n], h_end[n], reverse)
            row = lax.broadcasted_iota(jnp.int32, state.shape, 0)
            if reverse:
                starts.append(jnp.where(row < LRU_SEGS - 1, pltpu.roll(state, LRU_SEGS - 1, 0), 0.0))
            else:
                starts.append(jnp.where(row >= 1, pltpu.roll(state, 1, 0), 0.0))

        def final_body(tb, hs):
            hs = list(hs)
            for idx, a, bv in load_steps(tb):
                for n in range(n_blk):
                    hs[n] = a[n] * hs[n] + bv[n]
                    hs_ref[n, idx, :] = hs[n]
            return tuple(hs)

        lax.fori_loop(0, seg_len // SCAN_STEPS, final_body, tuple(starts))

        if not reverse:
            for n in range(n_blk):
                sl = slice(n * LRU_BLOCK_DIM, (n + 1) * LRU_BLOCK_DIM)
                for r0 in range(0, seq, GATE_ROWS):
                    xp_ref[r0:r0 + GATE_ROWS, sl] = hs_ref[n, lat_row0 + r0:lat_row0 + r0 + GATE_ROWS, :]
        else:
            for r0 in range(0, seq, GATE_ROWS):
                rows = slice(r0, r0 + GATE_ROWS)
                lru = xp_ref[rows, :] + jnp.concatenate(
                    [hs_ref[n, lat_row0 + r0:lat_row0 + r0 + GATE_ROWS, :] for n in range(n_blk)], axis=1)
                y_ref[rows, :] = (lru * nw_ref[...] * _silu(gl_ref[rows, :])).astype(BF16)
                ss_ref[rows, :] = jnp.broadcast_to(jnp.sum(lru * lru, axis=-1, keepdims=True),
                                                   (GATE_ROWS, LANES))


def _lru(xl, gates, conv_w, conv_b, w_cat, bias_cat, lam, norm_w, b, seq, n_ctx, d_att, cg=512):
    n_lat = b * seq
    d_lru = xl.shape[1]
    n_blk = cg // LRU_BLOCK_DIM
    n_groups = d_lru // cg
    seg_len = _segment_len(n_ctx + seq)
    assert n_ctx % GATE_ROWS == 0 and seq % GATE_ROWS == 0 and GATE_ROWS % CONV_PIECE == 0
    assert seg_len % SCAN_STEPS == 0
    kern = functools.partial(_lru_kernel, seq=seq, n_ctx=n_ctx, seg_len=seg_len)
    ctx_blk0 = n_lat // n_ctx
    gl_blk0 = d_att // cg
    scan_rows = LRU_SEGS * seg_len
    return pl.pallas_call(
        kern,
        grid=(b, n_groups),
        in_specs=[pl.BlockSpec((seq, cg), lambda bi, g: (bi, g)),
                  pl.BlockSpec((n_ctx, cg), lambda bi, g: (ctx_blk0 + bi, g)),
                  pl.BlockSpec((CONV_WIDTH, cg), lambda bi, g: (0, g)),
                  pl.BlockSpec((1, cg), lambda bi, g: (0, g)),
                  pl.BlockSpec((2, n_blk, LRU_BLOCK_DIM, 2 * LRU_BLOCK_DIM), lambda bi, g: (0, g, 0, 0)),
                  pl.BlockSpec((2, n_blk, 1, 2 * LRU_BLOCK_DIM), lambda bi, g: (0, g, 0, 0)),
                  pl.BlockSpec((2, cg), lambda bi, g: (0, g)),
                  pl.BlockSpec((seq, cg), lambda bi, g: (bi, gl_blk0 + g)),
                  pl.BlockSpec((1, cg), lambda bi, g: (0, g))],
        out_specs=[pl.BlockSpec((seq, cg), lambda bi, g: (bi, g)),
                   pl.BlockSpec((seq, LANES), lambda bi, g: (bi, g))],
        out_shape=[jax.ShapeDtypeStruct((n_lat, d_lru), BF16),
                   jax.ShapeDtypeStruct((n_lat, n_groups * LANES), F32)],
        scratch_shapes=[pltpu.VMEM((seq + 2 * CONV_PAD_ROWS, cg), F32),
                        pltpu.VMEM((2 * n_ctx + seq, cg), F32),
                        pltpu.VMEM((n_blk, scan_rows, LRU_BLOCK_DIM), F32),
                        pltpu.VMEM((n_blk, scan_rows, LRU_BLOCK_DIM), F32),
                        pltpu.VMEM((n_blk, scan_rows, LRU_BLOCK_DIM), F32)],
        compiler_params=_params("parallel", "parallel"),
        name="rglru",
    )(xl, xl, conv_w, conv_b, w_cat, bias_cat, lam, gates, norm_w)


def _out_kernel(ma_ref, yl_ref, ss_ref, wa_ref, wl_ref, x_ref, gate_ref, o_ref, wab_ref, wlb_ref,
                *, tiles_per_batch, d_lru):
    @pl.when(pl.program_id(1) == 0)
    def _():
        wab_ref[...] = wa_ref[...].astype(BF16)
        wlb_ref[...] = wl_ref[...].astype(BF16)

    bi = pl.program_id(1) // tiles_per_batch
    gate = gate_ref[pl.ds(bi, 1), :]
    sumsq = ss_ref[:, 0:LANES]
    for g in range(1, ss_ref.shape[1] // LANES):
        sumsq = sumsq + ss_ref[:, g * LANES:(g + 1) * LANES]
    inv_rms = lax.rsqrt(sumsq * (1.0 / d_lru) + EPS)
    inv_rms = jnp.concatenate([inv_rms] * (o_ref.shape[1] // LANES), axis=1)
    acc = jnp.dot(ma_ref[...], wab_ref[...], preferred_element_type=F32)
    acc = acc + inv_rms * jnp.dot(yl_ref[...], wlb_ref[...], preferred_element_type=F32)
    o_ref[...] = x_ref[...] + gate * acc


def _out_proj(mix_att, y_lru, ss, w_out, x2, mod, seq, gate_col0, tm=1024, tn=512):
    m, d_att = mix_att.shape
    d_lru = y_lru.shape[1]
    assert d_att == d_lru
    n = w_out.shape[1]
    gb = gate_col0 // tn
    kern = functools.partial(_out_kernel, tiles_per_batch=seq // tm, d_lru=d_lru)
    return pl.pallas_call(
        kern,
        grid=(n // tn, m // tm),
        in_specs=[pl.BlockSpec((tm, d_att), lambda j, i: (i, 0)),
                  pl.BlockSpec((tm, d_lru), lambda j, i: (i, 0)),
                  pl.BlockSpec((tm, ss.shape[1]), lambda j, i: (i, 0)),
                  pl.BlockSpec((d_att, tn), lambda j, i: (0, j)),
                  pl.BlockSpec((d_lru, tn), lambda j, i: (1, j)),
                  pl.BlockSpec((tm, tn), lambda j, i: (i, j)),
                  pl.BlockSpec((MOD_ROWS, tn), lambda j, i: (0, gb + j))],
        out_specs=pl.BlockSpec((tm, tn), lambda j, i: (i, j)),
        out_shape=jax.ShapeDtypeStruct((m, n), F32),
        scratch_shapes=[pltpu.VMEM((d_att, tn), BF16), pltpu.VMEM((d_lru, tn), BF16)],
        compiler_params=_params("parallel", "arbitrary"),
        name="out_proj",
    )(mix_att, y_lru, ss, w_out, w_out, x2, mod)


def _rope_tables(seq):
    pos = jnp.arange(seq)
    row = (pos // GRID_W).astype(F32)
    col = (pos % GRID_W).astype(F32)
    n_freq = HEAD_DIM // 4
    freqs = ROPE_THETA ** (-jnp.arange(n_freq, dtype=F32) / n_freq)
    ang_r = row[:, None] * freqs
    ang_c = col[:, None] * freqs
    cos = jnp.concatenate([jnp.cos(ang_r), jnp.cos(ang_r), jnp.cos(ang_c), jnp.cos(ang_c)], axis=1)
    sin = jnp.concatenate([-jnp.sin(ang_r), jnp.sin(ang_r), -jnp.sin(ang_c), jnp.sin(ang_c)], axis=1)
    return cos, sin


def kernel(x, c, ctx, c_ctx, w_ada, b_ada, norm_w, w_in, q_norm_w, k_norm_w, conv_w, conv_b,
           lru_wa, lru_ba, lru_wx, lru_bx, lru_lambda, out_norm_att, out_norm_lru, w_out):
    assert w_ada.shape[0] == 1, "single-layer kernel: only the latent stream is produced"
    b, seq, d = x.shape
    n_ctx = ctx.shape[1]
    d_att = out_norm_att.shape[1]
    d_lru = out_norm_lru.shape[1]
    d_kv = (w_in.shape[2] - 2 * d_att - 2 * d_lru) // 2
    n_lat = b * seq
    n_all = n_lat + b * n_ctx
    assert b + 1 <= MOD_ROWS
    assert n_lat % PROJ_TM == 0 and n_all % PROJ_TM == 0 and seq % PROJ_TM == 0
    tn = PROJ_TN
    k_col0, v_col0, ga_col0 = d_att, d_att + d_kv, d_att + 2 * d_kv
    xl_col0 = ga_col0 + d_att
    gl_col0 = xl_col0 + d_lru

    x2 = x.reshape(n_lat, d)
    c2 = ctx.reshape(b * n_ctx, d)

    cvec = jnp.concatenate([c, c_ctx[None, :], jnp.zeros((MOD_ROWS - b - 1, d), F32)], axis=0)
    mod = _modulation(cvec, w_ada[0], b_ada)

    h = _prenorm(x2, c2, norm_w, mod, seq)
    w_in0 = w_in[0]

    cos, sin = _rope_tables(seq)
    lat_tiles = n_lat // PROJ_TM
    qn = _proj_heads("proj_q", h, w_in0, n_lat, 0, d_att, cos, sin, q_norm_w, seq, lat_tiles,
                     HEAD_DIM ** -0.5 * LOG2_E)
    kn = _proj_heads("proj_k", h, w_in0, n_all, k_col0, d_kv, cos, sin, k_norm_w, seq, lat_tiles, 1.0)
    vb = _proj_plain("proj_v", h, w_in0, n_all, lambda j: v_col0 // tn + j, d_kv // tn, BF16)
    xl = _proj_plain("proj_xl", h, w_in0, n_all, lambda j: xl_col0 // tn + j, d_lru // tn, F32)
    ga_tiles = d_att // tn
    gates = _proj_plain("proj_gates", h, w_in0, n_lat,
                        lambda j: jnp.where(j < ga_tiles, ga_col0 // tn + j, gl_col0 // tn + j - ga_tiles),
                        (d_att + d_lru) // tn, F32)

    mix_att = _attention(qn, kn, vb, gates, out_norm_att, b, seq, n_ctx)

    n_blocks = d_lru // LRU_BLOCK_DIM
    w_cat = jnp.concatenate([lru_wa[0], lru_wx[0]], axis=-1).astype(BF16)
    bias_cat = 0.5 * jnp.concatenate([lru_ba[0].reshape(2, n_blocks, 1, LRU_BLOCK_DIM),
                                      lru_bx[0].reshape(2, n_blocks, 1, LRU_BLOCK_DIM)], axis=-1)
    y_lru, ss = _lru(xl, gates, conv_w[0], conv_b, w_cat, bias_cat, lru_lambda[0], out_norm_lru,
                     b, seq, n_ctx, d_att)

    out = _out_proj(mix_att, y_lru, ss, w_out[0], x2, mod, seq, 2 * d)
    return out.reshape(b, seq, d)
```

```python
import functools
import math

import jax
import jax.numpy as jnp
from jax import lax
from jax.experimental import pallas as pl
from jax.experimental.pallas import tpu as pltpu

F32 = jnp.float32
BF16 = jnp.bfloat16

HEAD_DIM = 128
GRID_W = 64
Q_PER_KV = 4
LRU_BLOCK_DIM = 128
CONV_WIDTH = 4
CONV_LEFT = 2
LRU_C = 8.0
ROPE_THETA = 10000.0
EPS = 1e-6
LOG2_E = 1.4426950408889634

LANES = 128
SUBLANES = 8
MXU_WIDTH = 256
MOD_ROWS = 8
VMEM_LIMIT = 56 * 1024 * 1024


def _params(*sem):
    return pltpu.CompilerParams(dimension_semantics=sem, vmem_limit_bytes=VMEM_LIMIT)


def _silu(x):
    return x * jax.nn.sigmoid(x)


def _rms(x, w):
    ms = jnp.mean(x * x, axis=-1, keepdims=True)
    return x * lax.rsqrt(ms + EPS) * w


def _mod_kernel(c_ref, w_ref, b_ref, o_ref):
    s = _silu(c_ref[...]).astype(BF16)
    o_ref[...] = jnp.dot(s, w_ref[...].astype(BF16), preferred_element_type=F32) + b_ref[...]


def _modulation(cvec, w_ada, b_ada, tn=512):
    d, n = w_ada.shape
    return pl.pallas_call(
        _mod_kernel,
        grid=(n // tn,),
        in_specs=[pl.BlockSpec((MOD_ROWS, d), lambda j: (0, 0)),
                  pl.BlockSpec((d, tn), lambda j: (0, j)),
                  pl.BlockSpec((1, tn), lambda j: (0, j))],
        out_specs=pl.BlockSpec((MOD_ROWS, tn), lambda j: (0, j)),
        out_shape=jax.ShapeDtypeStruct((MOD_ROWS, n), F32),
        compiler_params=_params("parallel"),
        name="modulation",
    )(cvec, w_ada, b_ada)


def _prenorm_kernel(x_ref, c_ref, nw_ref, mod_ref, o_ref, *, d, n_x_tiles, tiles_per_batch, ctx_row):
    t = pl.program_id(0)

    def emit(src_ref, row):
        y = _rms(src_ref[...], nw_ref[...])
        shift = mod_ref[pl.ds(row, 1), 0:d]
        scale = mod_ref[pl.ds(row, 1), d:2 * d]
        o_ref[...] = (y * (1.0 + scale) + shift).astype(BF16)

    @pl.when(t < n_x_tiles)
    def _():
        emit(x_ref, t // tiles_per_batch)

    @pl.when(t >= n_x_tiles)
    def _():
        emit(c_ref, ctx_row)


def _prenorm(x2, c2, norm_w, mod, seq, tr=256):
    mx, d = x2.shape
    mc = c2.shape[0]
    nx, nc = mx // tr, mc // tr
    kern = functools.partial(_prenorm_kernel, d=d, n_x_tiles=nx, tiles_per_batch=seq // tr,
                             ctx_row=mx // seq)
    return pl.pallas_call(
        kern,
        grid=(nx + nc,),
        in_specs=[pl.BlockSpec((tr, d), lambda t: (jnp.minimum(t, nx - 1), 0)),
                  pl.BlockSpec((tr, d), lambda t: (jnp.maximum(t - nx, 0), 0)),
                  pl.BlockSpec((1, d), lambda t: (0, 0)),
                  pl.BlockSpec(mod.shape, lambda t: (0, 0))],
        out_specs=pl.BlockSpec((tr, d), lambda t: (t, 0)),
        out_shape=jax.ShapeDtypeStruct((mx + mc, d), BF16),
        compiler_params=_params("parallel"),
        name="prenorm",
    )(x2, c2, norm_w, mod)


WIDE_TN = 1024
NARROW_TN = 512


def _proj_tiles(n_cols):
    if n_cols % WIDE_TN == 0:
        return 512, WIDE_TN, 1
    return 1024, NARROW_TN, 2


def _weight_spec(k, tn, buffers, index_map):
    if buffers == 1:
        return pl.BlockSpec((k, tn), index_map, pipeline_mode=pl.Buffered(1))
    return pl.BlockSpec((k, tn), index_map)


def _cast_weights_once(w_ref, wb_ref):
    @pl.when(pl.program_id(1) == 0)
    def _():
        wb_ref[...] = w_ref[...].astype(BF16)


def _rope(y, cos, sin_signed):
    lane = lax.broadcasted_iota(jnp.int32, y.shape, 1)
    first_half = (lane & (HEAD_DIM // 4)) == 0
    partner = jnp.where(first_half, pltpu.roll(y, HEAD_DIM - HEAD_DIM // 4, 1),
                        pltpu.roll(y, HEAD_DIM // 4, 1))
    return y * cos + partner * sin_signed


def _proj_heads_kernel(h_ref, w_ref, cos_ref, sin_ref, nw_ref, o_ref, wb_ref, acc0_ref, acc1_ref,
                       *, n_row_tiles, rope_tiles, out_scale):
    i = pl.program_id(1)
    _cast_weights_once(w_ref, wb_ref)
    accs = (acc0_ref, acc1_ref)

    def matmul_into(acc_ref):
        acc_ref[...] = jnp.dot(h_ref[...], wb_ref[...], preferred_element_type=F32)

    def finish_from(acc_ref, rope):
        norm_w = nw_ref[...] * out_scale
        for hh in range(o_ref.shape[1] // HEAD_DIM):
            sl = slice(hh * HEAD_DIM, (hh + 1) * HEAD_DIM)
            y = _rms(acc_ref[:, sl], norm_w)
            if rope:
                y = _rope(y, cos_ref[...], sin_ref[...])
            o_ref[:, sl] = y.astype(BF16)

    @pl.when(i == 0)
    def _():
        matmul_into(accs[0])

    for parity in range(2):
        @pl.when((i >= 1) & (i < n_row_tiles) & (i % 2 == parity))
        def _(parity=parity):
            matmul_into(accs[parity])
            finish_from(accs[1 - parity], True)

    @pl.when(i == n_row_tiles)
    def _():
        finish_from(accs[(n_row_tiles - 1) % 2], n_row_tiles - 1 < rope_tiles)


def _proj_heads(name, h, w_in, n_rows, col0, n_cols, cos, sin, norm_w, seq, rope_rows, out_scale):
    k = h.shape[1]
    tm, tn, w_buffers = _proj_tiles(n_cols)
    assert n_rows % tm == 0 and seq % tm == 0 and col0 % tn == 0
    n_row_tiles = n_rows // tm
    rope_tiles = rope_rows // tm
    assert n_row_tiles - 2 < rope_tiles
    spt = seq // tm
    kern = functools.partial(_proj_heads_kernel, n_row_tiles=n_row_tiles, rope_tiles=rope_tiles,
                             out_scale=out_scale)
    done = lambda i: jnp.maximum(i - 1, 0)
    return pl.pallas_call(
        kern,
        grid=(n_cols // tn, n_row_tiles + 1),
        in_specs=[pl.BlockSpec((tm, k), lambda j, i: (jnp.minimum(i, n_row_tiles - 1), 0)),
                  _weight_spec(k, tn, w_buffers, lambda j, i: (0, col0 // tn + j)),
                  pl.BlockSpec((tm, HEAD_DIM), lambda j, i: (done(i) % spt, 0)),
                  pl.BlockSpec((tm, HEAD_DIM), lambda j, i: (done(i) % spt, 0)),
                  pl.BlockSpec((1, HEAD_DIM), lambda j, i: (0, 0))],
        out_specs=pl.BlockSpec((tm, tn), lambda j, i: (done(i), j)),
        out_shape=jax.ShapeDtypeStruct((n_rows, n_cols), BF16),
        scratch_shapes=[pltpu.VMEM((k, tn), BF16),
                        pltpu.VMEM((tm, tn), F32),
                        pltpu.VMEM((tm, tn), F32)],
        compiler_params=_params("parallel", "arbitrary"),
        name=name,
    )(h, w_in, cos, sin, norm_w)


def _proj_plain_kernel(h_ref, w_ref, o_ref, wb_ref):
    _cast_weights_once(w_ref, wb_ref)
    o_ref[...] = jnp.dot(h_ref[...], wb_ref[...], preferred_element_type=F32).astype(o_ref.dtype)


def _proj_plain(name, h, w_in, n_rows, col_ranges, out_dtype):
    k = h.shape[1]
    tm, tn, w_buffers = _proj_tiles(math.gcd(*[c for r in col_ranges for c in r]))
    assert n_rows % tm == 0
    tiles = [(start // tn, width // tn) for start, width in col_ranges]

    def col_tile(j):
        first, idx = 0, tiles[0][0] + j
        for r in range(1, len(tiles)):
            first += tiles[r - 1][1]
            idx = jnp.where(j >= first, tiles[r][0] + j - first, idx)
        return idx

    n_col_tiles = sum(n for _, n in tiles)
    return pl.pallas_call(
        _proj_plain_kernel,
        grid=(n_col_tiles, n_rows // tm),
        in_specs=[pl.BlockSpec((tm, k), lambda j, i: (i, 0)),
                  _weight_spec(k, tn, w_buffers, lambda j, i: (0, col_tile(j)))],
        out_specs=pl.BlockSpec((tm, tn), lambda j, i: (i, j)),
        out_shape=jax.ShapeDtypeStruct((n_rows, n_col_tiles * tn), out_dtype),
        scratch_shapes=[pltpu.VMEM((k, tn), BF16)],
        compiler_params=_params("parallel", "arbitrary"),
        name=name,
    )(h, w_in)


V_EXTRA_ROWS = 16


def _attn_kernel(q_ref, kl_ref, vl_ref, kc_ref, vc_ref, ga_ref, nw_ref, o_ref,
                 k_ref, vt_ref, st_ref, att_ref, gate_ref):
    tq = q_ref.shape[0]
    seq, n_ctx = kl_ref.shape[0], kc_ref.shape[0]
    n_kv = kl_ref.shape[1] // HEAD_DIM
    nt = (((1,), (1,)), ((), ()))

    n_keys = seq + n_ctx

    @pl.when(pl.program_id(1) == 0)
    def _():
        k_ref[0:seq, :] = kl_ref[...]
        k_ref[seq:n_keys, :] = kc_ref[...]
        row = lax.broadcasted_iota(jnp.int32, (V_EXTRA_ROWS, n_keys), 0)
        tail = jnp.where(row == 0, 1.0, 0.0).astype(BF16)
        for kv in range(n_kv):
            ks = slice(kv * HEAD_DIM, (kv + 1) * HEAD_DIM)
            vt_ref[kv, 0:HEAD_DIM, 0:seq] = vl_ref[:, ks].T
            vt_ref[kv, 0:HEAD_DIM, seq:n_keys] = vc_ref[:, ks].T
            vt_ref[kv, HEAD_DIM:HEAD_DIM + V_EXTRA_ROWS, :] = tail

    def head_slices(kv):
        return [slice((kv * Q_PER_KV + g) * HEAD_DIM, (kv * Q_PER_KV + g + 1) * HEAD_DIM)
                for g in range(Q_PER_KV)]

    def scores(kv, slot):
        q4 = jnp.concatenate([q_ref[:, hs] for hs in head_slices(kv)], axis=0)
        st_ref[slot] = lax.dot_general(k_ref[:, kv * HEAD_DIM:(kv + 1) * HEAD_DIM], q4, nt,
                                       preferred_element_type=F32)

    def softmax_values(kv, slot):
        m = jnp.max(st_ref[slot], axis=0, keepdims=True)
        pt = jnp.exp2(st_ref[slot] - m).astype(BF16)
        res = jnp.dot(vt_ref[kv], pt, preferred_element_type=F32)
        ot = res[:HEAD_DIM, :] / res[HEAD_DIM:HEAD_DIM + 1, :]
        for g, hs in enumerate(head_slices(kv)):
            att_ref[:, hs] = ot[:, g * tq:(g + 1) * tq].T

    scores(0, 0)
    gate_ref[...] = _silu(ga_ref[...]) * nw_ref[...]
    for kv in range(n_kv):
        if kv + 1 < n_kv:
            scores(kv + 1, (kv + 1) % 2)
        softmax_values(kv, kv % 2)
    att = att_ref[...]
    inv_rms = lax.rsqrt(jnp.mean(att * att, axis=-1, keepdims=True) + EPS)
    o_ref[...] = (att * inv_rms * gate_ref[...]).astype(BF16)


def _attention(qn, kn, vb, gates, norm_w, b, seq, n_ctx, tq=256):
    n_lat, d_att = qn.shape
    d_kv = kn.shape[1]
    qt = seq // tq
    ctx_blk0 = n_lat // n_ctx
    return pl.pallas_call(
        _attn_kernel,
        grid=(b, qt),
        in_specs=[pl.BlockSpec((tq, d_att), lambda bi, i: (bi * qt + i, 0)),
                  pl.BlockSpec((seq, d_kv), lambda bi, i: (bi, 0)),
                  pl.BlockSpec((seq, d_kv), lambda bi, i: (bi, 0)),
                  pl.BlockSpec((n_ctx, d_kv), lambda bi, i: (ctx_blk0 + bi, 0)),
                  pl.BlockSpec((n_ctx, d_kv), lambda bi, i: (ctx_blk0 + bi, 0)),
                  pl.BlockSpec((tq, d_att), lambda bi, i: (bi * qt + i, 0)),
                  pl.BlockSpec((1, d_att), lambda bi, i: (0, 0))],
        out_specs=pl.BlockSpec((tq, d_att), lambda bi, i: (bi * qt + i, 0)),
        out_shape=jax.ShapeDtypeStruct((n_lat, d_att), BF16),
        scratch_shapes=[pltpu.VMEM((seq + n_ctx, d_kv), BF16),
                        pltpu.VMEM((d_kv // HEAD_DIM, HEAD_DIM + V_EXTRA_ROWS, seq + n_ctx), BF16),
                        pltpu.VMEM((2, seq + n_ctx, Q_PER_KV * tq), F32),
                        pltpu.VMEM((tq, d_att), F32),
                        pltpu.VMEM((tq, d_att), F32)],
        compiler_params=_params("parallel", "arbitrary"),
        name="attention",
    )(qn, kn, vb, kn, vb, gates, norm_w)


CONV_PAD_ROWS = 8
CONV_PIECE = 32
LRU_SEGS = SUBLANES
GATE_ROWS = 256
SCAN_STEPS = 4


def _softplus(z):
    return jnp.maximum(z, 0.0) + jnp.log1p(jnp.exp(-jnp.abs(z)))


def _segment_len(total):
    seg = -(-total // LRU_SEGS)
    return seg + (SUBLANES // 2 - seg) % SUBLANES


def _scan8(a, bv, reverse):
    row = lax.broadcasted_iota(jnp.int32, a.shape, 0)
    for s in (1, 2, 4):
        if reverse:
            keep = row < SUBLANES - s
            shift = SUBLANES - s
        else:
            keep = row >= s
            shift = s
        a_prev = jnp.where(keep, pltpu.roll(a, shift, 0), 1.0)
        b_prev = jnp.where(keep, pltpu.roll(bv, shift, 0), 0.0)
        bv = bv + a * b_prev
        a = a * a_prev
    return a, bv


def _lru_kernel(xl_ref, xc_ref, cw_ref, cb_ref, w_ref, bias_ref, lam_ref, gl_ref, nw_ref, y_ref, ss_ref,
                xp_ref, u_ref, a_ref, b_ref, hs_ref, *, seq, n_ctx, seg_len):
    cg = y_ref.shape[1]
    n_blk = cg // LRU_BLOCK_DIM
    total = n_ctx + seq
    padded = LRU_SEGS * seg_len
    n_chunks = total // GATE_ROWS
    gate_unroll = next(u for u in (3, 2, 1) if n_chunks % u == 0)
    for n in range(n_blk):
        a_ref[n, total:padded, :] = jnp.ones((padded - total, LRU_BLOCK_DIM), F32)
        b_ref[n, total:padded, :] = jnp.zeros((padded - total, LRU_BLOCK_DIM), F32)
    zeros_pad = jnp.zeros((CONV_PAD_ROWS, cg), F32)
    halo = CONV_PIECE + 2 * CONV_PAD_ROWS
    half_w = [0.5 * cw_ref[j:j + 1, :] for j in range(CONV_WIDTH)]
    half_b = 0.5 * cb_ref[...]

    def conv_into(src_ref, n_rows, u_rows):
        xp_ref[0:CONV_PAD_ROWS, :] = zeros_pad
        xp_ref[CONV_PAD_ROWS:CONV_PAD_ROWS + n_rows, :] = src_ref[...]
        xp_ref[CONV_PAD_ROWS + n_rows:2 * CONV_PAD_ROWS + n_rows, :] = zeros_pad
        for p in range(n_rows // CONV_PIECE):
            window = xp_ref[p * CONV_PIECE:p * CONV_PIECE + halo, :]
            acc = half_b
            for j in range(CONV_WIDTH):
                off = j - CONV_LEFT
                shifted = window if off == 0 else pltpu.roll(window, (-off) % halo, 0)
                acc = acc + half_w[j] * shifted[CONV_PAD_ROWS:CONV_PAD_ROWS + CONV_PIECE, :]
            for u0 in u_rows:
                u_ref[u0 + p * CONV_PIECE:u0 + (p + 1) * CONV_PIECE, :] = acc

    conv_into(xc_ref, n_ctx, (0, n_ctx + seq))
    conv_into(xl_ref, seq, (n_ctx,))

    for d in range(2):
        reverse = d == 1
        seq_row0 = n_ctx if reverse else 0
        lat_row0 = 0 if reverse else n_ctx
        half_decay = [0.5 * LRU_C * _softplus(-lam_ref[d:d + 1, n * LRU_BLOCK_DIM:(n + 1) * LRU_BLOCK_DIM])
                      for n in range(n_blk)]
        exp2_scale = [-LOG2_E * hd for hd in half_decay]

        def gate_body(cb, _, d=d, seq_row0=seq_row0, half_decay=half_decay, exp2_scale=exp2_scale):
            for cc in range(gate_unroll):
                s0 = pl.multiple_of((cb * gate_unroll + cc) * GATE_ROWS, GATE_ROWS)
                hu = u_ref[pl.ds(pl.multiple_of(seq_row0 + s0, SUBLANES), GATE_ROWS), :]
                hub = hu.astype(BF16)
                for n in range(n_blk):
                    sl = slice(n * LRU_BLOCK_DIM, (n + 1) * LRU_BLOCK_DIM)
                    pre = jnp.dot(hub[:, sl], w_ref[d, n], preferred_element_type=F32) + bias_ref[d, n]
                    t_r = jnp.tanh(pre[:, :LRU_BLOCK_DIM])
                    t_i = jnp.tanh(pre[:, LRU_BLOCK_DIM:])
                    neg_log_a = t_r * half_decay[n] + half_decay[n]
                    a = jnp.exp2(t_r * exp2_scale[n] + exp2_scale[n])
                    a_ref[n, pl.ds(s0, GATE_ROWS), :] = a
                    w = jnp.tanh(neg_log_a) * (1.0 + a * a)
                    mult = jnp.where(w > 0.0, w * lax.rsqrt(w), 0.0)
                    b_ref[n, pl.ds(s0, GATE_ROWS), :] = mult * ((t_i + 1.0) * hu[:, sl])
            return 0

        lax.fori_loop(0, n_chunks // gate_unroll, gate_body, 0)

        def load_steps(tb, reverse=reverse):
            rows = []
            base = (seg_len // SCAN_STEPS - 1 - tb) * SCAN_STEPS if reverse else tb * SCAN_STEPS
            for k in range(SCAN_STEPS):
                i = base + (SCAN_STEPS - 1 - k if reverse else k)
                idx = pl.ds(i, LRU_SEGS, stride=seg_len)
                rows.append((idx, [a_ref[n, idx, :] for n in range(n_blk)],
                             [b_ref[n, idx, :] for n in range(n_blk)]))
            return rows

        def local_body(tb, carry):
            hs, decs = list(carry[0]), list(carry[1])
            for _, a, bv in load_steps(tb):
                for n in range(n_blk):
                    hs[n] = a[n] * hs[n] + bv[n]
                    decs[n] = decs[n] * a[n]
            return tuple(hs), tuple(decs)

        zero = tuple(jnp.zeros((LRU_SEGS, LRU_BLOCK_DIM), F32) for _ in range(n_blk))
        one = tuple(jnp.ones((LRU_SEGS, LRU_BLOCK_DIM), F32) for _ in range(n_blk))
        h_end, dec_end = lax.fori_loop(0, seg_len // SCAN_STEPS, local_body, (zero, one))

        starts = []
        for n in range(n_blk):
            _, state = _scan8(dec_end[n], h_end[n], reverse)
            row = lax.broadcasted_iota(jnp.int32, state.shape, 0)
            if reverse:
                starts.append(jnp.where(row < LRU_SEGS - 1, pltpu.roll(state, LRU_SEGS - 1, 0), 0.0))
            else:
                starts.append(jnp.where(row >= 1, pltpu.roll(state, 1, 0), 0.0))

        def final_body(tb, hs):
            hs = list(hs)
            for idx, a, bv in load_steps(tb):
                for n in range(n_blk):
                    hs[n] = a[n] * hs[n] + bv[n]
                    hs_ref[n, idx, :] = hs[n]
            return tuple(hs)

        lax.fori_loop(0, seg_len // SCAN_STEPS, final_body, tuple(starts))

        if not reverse:
            for n in range(n_blk):
                sl = slice(n * LRU_BLOCK_DIM, (n + 1) * LRU_BLOCK_DIM)
                for r0 in range(0, seq, GATE_ROWS):
                    xp_ref[r0:r0 + GATE_ROWS, sl] = hs_ref[n, lat_row0 + r0:lat_row0 + r0 + GATE_ROWS, :]
        else:
            for r0 in range(0, seq, GATE_ROWS):
                rows = slice(r0, r0 + GATE_ROWS)
                lru = xp_ref[rows, :] + jnp.concatenate(
                    [hs_ref[n, lat_row0 + r0:lat_row0 + r0 + GATE_ROWS, :] for n in range(n_blk)], axis=1)
                y_ref[rows, :] = (lru * nw_ref[...] * _silu(gl_ref[rows, :])).astype(BF16)
                sumsq = jnp.broadcast_to(jnp.sum(lru * lru, axis=-1, keepdims=True), (GATE_ROWS, LANES))

                @pl.when(pl.program_id(1) == 0)
                def _(rows=rows, sumsq=sumsq):
                    ss_ref[rows, :] = sumsq

                @pl.when(pl.program_id(1) != 0)
                def _(rows=rows, sumsq=sumsq):
                    ss_ref[rows, :] += sumsq


def _lru(xl, gates, conv_w, conv_b, w_cat, bias_cat, lam, norm_w, b, seq, n_ctx, d_att, cg=512):
    n_lat = b * seq
    d_lru = xl.shape[1]
    n_blk = cg // LRU_BLOCK_DIM
    n_groups = d_lru // cg
    seg_len = _segment_len(n_ctx + seq)
    assert n_ctx % GATE_ROWS == 0 and seq % GATE_ROWS == 0 and GATE_ROWS % CONV_PIECE == 0
    assert seg_len % SCAN_STEPS == 0
    kern = functools.partial(_lru_kernel, seq=seq, n_ctx=n_ctx, seg_len=seg_len)
    ctx_blk0 = n_lat // n_ctx
    gl_blk0 = d_att // cg
    scan_rows = LRU_SEGS * seg_len
    return pl.pallas_call(
        kern,
        grid=(b, n_groups),
        in_specs=[pl.BlockSpec((seq, cg), lambda bi, g: (bi, g)),
                  pl.BlockSpec((n_ctx, cg), lambda bi, g: (ctx_blk0 + bi, g)),
                  pl.BlockSpec((CONV_WIDTH, cg), lambda bi, g: (0, g)),
                  pl.BlockSpec((1, cg), lambda bi, g: (0, g)),
                  pl.BlockSpec((2, n_blk, LRU_BLOCK_DIM, 2 * LRU_BLOCK_DIM), lambda bi, g: (0, g, 0, 0)),
                  pl.BlockSpec((2, n_blk, 1, 2 * LRU_BLOCK_DIM), lambda bi, g: (0, g, 0, 0)),
                  pl.BlockSpec((2, cg), lambda bi, g: (0, g)),
                  pl.BlockSpec((seq, cg), lambda bi, g: (bi, gl_blk0 + g)),
                  pl.BlockSpec((1, cg), lambda bi, g: (0, g))],
        out_specs=[pl.BlockSpec((seq, cg), lambda bi, g: (bi, g)),
                   pl.BlockSpec((seq, LANES), lambda bi, g: (bi, 0))],
        out_shape=[jax.ShapeDtypeStruct((n_lat, d_lru), BF16),
                   jax.ShapeDtypeStruct((n_lat, LANES), F32)],
        scratch_shapes=[pltpu.VMEM((seq + 2 * CONV_PAD_ROWS, cg), F32),
                        pltpu.VMEM((2 * n_ctx + seq, cg), F32),
                        pltpu.VMEM((n_blk, scan_rows, LRU_BLOCK_DIM), F32),
                        pltpu.VMEM((n_blk, scan_rows, LRU_BLOCK_DIM), F32),
                        pltpu.VMEM((n_blk, scan_rows, LRU_BLOCK_DIM), F32)],
        compiler_params=_params("parallel", "arbitrary"),
        name="rglru",
    )(xl, xl, conv_w, conv_b, w_cat, bias_cat, lam, gates, norm_w)


def _out_kernel(ma_ref, yl_ref, ss_ref, wa_ref, wl_ref, x_ref, gate_ref, o_ref, wab_ref, wlb_ref,
                *, tiles_per_batch, d_lru):
    @pl.when(pl.program_id(1) == 0)
    def _():
        wab_ref[...] = wa_ref[...].astype(BF16)
        wlb_ref[...] = wl_ref[...].astype(BF16)

    bi = pl.program_id(1) // tiles_per_batch
    gate = gate_ref[pl.ds(bi, 1), :]
    sumsq = ss_ref[:, 0:LANES]
    for g in range(1, ss_ref.shape[1] // LANES):
        sumsq = sumsq + ss_ref[:, g * LANES:(g + 1) * LANES]
    inv_rms = lax.rsqrt(sumsq * (1.0 / d_lru) + EPS)
    inv_rms = jnp.concatenate([inv_rms] * (o_ref.shape[1] // LANES), axis=1)
    acc = jnp.dot(ma_ref[...], wab_ref[...], preferred_element_type=F32)
    acc = acc + inv_rms * jnp.dot(yl_ref[...], wlb_ref[...], preferred_element_type=F32)
    o_ref[...] = x_ref[...] + gate * acc


def _out_proj(mix_att, y_lru, ss, w_out, x2, mod, seq, gate_col0):
    m, d_att = mix_att.shape
    d_lru = y_lru.shape[1]
    assert d_att == d_lru
    n = w_out.shape[1]
    tm, tn, w_buffers = _proj_tiles(n)
    assert m % tm == 0 and seq % tm == 0 and gate_col0 % tn == 0
    gb = gate_col0 // tn
    kern = functools.partial(_out_kernel, tiles_per_batch=seq // tm, d_lru=d_lru)
    return pl.pallas_call(
        kern,
        grid=(n // tn, m // tm),
        in_specs=[pl.BlockSpec((tm, d_att), lambda j, i: (i, 0)),
                  pl.BlockSpec((tm, d_lru), lambda j, i: (i, 0)),
                  pl.BlockSpec((tm, ss.shape[1]), lambda j, i: (i, 0)),
                  _weight_spec(d_att, tn, w_buffers, lambda j, i: (0, j)),
                  _weight_spec(d_lru, tn, w_buffers, lambda j, i: (1, j)),
                  pl.BlockSpec((tm, tn), lambda j, i: (i, j)),
                  pl.BlockSpec((MOD_ROWS, tn), lambda j, i: (0, gb + j))],
        out_specs=pl.BlockSpec((tm, tn), lambda j, i: (i, j)),
        out_shape=jax.ShapeDtypeStruct((m, n), F32),
        scratch_shapes=[pltpu.VMEM((d_att, tn), BF16), pltpu.VMEM((d_lru, tn), BF16)],
        compiler_params=_params("parallel", "arbitrary"),
        name="out_proj",
    )(mix_att, y_lru, ss, w_out, w_out, x2, mod)


def _rope_tables(seq):
    pos = jnp.arange(seq)
    row = (pos // GRID_W).astype(F32)
    col = (pos % GRID_W).astype(F32)
    n_freq = HEAD_DIM // 4
    freqs = ROPE_THETA ** (-jnp.arange(n_freq, dtype=F32) / n_freq)
    ang_r = row[:, None] * freqs
    ang_c = col[:, None] * freqs
    cos = jnp.concatenate([jnp.cos(ang_r), jnp.cos(ang_r), jnp.cos(ang_c), jnp.cos(ang_c)], axis=1)
    sin = jnp.concatenate([-jnp.sin(ang_r), jnp.sin(ang_r), -jnp.sin(ang_c), jnp.sin(ang_c)], axis=1)
    return cos, sin


def kernel(x, c, ctx, c_ctx, w_ada, b_ada, norm_w, w_in, q_norm_w, k_norm_w, conv_w, conv_b,
           lru_wa, lru_ba, lru_wx, lru_bx, lru_lambda, out_norm_att, out_norm_lru, w_out):
    assert w_ada.shape[0] == 1, "single-layer kernel: only the latent stream is produced"
    b, seq, d = x.shape
    n_ctx = ctx.shape[1]
    d_att = out_norm_att.shape[1]
    d_lru = out_norm_lru.shape[1]
    d_kv = (w_in.shape[2] - 2 * d_att - 2 * d_lru) // 2
    n_lat = b * seq
    n_all = n_lat + b * n_ctx
    assert b + 1 <= MOD_ROWS
    k_col0, v_col0, ga_col0 = d_att, d_att + d_kv, d_att + 2 * d_kv
    xl_col0 = ga_col0 + d_att
    gl_col0 = xl_col0 + d_lru

    x2 = x.reshape(n_lat, d)
    c2 = ctx.reshape(b * n_ctx, d)

    cvec = jnp.concatenate([c, c_ctx[None, :], jnp.zeros((MOD_ROWS - b - 1, d), F32)], axis=0)
    mod = _modulation(cvec, w_ada[0], b_ada)

    h = _prenorm(x2, c2, norm_w, mod, seq)
    w_in0 = w_in[0]

    cos, sin = _rope_tables(seq)
    qn = _proj_heads("proj_q", h, w_in0, n_lat, 0, d_att, cos, sin, q_norm_w, seq, n_lat,
                     HEAD_DIM ** -0.5 * LOG2_E)
    kn = _proj_heads("proj_k", h, w_in0, n_all, k_col0, d_kv, cos, sin, k_norm_w, seq, n_lat, 1.0)
    vb = _proj_plain("proj_v", h, w_in0, n_all, [(v_col0, d_kv)], BF16)
    xl = _proj_plain("proj_xl", h, w_in0, n_all, [(xl_col0, d_lru)], F32)
    gates = _proj_plain("proj_gates", h, w_in0, n_lat, [(ga_col0, d_att), (gl_col0, d_lru)], F32)

    mix_att = _attention(qn, kn, vb, gates, out_norm_att, b, seq, n_ctx)

    n_blocks = d_lru // LRU_BLOCK_DIM
    w_cat = jnp.concatenate([lru_wa[0], lru_wx[0]], axis=-1).astype(BF16)
    bias_cat = 0.5 * jnp.concatenate([lru_ba[0].reshape(2, n_blocks, 1, LRU_BLOCK_DIM),
                                      lru_bx[0].reshape(2, n_blocks, 1, LRU_BLOCK_DIM)], axis=-1)
    y_lru, ss = _lru(xl, gates, conv_w[0], conv_b, w_cat, bias_cat, lru_lambda[0], out_norm_lru,
                     b, seq, n_ctx, d_att)

    out = _out_proj(mix_att, y_lru, ss, w_out[0], x2, mod, seq, 2 * d)
    return out.reshape(b, seq, d)
```

```python
import functools
import math

import jax
import jax.numpy as jnp
from jax import lax
from jax.experimental import pallas as pl
from jax.experimental.pallas import tpu as pltpu

F32 = jnp.float32
BF16 = jnp.bfloat16

HEAD_DIM = 128
GRID_W = 64
Q_PER_KV = 4
LRU_BLOCK_DIM = 128
CONV_WIDTH = 4
CONV_LEFT = 2
LRU_C = 8.0
ROPE_THETA = 10000.0
EPS = 1e-6
LOG2_E = 1.4426950408889634

LANES = 128
SUBLANES = 8
MXU_WIDTH = 256
MOD_ROWS = 8
VMEM_LIMIT = 56 * 1024 * 1024
MATMUL_TEMP_BYTES = 4 * 1024 * 1024


def _params(*sem):
    return pltpu.CompilerParams(dimension_semantics=sem, vmem_limit_bytes=VMEM_LIMIT)


def _silu(x):
    return x * jax.nn.sigmoid(x)


def _rms(x, w):
    ms = jnp.mean(x * x, axis=-1, keepdims=True)
    return x * lax.rsqrt(ms + EPS) * w


def _mod_kernel(c_ref, w_ref, b_ref, o_ref):
    s = _silu(c_ref[...]).astype(BF16)
    o_ref[...] = jnp.dot(s, w_ref[...].astype(BF16), preferred_element_type=F32) + b_ref[...]


def _modulation(cvec, w_ada, b_ada, tn=512):
    d, n = w_ada.shape
    return pl.pallas_call(
        _mod_kernel,
        grid=(n // tn,),
        in_specs=[pl.BlockSpec((MOD_ROWS, d), lambda j: (0, 0)),
                  pl.BlockSpec((d, tn), lambda j: (0, j)),
                  pl.BlockSpec((1, tn), lambda j: (0, j))],
        out_specs=pl.BlockSpec((MOD_ROWS, tn), lambda j: (0, j)),
        out_shape=jax.ShapeDtypeStruct((MOD_ROWS, n), F32),
        compiler_params=_params("parallel"),
        name="modulation",
    )(cvec, w_ada, b_ada)


def _prenorm_kernel(x_ref, c_ref, nw_ref, mod_ref, o_ref, *, d, n_x_tiles, tiles_per_batch, ctx_row):
    t = pl.program_id(0)

    def emit(src_ref, row):
        y = _rms(src_ref[...], nw_ref[...])
        shift = mod_ref[pl.ds(row, 1), 0:d]
        scale = mod_ref[pl.ds(row, 1), d:2 * d]
        o_ref[...] = (y * (1.0 + scale) + shift).astype(BF16)

    @pl.when(t < n_x_tiles)
    def _():
        emit(x_ref, t // tiles_per_batch)

    @pl.when(t >= n_x_tiles)
    def _():
        emit(c_ref, ctx_row)


def _prenorm(x2, c2, norm_w, mod, seq, tr=512):
    mx, d = x2.shape
    mc = c2.shape[0]
    nx, nc = mx // tr, mc // tr
    kern = functools.partial(_prenorm_kernel, d=d, n_x_tiles=nx, tiles_per_batch=seq // tr,
                             ctx_row=mx // seq)
    return pl.pallas_call(
        kern,
        grid=(nx + nc,),
        in_specs=[pl.BlockSpec((tr, d), lambda t: (jnp.minimum(t, nx - 1), 0)),
                  pl.BlockSpec((tr, d), lambda t: (jnp.maximum(t - nx, 0), 0)),
                  pl.BlockSpec((1, d), lambda t: (0, 0)),
                  pl.BlockSpec(mod.shape, lambda t: (0, 0))],
        out_specs=pl.BlockSpec((tr, d), lambda t: (t, 0)),
        out_shape=jax.ShapeDtypeStruct((mx + mc, d), BF16),
        compiler_params=_params("parallel"),
        name="prenorm",
    )(x2, c2, norm_w, mod)


WIDE_TN = 1024
NARROW_TN = 512


def _proj_tiles(n_cols, k, resident_bytes):
    tm, tn = (512, WIDE_TN) if n_cols % WIDE_TN == 0 else (1024, NARROW_TN)
    w_tile = k * tn * 4
    fits = resident_bytes(tm, tn) + k * tn * 2 + 2 * w_tile + MATMUL_TEMP_BYTES <= VMEM_LIMIT
    return tm, tn, 2 if fits else 1


def _weight_spec(k, tn, buffers, index_map):
    if buffers == 1:
        return pl.BlockSpec((k, tn), index_map, pipeline_mode=pl.Buffered(1))
    return pl.BlockSpec((k, tn), index_map)


def _cast_weights_once(w_ref, wb_ref):
    @pl.when(pl.program_id(1) == 0)
    def _():
        wb_ref[...] = w_ref[...].astype(BF16)


def _rope(y, cos, sin_signed):
    lane = lax.broadcasted_iota(jnp.int32, y.shape, 1)
    first_half = (lane & (HEAD_DIM // 4)) == 0
    partner = jnp.where(first_half, pltpu.roll(y, HEAD_DIM - HEAD_DIM // 4, 1),
                        pltpu.roll(y, HEAD_DIM // 4, 1))
    return y * cos + partner * sin_signed


def _proj_heads_kernel(h_ref, w_ref, cos_ref, sin_ref, nw_ref, o_ref, wb_ref, acc0_ref, acc1_ref,
                       *, n_row_tiles, rope_tiles, out_scale):
    i = pl.program_id(1)
    _cast_weights_once(w_ref, wb_ref)
    accs = (acc0_ref, acc1_ref)

    def matmul_into(acc_ref):
        acc_ref[...] = jnp.dot(h_ref[...], wb_ref[...], preferred_element_type=F32)

    def finish_from(acc_ref, rope):
        norm_w = nw_ref[...] * out_scale
        for hh in range(o_ref.shape[1] // HEAD_DIM):
            sl = slice(hh * HEAD_DIM, (hh + 1) * HEAD_DIM)
            y = _rms(acc_ref[:, sl], norm_w)
            if rope:
                y = _rope(y, cos_ref[...], sin_ref[...])
            o_ref[:, sl] = y.astype(BF16)

    @pl.when(i == 0)
    def _():
        matmul_into(accs[0])

    for parity in range(2):
        @pl.when((i >= 1) & (i < n_row_tiles) & (i % 2 == parity))
        def _(parity=parity):
            matmul_into(accs[parity])
            finish_from(accs[1 - parity], True)

    @pl.when(i == n_row_tiles)
    def _():
        finish_from(accs[(n_row_tiles - 1) % 2], n_row_tiles - 1 < rope_tiles)


def _proj_heads(name, h, w_in, n_rows, col0, n_cols, cos, sin, norm_w, seq, rope_rows, out_scale):
    k = h.shape[1]
    tm, tn, w_buffers = _proj_tiles(
        n_cols, k, lambda tm, tn: 2 * tm * k * 2 + 2 * tm * tn * 4 + 2 * tm * tn * 2 + 4 * tm * HEAD_DIM * 4)
    assert n_rows % tm == 0 and seq % tm == 0 and col0 % tn == 0
    n_row_tiles = n_rows // tm
    rope_tiles = rope_rows // tm
    assert n_row_tiles - 2 < rope_tiles
    spt = seq // tm
    kern = functools.partial(_proj_heads_kernel, n_row_tiles=n_row_tiles, rope_tiles=rope_tiles,
                             out_scale=out_scale)
    done = lambda i: jnp.maximum(i - 1, 0)
    return pl.pallas_call(
        kern,
        grid=(n_cols // tn, n_row_tiles + 1),
        in_specs=[pl.BlockSpec((tm, k), lambda j, i: (jnp.minimum(i, n_row_tiles - 1), 0)),
                  _weight_spec(k, tn, w_buffers, lambda j, i: (0, col0 // tn + j)),
                  pl.BlockSpec((tm, HEAD_DIM), lambda j, i: (done(i) % spt, 0)),
                  pl.BlockSpec((tm, HEAD_DIM), lambda j, i: (done(i) % spt, 0)),
                  pl.BlockSpec((1, HEAD_DIM), lambda j, i: (0, 0))],
        out_specs=pl.BlockSpec((tm, tn), lambda j, i: (done(i), j)),
        out_shape=jax.ShapeDtypeStruct((n_rows, n_cols), BF16),
        scratch_shapes=[pltpu.VMEM((k, tn), BF16),
                        pltpu.VMEM((tm, tn), F32),
                        pltpu.VMEM((tm, tn), F32)],
        compiler_params=_params("parallel", "arbitrary"),
        name=name,
    )(h, w_in, cos, sin, norm_w)


def _proj_plain_kernel(h_ref, w_ref, o_ref, wb_ref):
    _cast_weights_once(w_ref, wb_ref)
    o_ref[...] = jnp.dot(h_ref[...], wb_ref[...], preferred_element_type=F32).astype(o_ref.dtype)


def _proj_plain(name, h, w_in, n_rows, col_ranges, out_dtype):
    k = h.shape[1]
    out_bytes = jnp.dtype(out_dtype).itemsize
    tm, tn, w_buffers = _proj_tiles(math.gcd(*[c for r in col_ranges for c in r]), k,
                                    lambda tm, tn: 2 * tm * k * 2 + 2 * tm * tn * out_bytes)
    assert n_rows % tm == 0
    tiles = [(start // tn, width // tn) for start, width in col_ranges]

    def col_tile(j):
        first, idx = 0, tiles[0][0] + j
        for r in range(1, len(tiles)):
            first += tiles[r - 1][1]
            idx = jnp.where(j >= first, tiles[r][0] + j - first, idx)
        return idx

    n_col_tiles = sum(n for _, n in tiles)
    return pl.pallas_call(
        _proj_plain_kernel,
        grid=(n_col_tiles, n_rows // tm),
        in_specs=[pl.BlockSpec((tm, k), lambda j, i: (i, 0)),
                  _weight_spec(k, tn, w_buffers, lambda j, i: (0, col_tile(j)))],
        out_specs=pl.BlockSpec((tm, tn), lambda j, i: (i, j)),
        out_shape=jax.ShapeDtypeStruct((n_rows, n_col_tiles * tn), out_dtype),
        scratch_shapes=[pltpu.VMEM((k, tn), BF16)],
        compiler_params=_params("parallel", "arbitrary"),
        name=name,
    )(h, w_in)


V_EXTRA_ROWS = 16
ATTN_VMEM_LIMIT = 62 * 1024 * 1024


def _attn_kernel(q_ref, kl_ref, vl_ref, kc_ref, vc_ref, ga_ref, nw_ref, o_ref,
                 k_ref, vt_ref, st_ref, att_ref, *, n_q_tiles):
    tq = q_ref.shape[0]
    seq, n_ctx = kl_ref.shape[0], kc_ref.shape[0]
    n_kv = kl_ref.shape[1] // HEAD_DIM
    nt = (((1,), (1,)), ((), ()))
    i = pl.program_id(1)
    n_keys = seq + n_ctx

    @pl.when(i == 0)
    def _():
        k_ref[0:seq, :] = kl_ref[...]
        k_ref[seq:n_keys, :] = kc_ref[...]
        row = lax.broadcasted_iota(jnp.int32, (V_EXTRA_ROWS, n_keys), 0)
        tail = jnp.where(row == 0, 1.0, 0.0).astype(BF16)
        for kv in range(n_kv):
            ks = slice(kv * HEAD_DIM, (kv + 1) * HEAD_DIM)
            vt_ref[kv, 0:HEAD_DIM, 0:seq] = vl_ref[:, ks].T
            vt_ref[kv, 0:HEAD_DIM, seq:n_keys] = vc_ref[:, ks].T
            vt_ref[kv, HEAD_DIM:HEAD_DIM + V_EXTRA_ROWS, :] = tail

    def head_slices(kv):
        return [slice((kv * Q_PER_KV + g) * HEAD_DIM, (kv * Q_PER_KV + g + 1) * HEAD_DIM)
                for g in range(Q_PER_KV)]

    def scores(kv, slot):
        q4 = jnp.concatenate([q_ref[:, hs] for hs in head_slices(kv)], axis=0)
        st_ref[slot] = lax.dot_general(k_ref[:, kv * HEAD_DIM:(kv + 1) * HEAD_DIM], q4, nt,
                                       preferred_element_type=F32)

    def softmax_values(kv, slot, tile_slot):
        m = jnp.max(st_ref[slot], axis=0, keepdims=True)
        pt = jnp.exp2(st_ref[slot] - m).astype(BF16)
        res = jnp.dot(vt_ref[kv], pt, preferred_element_type=F32)
        ot = res[:HEAD_DIM, :] / res[HEAD_DIM:HEAD_DIM + 1, :]
        for g, hs in enumerate(head_slices(kv)):
            att_ref[tile_slot, :, hs] = ot[:, g * tq:(g + 1) * tq].T

    def attend(tile_slot):
        scores(0, 0)
        for kv in range(n_kv):
            if kv + 1 < n_kv:
                scores(kv + 1, (kv + 1) % 2)
            softmax_values(kv, kv % 2, tile_slot)

    def finish(tile_slot):
        att = att_ref[tile_slot]
        inv_rms = lax.rsqrt(jnp.mean(att * att, axis=-1, keepdims=True) + EPS)
        o_ref[...] = (att * inv_rms * (_silu(ga_ref[...]) * nw_ref[...])).astype(BF16)

    @pl.when(i == 0)
    def _():
        attend(0)

    @pl.when((i >= 1) & (i < n_q_tiles))
    def _():
        attend(i % 2)
        finish((i - 1) % 2)

    @pl.when(i == n_q_tiles)
    def _():
        finish((n_q_tiles - 1) % 2)


def _attention(qn, kn, vb, gates, norm_w, b, seq, n_ctx, tq=256):
    n_lat, d_att = qn.shape
    d_kv = kn.shape[1]
    qt = seq // tq
    ctx_blk0 = n_lat // n_ctx
    this = lambda bi, i: (bi * qt + jnp.minimum(i, qt - 1), 0)
    done = lambda bi, i: (bi * qt + jnp.maximum(i - 1, 0), 0)
    return pl.pallas_call(
        functools.partial(_attn_kernel, n_q_tiles=qt),
        grid=(b, qt + 1),
        in_specs=[pl.BlockSpec((tq, d_att), this),
                  pl.BlockSpec((seq, d_kv), lambda bi, i: (bi, 0), pipeline_mode=pl.Buffered(1)),
                  pl.BlockSpec((seq, d_kv), lambda bi, i: (bi, 0), pipeline_mode=pl.Buffered(1)),
                  pl.BlockSpec((n_ctx, d_kv), lambda bi, i: (ctx_blk0 + bi, 0), pipeline_mode=pl.Buffered(1)),
                  pl.BlockSpec((n_ctx, d_kv), lambda bi, i: (ctx_blk0 + bi, 0), pipeline_mode=pl.Buffered(1)),
                  pl.BlockSpec((tq, d_att), done),
                  pl.BlockSpec((1, d_att), lambda bi, i: (0, 0))],
        out_specs=pl.BlockSpec((tq, d_att), done),
        out_shape=jax.ShapeDtypeStruct((n_lat, d_att), BF16),
        scratch_shapes=[pltpu.VMEM((seq + n_ctx, d_kv), BF16),
                        pltpu.VMEM((d_kv // HEAD_DIM, HEAD_DIM + V_EXTRA_ROWS, seq + n_ctx), BF16),
                        pltpu.VMEM((2, seq + n_ctx, Q_PER_KV * tq), F32),
                        pltpu.VMEM((2, tq, d_att), F32)],
        compiler_params=pltpu.CompilerParams(dimension_semantics=("parallel", "arbitrary"),
                                             vmem_limit_bytes=ATTN_VMEM_LIMIT),
        name="attention",
    )(qn, kn, vb, kn, vb, gates, norm_w)


CONV_PAD_ROWS = 8
CONV_PIECE = 32
LRU_SEGS = SUBLANES
GATE_ROWS = 256
SCAN_STEPS = 4


def _softplus(z):
    return jnp.maximum(z, 0.0) + jnp.log1p(jnp.exp(-jnp.abs(z)))


def _segment_len(total):
    seg = -(-total // LRU_SEGS)
    return seg + (SUBLANES // 2 - seg) % SUBLANES


def _scan8(a, bv, reverse):
    row = lax.broadcasted_iota(jnp.int32, a.shape, 0)
    for s in (1, 2, 4):
        if reverse:
            keep = row < SUBLANES - s
            shift = SUBLANES - s
        else:
            keep = row >= s
            shift = s
        a_prev = jnp.where(keep, pltpu.roll(a, shift, 0), 1.0)
        b_prev = jnp.where(keep, pltpu.roll(bv, shift, 0), 0.0)
        bv = bv + a * b_prev
        a = a * a_prev
    return a, bv


def _lru_kernel(xl_ref, xc_ref, cw_ref, cb_ref, w_ref, bias_ref, lam_ref, gl_ref, nw_ref, y_ref, ss_ref,
                xp_ref, u_ref, a_ref, b_ref, hs_ref, *, seq, n_ctx, seg_len):
    cg = y_ref.shape[1]
    n_blk = cg // LRU_BLOCK_DIM
    total = n_ctx + seq
    padded = LRU_SEGS * seg_len
    n_chunks = total // GATE_ROWS
    gate_unroll = next(u for u in (3, 2, 1) if n_chunks % u == 0)

    @pl.when(pl.program_id(1) == 0)
    def _():
        ss_ref[...] = jnp.zeros(ss_ref.shape, F32)

    for n in range(n_blk):
        a_ref[n, total:padded, :] = jnp.ones((padded - total, LRU_BLOCK_DIM), F32)
        b_ref[n, total:padded, :] = jnp.zeros((padded - total, LRU_BLOCK_DIM), F32)
    zeros_pad = jnp.zeros((CONV_PAD_ROWS, cg), F32)
    halo = CONV_PIECE + 2 * CONV_PAD_ROWS
    half_w = [0.5 * cw_ref[j:j + 1, :] for j in range(CONV_WIDTH)]
    half_b = 0.5 * cb_ref[...]

    def conv_into(src_ref, n_rows, u_rows):
        xp_ref[0:CONV_PAD_ROWS, :] = zeros_pad
        xp_ref[CONV_PAD_ROWS:CONV_PAD_ROWS + n_rows, :] = src_ref[...]
        xp_ref[CONV_PAD_ROWS + n_rows:2 * CONV_PAD_ROWS + n_rows, :] = zeros_pad
        for p in range(n_rows // CONV_PIECE):
            window = xp_ref[p * CONV_PIECE:p * CONV_PIECE + halo, :]
            acc = half_b
            for j in range(CONV_WIDTH):
                off = j - CONV_LEFT
                shifted = window if off == 0 else pltpu.roll(window, (-off) % halo, 0)
                acc = acc + half_w[j] * shifted[CONV_PAD_ROWS:CONV_PAD_ROWS + CONV_PIECE, :]
            for u0 in u_rows:
                u_ref[u0 + p * CONV_PIECE:u0 + (p + 1) * CONV_PIECE, :] = acc

    conv_into(xc_ref, n_ctx, (0, n_ctx + seq))
    conv_into(xl_ref, seq, (n_ctx,))

    for d in range(2):
        reverse = d == 1
        seq_row0 = n_ctx if reverse else 0
        half_decay = [0.5 * LRU_C * _softplus(-lam_ref[d:d + 1, n * LRU_BLOCK_DIM:(n + 1) * LRU_BLOCK_DIM])
                      for n in range(n_blk)]
        exp2_scale = [-LOG2_E * hd for hd in half_decay]

        def gate_body(cb, _, d=d, seq_row0=seq_row0, half_decay=half_decay, exp2_scale=exp2_scale):
            for cc in range(gate_unroll):
                s0 = pl.multiple_of((cb * gate_unroll + cc) * GATE_ROWS, GATE_ROWS)
                hu = u_ref[pl.ds(pl.multiple_of(seq_row0 + s0, SUBLANES), GATE_ROWS), :]
                hub = hu.astype(BF16)
                for n in range(n_blk):
                    sl = slice(n * LRU_BLOCK_DIM, (n + 1) * LRU_BLOCK_DIM)
                    pre = jnp.dot(hub[:, sl], w_ref[d, n], preferred_element_type=F32) + bias_ref[d, n]
                    t_r = jnp.tanh(pre[:, :LRU_BLOCK_DIM])
                    t_i = jnp.tanh(pre[:, LRU_BLOCK_DIM:])
                    neg_log_a = t_r * half_decay[n] + half_decay[n]
                    a = jnp.exp2(t_r * exp2_scale[n] + exp2_scale[n])
                    a_ref[n, pl.ds(s0, GATE_ROWS), :] = a
                    w = jnp.tanh(neg_log_a) * (1.0 + a * a)
                    mult = jnp.where(w > 0.0, w * lax.rsqrt(w), 0.0)
                    b_ref[n, pl.ds(s0, GATE_ROWS), :] = mult * ((t_i + 1.0) * hu[:, sl])
            return 0

        lax.fori_loop(0, n_chunks // gate_unroll, gate_body, 0)

        def load_steps(tb, reverse=reverse):
            rows = []
            base = (seg_len // SCAN_STEPS - 1 - tb) * SCAN_STEPS if reverse else tb * SCAN_STEPS
            for k in range(SCAN_STEPS):
                i = base + (SCAN_STEPS - 1 - k if reverse else k)
                idx = pl.ds(i, LRU_SEGS, stride=seg_len)
                rows.append((idx, [a_ref[n, idx, :] for n in range(n_blk)],
                             [b_ref[n, idx, :] for n in range(n_blk)]))
            return rows

        def local_body(tb, carry):
            hs, decs = list(carry[0]), list(carry[1])
            for _, a, bv in load_steps(tb):
                for n in range(n_blk):
                    hs[n] = a[n] * hs[n] + bv[n]
                    decs[n] = decs[n] * a[n]
            return tuple(hs), tuple(decs)

        zero = tuple(jnp.zeros((LRU_SEGS, LRU_BLOCK_DIM), F32) for _ in range(n_blk))
        one = tuple(jnp.ones((LRU_SEGS, LRU_BLOCK_DIM), F32) for _ in range(n_blk))
        h_end, dec_end = lax.fori_loop(0, seg_len // SCAN_STEPS, local_body, (zero, one))

        starts = []
        for n in range(n_blk):
            _, state = _scan8(dec_end[n], h_end[n], reverse)
            row = lax.broadcasted_iota(jnp.int32, state.shape, 0)
            if reverse:
                starts.append(jnp.where(row < LRU_SEGS - 1, pltpu.roll(state, LRU_SEGS - 1, 0), 0.0))
            else:
                starts.append(jnp.where(row >= 1, pltpu.roll(state, 1, 0), 0.0))

        def final_body(tb, hs):
            hs = list(hs)
            for idx, a, bv in load_steps(tb):
                for n in range(n_blk):
                    hs[n] = a[n] * hs[n] + bv[n]
                    hs_ref[d, n, idx, :] = hs[n]
            return tuple(hs)

        lax.fori_loop(0, seg_len // SCAN_STEPS, final_body, tuple(starts))

    for r0 in range(0, seq, GATE_ROWS):
        rows = slice(r0, r0 + GATE_ROWS)
        lru = jnp.concatenate(
            [hs_ref[0, n, n_ctx + r0:n_ctx + r0 + GATE_ROWS, :] + hs_ref[1, n, r0:r0 + GATE_ROWS, :]
             for n in range(n_blk)], axis=1)
        y_ref[rows, :] = (lru * nw_ref[...] * _silu(gl_ref[rows, :])).astype(BF16)
        ss_ref[rows, :] += jnp.broadcast_to(jnp.sum(lru * lru, axis=-1, keepdims=True), (GATE_ROWS, LANES))


def _lru(xl, gates, conv_w, conv_b, w_cat, bias_cat, lam, norm_w, b, seq, n_ctx, d_att, cg=512):
    n_lat = b * seq
    d_lru = xl.shape[1]
    n_blk = cg // LRU_BLOCK_DIM
    n_groups = d_lru // cg
    seg_len = _segment_len(n_ctx + seq)
    assert n_ctx % GATE_ROWS == 0 and seq % GATE_ROWS == 0 and GATE_ROWS % CONV_PIECE == 0
    assert seg_len % SCAN_STEPS == 0
    kern = functools.partial(_lru_kernel, seq=seq, n_ctx=n_ctx, seg_len=seg_len)
    ctx_blk0 = n_lat // n_ctx
    gl_blk0 = d_att // cg
    scan_rows = LRU_SEGS * seg_len
    return pl.pallas_call(
        kern,
        grid=(b, n_groups),
        in_specs=[pl.BlockSpec((seq, cg), lambda bi, g: (bi, g)),
                  pl.BlockSpec((n_ctx, cg), lambda bi, g: (ctx_blk0 + bi, g)),
                  pl.BlockSpec((CONV_WIDTH, cg), lambda bi, g: (0, g)),
                  pl.BlockSpec((1, cg), lambda bi, g: (0, g)),
                  pl.BlockSpec((2, n_blk, LRU_BLOCK_DIM, 2 * LRU_BLOCK_DIM), lambda bi, g: (0, g, 0, 0)),
                  pl.BlockSpec((2, n_blk, 1, 2 * LRU_BLOCK_DIM), lambda bi, g: (0, g, 0, 0)),
                  pl.BlockSpec((2, cg), lambda bi, g: (0, g)),
                  pl.BlockSpec((seq, cg), lambda bi, g: (bi, gl_blk0 + g)),
                  pl.BlockSpec((1, cg), lambda bi, g: (0, g))],
        out_specs=[pl.BlockSpec((seq, cg), lambda bi, g: (bi, g)),
                   pl.BlockSpec((seq, LANES), lambda bi, g: (bi, 0))],
        out_shape=[jax.ShapeDtypeStruct((n_lat, d_lru), BF16),
                   jax.ShapeDtypeStruct((n_lat, LANES), F32)],
        scratch_shapes=[pltpu.VMEM((seq + 2 * CONV_PAD_ROWS, cg), F32),
                        pltpu.VMEM((2 * n_ctx + seq, cg), F32),
                        pltpu.VMEM((n_blk, scan_rows, LRU_BLOCK_DIM), F32),
                        pltpu.VMEM((n_blk, scan_rows, LRU_BLOCK_DIM), F32),
                        pltpu.VMEM((2, n_blk, scan_rows, LRU_BLOCK_DIM), F32)],
        compiler_params=_params("parallel", "arbitrary"),
        name="rglru",
    )(xl, xl, conv_w, conv_b, w_cat, bias_cat, lam, gates, norm_w)


def _out_kernel(ma_ref, yl_ref, ss_ref, wa_ref, wl_ref, x_ref, gate_ref, o_ref, wab_ref, wlb_ref,
                *, tiles_per_batch, d_lru):
    @pl.when(pl.program_id(1) == 0)
    def _():
        wab_ref[...] = wa_ref[...].astype(BF16)
        wlb_ref[...] = wl_ref[...].astype(BF16)

    bi = pl.program_id(1) // tiles_per_batch
    gate = gate_ref[pl.ds(bi, 1), :]
    sumsq = ss_ref[:, 0:LANES]
    for g in range(1, ss_ref.shape[1] // LANES):
        sumsq = sumsq + ss_ref[:, g * LANES:(g + 1) * LANES]
    inv_rms = lax.rsqrt(sumsq * (1.0 / d_lru) + EPS)
    inv_rms = jnp.concatenate([inv_rms] * (o_ref.shape[1] // LANES), axis=1)
    acc = jnp.dot(ma_ref[...], wab_ref[...], preferred_element_type=F32)
    acc = acc + inv_rms * jnp.dot(yl_ref[...], wlb_ref[...], preferred_element_type=F32)
    o_ref[...] = x_ref[...] + gate * acc


def _out_proj(mix_att, y_lru, ss, w_out, x2, mod, seq, gate_col0):
    m, d_att = mix_att.shape
    d_lru = y_lru.shape[1]
    assert d_att == d_lru
    n = w_out.shape[1]
    tm, tn, w_buffers = _proj_tiles(
        n, d_att + d_lru,
        lambda tm, tn: 2 * tm * (d_att + d_lru) * 2 + 4 * tm * tn * 4 + 2 * tm * LANES * 4)
    assert m % tm == 0 and seq % tm == 0 and gate_col0 % tn == 0
    gb = gate_col0 // tn
    kern = functools.partial(_out_kernel, tiles_per_batch=seq // tm, d_lru=d_lru)
    return pl.pallas_call(
        kern,
        grid=(n // tn, m // tm),
        in_specs=[pl.BlockSpec((tm, d_att), lambda j, i: (i, 0)),
                  pl.BlockSpec((tm, d_lru), lambda j, i: (i, 0)),
                  pl.BlockSpec((tm, ss.shape[1]), lambda j, i: (i, 0)),
                  _weight_spec(d_att, tn, w_buffers, lambda j, i: (0, j)),
                  _weight_spec(d_lru, tn, w_buffers, lambda j, i: (1, j)),
                  pl.BlockSpec((tm, tn), lambda j, i: (i, j)),
                  pl.BlockSpec((MOD_ROWS, tn), lambda j, i: (0, gb + j))],
        out_specs=pl.BlockSpec((tm, tn), lambda j, i: (i, j)),
        out_shape=jax.ShapeDtypeStruct((m, n), F32),
        scratch_shapes=[pltpu.VMEM((d_att, tn), BF16), pltpu.VMEM((d_lru, tn), BF16)],
        compiler_params=_params("parallel", "arbitrary"),
        name="out_proj",
    )(mix_att, y_lru, ss, w_out, w_out, x2, mod)


def _rope_tables(seq):
    pos = jnp.arange(seq)
    row = (pos // GRID_W).astype(F32)
    col = (pos % GRID_W).astype(F32)
    n_freq = HEAD_DIM // 4
    freqs = ROPE_THETA ** (-jnp.arange(n_freq, dtype=F32) / n_freq)
    ang_r = row[:, None] * freqs
    ang_c = col[:, None] * freqs
    cos = jnp.concatenate([jnp.cos(ang_r), jnp.cos(ang_r), jnp.cos(ang_c), jnp.cos(ang_c)], axis=1)
    sin = jnp.concatenate([-jnp.sin(ang_r), jnp.sin(ang_r), -jnp.sin(ang_c), jnp.sin(ang_c)], axis=1)
    return cos, sin


def kernel(x, c, ctx, c_ctx, w_ada, b_ada, norm_w, w_in, q_norm_w, k_norm_w, conv_w, conv_b,
           lru_wa, lru_ba, lru_wx, lru_bx, lru_lambda, out_norm_att, out_norm_lru, w_out):
    assert w_ada.shape[0] == 1, "single-layer kernel: only the latent stream is produced"
    b, seq, d = x.shape
    n_ctx = ctx.shape[1]
    d_att = out_norm_att.shape[1]
    d_lru = out_norm_lru.shape[1]
    d_kv = (w_in.shape[2] - 2 * d_att - 2 * d_lru) // 2
    n_lat = b * seq
    n_all = n_lat + b * n_ctx
    assert b + 1 <= MOD_ROWS
    k_col0, v_col0, ga_col0 = d_att, d_att + d_kv, d_att + 2 * d_kv
    xl_col0 = ga_col0 + d_att
    gl_col0 = xl_col0 + d_lru

    x2 = x.reshape(n_lat, d)
    c2 = ctx.reshape(b * n_ctx, d)

    cvec = jnp.concatenate([c, c_ctx[None, :], jnp.zeros((MOD_ROWS - b - 1, d), F32)], axis=0)
    mod = _modulation(cvec, w_ada[0], b_ada)

    h = _prenorm(x2, c2, norm_w, mod, seq)
    w_in0 = w_in[0]

    cos, sin = _rope_tables(seq)
    qn = _proj_heads("proj_q", h, w_in0, n_lat, 0, d_att, cos, sin, q_norm_w, seq, n_lat,
                     HEAD_DIM ** -0.5 * LOG2_E)
    kn = _proj_heads("proj_k", h, w_in0, n_all, k_col0, d_kv, cos, sin, k_norm_w, seq, n_lat, 1.0)
    vb = _proj_plain("proj_v", h, w_in0, n_all, [(v_col0, d_kv)], BF16)
    xl = _proj_plain("proj_xl", h, w_in0, n_all, [(xl_col0, d_lru)], F32)
    gates = _proj_plain("proj_gates", h, w_in0, n_lat, [(ga_col0, d_att), (gl_col0, d_lru)], F32)

    mix_att = _attention(qn, kn, vb, gates, out_norm_att, b, seq, n_ctx)

    n_blocks = d_lru // LRU_BLOCK_DIM
    w_cat = jnp.concatenate([lru_wa[0], lru_wx[0]], axis=-1).astype(BF16)
    bias_cat = 0.5 * jnp.concatenate([lru_ba[0].reshape(2, n_blocks, 1, LRU_BLOCK_DIM),
                                      lru_bx[0].reshape(2, n_blocks, 1, LRU_BLOCK_DIM)], axis=-1)
    y_lru, ss = _lru(xl, gates, conv_w[0], conv_b, w_cat, bias_cat, lru_lambda[0], out_norm_lru,
                     b, seq, n_ctx, d_att)

    out = _out_proj(mix_att, y_lru, ss, w_out[0], x2, mod, seq, 2 * d)
    return out.reshape(b, seq, d)
```

```python
import functools
import math

import jax
import jax.numpy as jnp
from jax import lax
from jax.experimental import pallas as pl
from jax.experimental.pallas import tpu as pltpu

F32 = jnp.float32
BF16 = jnp.bfloat16

HEAD_DIM = 128
GRID_W = 64
Q_PER_KV = 4
LRU_BLOCK_DIM = 128
CONV_WIDTH = 4
CONV_LEFT = 2
LRU_C = 8.0
ROPE_THETA = 10000.0
EPS = 1e-6
LOG2_E = 1.4426950408889634

LANES = 128
SUBLANES = 8
MXU_WIDTH = 256
MOD_ROWS = 8
V7X_VMEM_BYTES = 64 * 1024 * 1024
VMEM_LIMIT = V7X_VMEM_BYTES - 2 * 1024 * 1024
MATMUL_TEMP_BYTES = 4 * 1024 * 1024


def _params(*sem):
    return pltpu.CompilerParams(dimension_semantics=sem, vmem_limit_bytes=VMEM_LIMIT)


def _silu(x):
    return x * jax.nn.sigmoid(x)


def _rms(x, w):
    ms = jnp.mean(x * x, axis=-1, keepdims=True)
    return x * lax.rsqrt(ms + EPS) * w


def _mod_kernel(c_ref, w_ref, b_ref, o_ref):
    s = _silu(c_ref[...]).astype(BF16)
    o_ref[...] = jnp.dot(s, w_ref[...].astype(BF16), preferred_element_type=F32) + b_ref[...]


def _modulation(cvec, w_ada, b_ada, tn=512):
    d, n = w_ada.shape
    return pl.pallas_call(
        _mod_kernel,
        grid=(n // tn,),
        in_specs=[pl.BlockSpec((MOD_ROWS, d), lambda j: (0, 0)),
                  pl.BlockSpec((d, tn), lambda j: (0, j)),
                  pl.BlockSpec((1, tn), lambda j: (0, j))],
        out_specs=pl.BlockSpec((MOD_ROWS, tn), lambda j: (0, j)),
        out_shape=jax.ShapeDtypeStruct((MOD_ROWS, n), F32),
        compiler_params=_params("parallel"),
        name="modulation",
    )(cvec, w_ada, b_ada)


def _prenorm_kernel(x_ref, c_ref, nw_ref, mod_ref, o_ref, *, d, n_x_tiles, tiles_per_batch, ctx_row):
    t = pl.program_id(0)

    def emit(src_ref, row):
        y = _rms(src_ref[...], nw_ref[...])
        shift = mod_ref[pl.ds(row, 1), 0:d]
        scale = mod_ref[pl.ds(row, 1), d:2 * d]
        o_ref[...] = (y * (1.0 + scale) + shift).astype(BF16)

    @pl.when(t < n_x_tiles)
    def _():
        emit(x_ref, t // tiles_per_batch)

    @pl.when(t >= n_x_tiles)
    def _():
        emit(c_ref, ctx_row)


def _prenorm(x2, c2, norm_w, mod, seq, tr=512):
    mx, d = x2.shape
    mc = c2.shape[0]
    nx, nc = mx // tr, mc // tr
    kern = functools.partial(_prenorm_kernel, d=d, n_x_tiles=nx, tiles_per_batch=seq // tr,
                             ctx_row=mx // seq)
    return pl.pallas_call(
        kern,
        grid=(nx + nc,),
        in_specs=[pl.BlockSpec((tr, d), lambda t: (jnp.minimum(t, nx - 1), 0)),
                  pl.BlockSpec((tr, d), lambda t: (jnp.maximum(t - nx, 0), 0)),
                  pl.BlockSpec((1, d), lambda t: (0, 0)),
                  pl.BlockSpec(mod.shape, lambda t: (0, 0))],
        out_specs=pl.BlockSpec((tr, d), lambda t: (t, 0)),
        out_shape=jax.ShapeDtypeStruct((mx + mc, d), BF16),
        compiler_params=_params("parallel"),
        name="prenorm",
    )(x2, c2, norm_w, mod)


WIDE_TN = 1024
NARROW_TN = 512


def _proj_tiles(n_cols, k, resident_bytes):
    tm, tn = (512, WIDE_TN) if n_cols % WIDE_TN == 0 else (1024, NARROW_TN)
    w_tile = k * tn * 4
    fits = resident_bytes(tm, tn) + k * tn * 2 + 2 * w_tile + MATMUL_TEMP_BYTES <= VMEM_LIMIT
    return tm, tn, 2 if fits else 1


def _weight_spec(k, tn, buffers, index_map):
    if buffers == 1:
        return pl.BlockSpec((k, tn), index_map, pipeline_mode=pl.Buffered(1))
    return pl.BlockSpec((k, tn), index_map)


def _cast_weights_once(w_ref, wb_ref):
    @pl.when(pl.program_id(1) == 0)
    def _():
        wb_ref[...] = w_ref[...].astype(BF16)


def _rope(y, cos, sin_signed):
    lane = lax.broadcasted_iota(jnp.int32, y.shape, 1)
    first_half = (lane & (HEAD_DIM // 4)) == 0
    partner = jnp.where(first_half, pltpu.roll(y, HEAD_DIM - HEAD_DIM // 4, 1),
                        pltpu.roll(y, HEAD_DIM // 4, 1))
    return y * cos + partner * sin_signed


def _proj_heads_kernel(h_ref, w_ref, cos_ref, sin_ref, nw_ref, o_ref, wb_ref, acc0_ref, acc1_ref,
                       *, n_row_tiles, rope_tiles, out_scale):
    i = pl.program_id(1)
    _cast_weights_once(w_ref, wb_ref)
    accs = (acc0_ref, acc1_ref)

    def matmul_into(acc_ref):
        acc_ref[...] = jnp.dot(h_ref[...], wb_ref[...], preferred_element_type=F32)

    def finish_from(acc_ref, rope):
        norm_w = nw_ref[...] * out_scale
        for hh in range(o_ref.shape[1] // HEAD_DIM):
            sl = slice(hh * HEAD_DIM, (hh + 1) * HEAD_DIM)
            y = _rms(acc_ref[:, sl], norm_w)
            if rope:
                y = _rope(y, cos_ref[...], sin_ref[...])
            o_ref[:, sl] = y.astype(BF16)

    @pl.when(i == 0)
    def _():
        matmul_into(accs[0])

    for parity in range(2):
        @pl.when((i >= 1) & (i < n_row_tiles) & (i % 2 == parity))
        def _(parity=parity):
            matmul_into(accs[parity])
            finish_from(accs[1 - parity], True)

    @pl.when(i == n_row_tiles)
    def _():
        finish_from(accs[(n_row_tiles - 1) % 2], n_row_tiles - 1 < rope_tiles)


def _proj_heads(name, h, w_in, n_rows, col0, n_cols, cos, sin, norm_w, seq, rope_rows, out_scale):
    k = h.shape[1]
    tm, tn, w_buffers = _proj_tiles(
        n_cols, k, lambda tm, tn: 2 * tm * k * 2 + 2 * tm * tn * 4 + 2 * tm * tn * 2 + 4 * tm * HEAD_DIM * 4)
    assert n_rows % tm == 0 and seq % tm == 0 and col0 % tn == 0
    n_row_tiles = n_rows // tm
    rope_tiles = rope_rows // tm
    assert n_row_tiles - 2 < rope_tiles
    spt = seq // tm
    kern = functools.partial(_proj_heads_kernel, n_row_tiles=n_row_tiles, rope_tiles=rope_tiles,
                             out_scale=out_scale)
    done = lambda i: jnp.maximum(i - 1, 0)
    return pl.pallas_call(
        kern,
        grid=(n_cols // tn, n_row_tiles + 1),
        in_specs=[pl.BlockSpec((tm, k), lambda j, i: (jnp.minimum(i, n_row_tiles - 1), 0)),
                  _weight_spec(k, tn, w_buffers, lambda j, i: (0, col0 // tn + j)),
                  pl.BlockSpec((tm, HEAD_DIM), lambda j, i: (done(i) % spt, 0)),
                  pl.BlockSpec((tm, HEAD_DIM), lambda j, i: (done(i) % spt, 0)),
                  pl.BlockSpec((1, HEAD_DIM), lambda j, i: (0, 0))],
        out_specs=pl.BlockSpec((tm, tn), lambda j, i: (done(i), j)),
        out_shape=jax.ShapeDtypeStruct((n_rows, n_cols), BF16),
        scratch_shapes=[pltpu.VMEM((k, tn), BF16),
                        pltpu.VMEM((tm, tn), F32),
                        pltpu.VMEM((tm, tn), F32)],
        compiler_params=_params("parallel", "arbitrary"),
        name=name,
    )(h, w_in, cos, sin, norm_w)


def _proj_plain_kernel(h_ref, w_ref, o_ref, wb_ref):
    _cast_weights_once(w_ref, wb_ref)
    o_ref[...] = jnp.dot(h_ref[...], wb_ref[...], preferred_element_type=F32).astype(o_ref.dtype)


def _proj_plain(name, h, w_in, n_rows, col_ranges, out_dtype):
    k = h.shape[1]
    out_bytes = jnp.dtype(out_dtype).itemsize
    tm, tn, w_buffers = _proj_tiles(math.gcd(*[c for r in col_ranges for c in r]), k,
                                    lambda tm, tn: 2 * tm * k * 2 + 2 * tm * tn * out_bytes)
    assert n_rows % tm == 0
    tiles = [(start // tn, width // tn) for start, width in col_ranges]

    def col_tile(j):
        first, idx = 0, tiles[0][0] + j
        for r in range(1, len(tiles)):
            first += tiles[r - 1][1]
            idx = jnp.where(j >= first, tiles[r][0] + j - first, idx)
        return idx

    n_col_tiles = sum(n for _, n in tiles)
    return pl.pallas_call(
        _proj_plain_kernel,
        grid=(n_col_tiles, n_rows // tm),
        in_specs=[pl.BlockSpec((tm, k), lambda j, i: (i, 0)),
                  _weight_spec(k, tn, w_buffers, lambda j, i: (0, col_tile(j)))],
        out_specs=pl.BlockSpec((tm, tn), lambda j, i: (i, j)),
        out_shape=jax.ShapeDtypeStruct((n_rows, n_col_tiles * tn), out_dtype),
        scratch_shapes=[pltpu.VMEM((k, tn), BF16)],
        compiler_params=_params("parallel", "arbitrary"),
        name=name,
    )(h, w_in)


V_EXTRA_ROWS = 16


def _attn_kernel(q_ref, kl_ref, vl_ref, kc_ref, vc_ref, ga_ref, nw_ref, o_ref,
                 k_ref, vt_ref, st_ref, att_ref, *, n_q_tiles):
    tq = q_ref.shape[0]
    seq, n_ctx = kl_ref.shape[0], kc_ref.shape[0]
    n_kv = kl_ref.shape[1] // HEAD_DIM
    nt = (((1,), (1,)), ((), ()))
    i = pl.program_id(1)
    n_keys = seq + n_ctx

    @pl.when(i == 0)
    def _():
        k_ref[0:seq, :] = kl_ref[...]
        k_ref[seq:n_keys, :] = kc_ref[...]
        row = lax.broadcasted_iota(jnp.int32, (V_EXTRA_ROWS, n_keys), 0)
        tail = jnp.where(row == 0, 1.0, 0.0).astype(BF16)
        for kv in range(n_kv):
            ks = slice(kv * HEAD_DIM, (kv + 1) * HEAD_DIM)
            vt_ref[kv, 0:HEAD_DIM, 0:seq] = vl_ref[:, ks].T
            vt_ref[kv, 0:HEAD_DIM, seq:n_keys] = vc_ref[:, ks].T
            vt_ref[kv, HEAD_DIM:HEAD_DIM + V_EXTRA_ROWS, :] = tail

    def head_slices(kv):
        return [slice((kv * Q_PER_KV + g) * HEAD_DIM, (kv * Q_PER_KV + g + 1) * HEAD_DIM)
                for g in range(Q_PER_KV)]

    def scores(kv, slot):
        q4 = jnp.concatenate([q_ref[:, hs] for hs in head_slices(kv)], axis=0)
        st_ref[slot] = lax.dot_general(k_ref[:, kv * HEAD_DIM:(kv + 1) * HEAD_DIM], q4, nt,
                                       preferred_element_type=F32)

    def softmax_values(kv, slot, tile_slot):
        m = jnp.max(st_ref[slot], axis=0, keepdims=True)
        pt = jnp.exp2(st_ref[slot] - m).astype(BF16)
        res = jnp.dot(vt_ref[kv], pt, preferred_element_type=F32)
        ot = res[:HEAD_DIM, :] / res[HEAD_DIM:HEAD_DIM + 1, :]
        for g, hs in enumerate(head_slices(kv)):
            att_ref[tile_slot, :, hs] = ot[:, g * tq:(g + 1) * tq].T

    def attend(tile_slot):
        scores(0, 0)
        for kv in range(n_kv):
            if kv + 1 < n_kv:
                scores(kv + 1, (kv + 1) % 2)
            softmax_values(kv, kv % 2, tile_slot)

    def finish(tile_slot):
        att = att_ref[tile_slot]
        inv_rms = lax.rsqrt(jnp.mean(att * att, axis=-1, keepdims=True) + EPS)
        o_ref[...] = (att * inv_rms * (_silu(ga_ref[...]) * nw_ref[...])).astype(BF16)

    @pl.when(i == 0)
    def _():
        attend(0)

    @pl.when((i >= 1) & (i < n_q_tiles))
    def _():
        attend(i % 2)
        finish((i - 1) % 2)

    @pl.when(i == n_q_tiles)
    def _():
        finish((n_q_tiles - 1) % 2)


def _attention(qn, kn, vb, gates, norm_w, b, seq, n_ctx, tq=256):
    n_lat, d_att = qn.shape
    d_kv = kn.shape[1]
    qt = seq // tq
    ctx_blk0 = n_lat // n_ctx
    this = lambda bi, i: (bi * qt + jnp.minimum(i, qt - 1), 0)
    done = lambda bi, i: (bi * qt + jnp.maximum(i - 1, 0), 0)
    return pl.pallas_call(
        functools.partial(_attn_kernel, n_q_tiles=qt),
        grid=(b, qt + 1),
        in_specs=[pl.BlockSpec((tq, d_att), this),
                  pl.BlockSpec((seq, d_kv), lambda bi, i: (bi, 0), pipeline_mode=pl.Buffered(1)),
                  pl.BlockSpec((seq, d_kv), lambda bi, i: (bi, 0), pipeline_mode=pl.Buffered(1)),
                  pl.BlockSpec((n_ctx, d_kv), lambda bi, i: (ctx_blk0 + bi, 0), pipeline_mode=pl.Buffered(1)),
                  pl.BlockSpec((n_ctx, d_kv), lambda bi, i: (ctx_blk0 + bi, 0), pipeline_mode=pl.Buffered(1)),
                  pl.BlockSpec((tq, d_att), done),
                  pl.BlockSpec((1, d_att), lambda bi, i: (0, 0))],
        out_specs=pl.BlockSpec((tq, d_att), done),
        out_shape=jax.ShapeDtypeStruct((n_lat, d_att), BF16),
        scratch_shapes=[pltpu.VMEM((seq + n_ctx, d_kv), BF16),
                        pltpu.VMEM((d_kv // HEAD_DIM, HEAD_DIM + V_EXTRA_ROWS, seq + n_ctx), BF16),
                        pltpu.VMEM((2, seq + n_ctx, Q_PER_KV * tq), F32),
                        pltpu.VMEM((2, tq, d_att), F32)],
        compiler_params=_params("parallel", "arbitrary"),
        name="attention",
    )(qn, kn, vb, kn, vb, gates, norm_w)


CONV_PAD_ROWS = 8
CONV_PIECE = 32
LRU_SEGS = SUBLANES
GATE_ROWS = 256
SCAN_STEPS = 4


def _softplus(z):
    return jnp.maximum(z, 0.0) + jnp.log1p(jnp.exp(-jnp.abs(z)))


def _segment_len(total):
    seg = -(-total // LRU_SEGS)
    return seg + (SUBLANES // 2 - seg) % SUBLANES


def _scan8(a, bv, reverse):
    row = lax.broadcasted_iota(jnp.int32, a.shape, 0)
    for s in (1, 2, 4):
        if reverse:
            keep = row < SUBLANES - s
            shift = SUBLANES - s
        else:
            keep = row >= s
            shift = s
        a_prev = jnp.where(keep, pltpu.roll(a, shift, 0), 1.0)
        b_prev = jnp.where(keep, pltpu.roll(bv, shift, 0), 0.0)
        bv = bv + a * b_prev
        a = a * a_prev
    return a, bv


def _lru_kernel(xl_ref, xc_ref, cw_ref, cb_ref, w_ref, bias_ref, lam_ref, gl_ref, nw_ref, y_ref, ss_ref,
                xp_ref, u_ref, a_ref, b_ref, hs_ref, *, seq, n_ctx, seg_len):
    cg = y_ref.shape[1]
    n_blk = cg // LRU_BLOCK_DIM
    total = n_ctx + seq
    padded = LRU_SEGS * seg_len
    n_chunks = total // GATE_ROWS
    gate_unroll = next(u for u in (3, 2, 1) if n_chunks % u == 0)

    @pl.when(pl.program_id(1) == 0)
    def _():
        ss_ref[...] = jnp.zeros(ss_ref.shape, F32)

    for n in range(n_blk):
        a_ref[n, total:padded, :] = jnp.ones((padded - total, LRU_BLOCK_DIM), F32)
        b_ref[n, total:padded, :] = jnp.zeros((padded - total, LRU_BLOCK_DIM), F32)
    zeros_pad = jnp.zeros((CONV_PAD_ROWS, cg), F32)
    halo = CONV_PIECE + 2 * CONV_PAD_ROWS
    half_w = [0.5 * cw_ref[j:j + 1, :] for j in range(CONV_WIDTH)]
    half_b = 0.5 * cb_ref[...]

    def conv_into(src_ref, n_rows, u_rows):
        xp_ref[0:CONV_PAD_ROWS, :] = zeros_pad
        xp_ref[CONV_PAD_ROWS:CONV_PAD_ROWS + n_rows, :] = src_ref[...]
        xp_ref[CONV_PAD_ROWS + n_rows:2 * CONV_PAD_ROWS + n_rows, :] = zeros_pad
        for p in range(n_rows // CONV_PIECE):
            window = xp_ref[p * CONV_PIECE:p * CONV_PIECE + halo, :]
            acc = half_b
            for j in range(CONV_WIDTH):
                off = j - CONV_LEFT
                shifted = window if off == 0 else pltpu.roll(window, (-off) % halo, 0)
                acc = acc + half_w[j] * shifted[CONV_PAD_ROWS:CONV_PAD_ROWS + CONV_PIECE, :]
            for u0 in u_rows:
                u_ref[u0 + p * CONV_PIECE:u0 + (p + 1) * CONV_PIECE, :] = acc

    conv_into(xc_ref, n_ctx, (0, n_ctx + seq))
    conv_into(xl_ref, seq, (n_ctx,))

    for d in range(2):
        reverse = d == 1
        seq_row0 = n_ctx if reverse else 0
        half_decay = [0.5 * LRU_C * _softplus(-lam_ref[d:d + 1, n * LRU_BLOCK_DIM:(n + 1) * LRU_BLOCK_DIM])
                      for n in range(n_blk)]

        def gate_body(cb, _, d=d, seq_row0=seq_row0, half_decay=half_decay):
            for cc in range(gate_unroll):
                s0 = pl.multiple_of((cb * gate_unroll + cc) * GATE_ROWS, GATE_ROWS)
                hu = u_ref[pl.ds(pl.multiple_of(seq_row0 + s0, SUBLANES), GATE_ROWS), :]
                hub = hu.astype(BF16)
                for n in range(n_blk):
                    sl = slice(n * LRU_BLOCK_DIM, (n + 1) * LRU_BLOCK_DIM)
                    pre = jnp.dot(hub[:, sl], w_ref[d, n], preferred_element_type=F32) + bias_ref[d, n]
                    t_r = jnp.tanh(pre[:, :LRU_BLOCK_DIM])
                    t_i = jnp.tanh(pre[:, LRU_BLOCK_DIM:])
                    neg_log_a = t_r * half_decay[n] + half_decay[n]
                    a = jnp.exp2(neg_log_a * (-LOG2_E))
                    a_ref[n, pl.ds(s0, GATE_ROWS), :] = a
                    w = jnp.tanh(neg_log_a) * (1.0 + a * a)
                    mult = jnp.where(w > 0.0, w * lax.rsqrt(w), 0.0)
                    b_ref[n, pl.ds(s0, GATE_ROWS), :] = mult * ((t_i + 1.0) * hu[:, sl])
            return 0

        lax.fori_loop(0, n_chunks // gate_unroll, gate_body, 0)

        def load_steps(tb, reverse=reverse):
            rows = []
            base = (seg_len // SCAN_STEPS - 1 - tb) * SCAN_STEPS if reverse else tb * SCAN_STEPS
            for k in range(SCAN_STEPS):
                i = base + (SCAN_STEPS - 1 - k if reverse else k)
                idx = pl.ds(i, LRU_SEGS, stride=seg_len)
                rows.append((idx, [a_ref[n, idx, :] for n in range(n_blk)],
                             [b_ref[n, idx, :] for n in range(n_blk)]))
            return rows

        def local_body(tb, carry):
            hs, decs = list(carry[0]), list(carry[1])
            for _, a, bv in load_steps(tb):
                for n in range(n_blk):
                    hs[n] = a[n] * hs[n] + bv[n]
                    decs[n] = decs[n] * a[n]
            return tuple(hs), tuple(decs)

        zero = tuple(jnp.zeros((LRU_SEGS, LRU_BLOCK_DIM), F32) for _ in range(n_blk))
        one = tuple(jnp.ones((LRU_SEGS, LRU_BLOCK_DIM), F32) for _ in range(n_blk))
        h_end, dec_end = lax.fori_loop(0, seg_len // SCAN_STEPS, local_body, (zero, one))

        starts = []
        for n in range(n_blk):
            _, state = _scan8(dec_end[n], h_end[n], reverse)
            row = lax.broadcasted_iota(jnp.int32, state.shape, 0)
            if reverse:
                starts.append(jnp.where(row < LRU_SEGS - 1, pltpu.roll(state, LRU_SEGS - 1, 0), 0.0))
            else:
                starts.append(jnp.where(row >= 1, pltpu.roll(state, 1, 0), 0.0))

        def final_body(tb, hs):
            hs = list(hs)
            for idx, a, bv in load_steps(tb):
                for n in range(n_blk):
                    hs[n] = a[n] * hs[n] + bv[n]
                    hs_ref[d, n, idx, :] = hs[n]
            return tuple(hs)

        lax.fori_loop(0, seg_len // SCAN_STEPS, final_body, tuple(starts))

    for r0 in range(0, seq, GATE_ROWS):
        rows = slice(r0, r0 + GATE_ROWS)
        lru = jnp.concatenate(
            [hs_ref[0, n, n_ctx + r0:n_ctx + r0 + GATE_ROWS, :] + hs_ref[1, n, r0:r0 + GATE_ROWS, :]
             for n in range(n_blk)], axis=1)
        y_ref[rows, :] = (lru * nw_ref[...] * _silu(gl_ref[rows, :])).astype(BF16)
        ss_ref[rows, :] += jnp.broadcast_to(jnp.sum(lru * lru, axis=-1, keepdims=True), (GATE_ROWS, LANES))


def _lru(xl, gates, conv_w, conv_b, w_cat, bias_cat, lam, norm_w, b, seq, n_ctx, d_att, cg=512):
    n_lat = b * seq
    d_lru = xl.shape[1]
    n_blk = cg // LRU_BLOCK_DIM
    n_groups = d_lru // cg
    seg_len = _segment_len(n_ctx + seq)
    assert n_ctx % GATE_ROWS == 0 and seq % GATE_ROWS == 0 and GATE_ROWS % CONV_PIECE == 0
    assert seg_len % SCAN_STEPS == 0
    kern = functools.partial(_lru_kernel, seq=seq, n_ctx=n_ctx, seg_len=seg_len)
    ctx_blk0 = n_lat // n_ctx
    gl_blk0 = d_att // cg
    scan_rows = LRU_SEGS * seg_len
    return pl.pallas_call(
        kern,
        grid=(b, n_groups),
        in_specs=[pl.BlockSpec((seq, cg), lambda bi, g: (bi, g)),
                  pl.BlockSpec((n_ctx, cg), lambda bi, g: (ctx_blk0 + bi, g)),
                  pl.BlockSpec((CONV_WIDTH, cg), lambda bi, g: (0, g)),
                  pl.BlockSpec((1, cg), lambda bi, g: (0, g)),
                  pl.BlockSpec((2, n_blk, LRU_BLOCK_DIM, 2 * LRU_BLOCK_DIM), lambda bi, g: (0, g, 0, 0)),
                  pl.BlockSpec((2, n_blk, 1, 2 * LRU_BLOCK_DIM), lambda bi, g: (0, g, 0, 0)),
                  pl.BlockSpec((2, cg), lambda bi, g: (0, g)),
                  pl.BlockSpec((seq, cg), lambda bi, g: (bi, gl_blk0 + g)),
                  pl.BlockSpec((1, cg), lambda bi, g: (0, g))],
        out_specs=[pl.BlockSpec((seq, cg), lambda bi, g: (bi, g)),
                   pl.BlockSpec((seq, LANES), lambda bi, g: (bi, 0))],
        out_shape=[jax.ShapeDtypeStruct((n_lat, d_lru), BF16),
                   jax.ShapeDtypeStruct((n_lat, LANES), F32)],
        scratch_shapes=[pltpu.VMEM((seq + 2 * CONV_PAD_ROWS, cg), F32),
                        pltpu.VMEM((2 * n_ctx + seq, cg), F32),
                        pltpu.VMEM((n_blk, scan_rows, LRU_BLOCK_DIM), F32),
                        pltpu.VMEM((n_blk, scan_rows, LRU_BLOCK_DIM), F32),
                        pltpu.VMEM((2, n_blk, scan_rows, LRU_BLOCK_DIM), F32)],
        compiler_params=_params("parallel", "arbitrary"),
        name="rglru",
    )(xl, xl, conv_w, conv_b, w_cat, bias_cat, lam, gates, norm_w)


def _out_kernel(ma_ref, yl_ref, ss_ref, wa_ref, wl_ref, x_ref, gate_ref, o_ref, wab_ref, wlb_ref,
                *, tiles_per_batch, d_lru):
    @pl.when(pl.program_id(1) == 0)
    def _():
        wab_ref[...] = wa_ref[...].astype(BF16)
        wlb_ref[...] = wl_ref[...].astype(BF16)

    bi = pl.program_id(1) // tiles_per_batch
    gate = gate_ref[pl.ds(bi, 1), :]
    sumsq = ss_ref[:, 0:LANES]
    for g in range(1, ss_ref.shape[1] // LANES):
        sumsq = sumsq + ss_ref[:, g * LANES:(g + 1) * LANES]
    inv_rms = lax.rsqrt(sumsq * (1.0 / d_lru) + EPS)
    inv_rms = jnp.concatenate([inv_rms] * (o_ref.shape[1] // LANES), axis=1)
    acc = jnp.dot(ma_ref[...], wab_ref[...], preferred_element_type=F32)
    acc = acc + inv_rms * jnp.dot(yl_ref[...], wlb_ref[...], preferred_element_type=F32)
    o_ref[...] = x_ref[...] + gate * acc


def _out_proj(mix_att, y_lru, ss, w_out, x2, mod, seq, gate_col0):
    m, d_att = mix_att.shape
    d_lru = y_lru.shape[1]
    assert d_att == d_lru
    n = w_out.shape[1]
    tm, tn, w_buffers = _proj_tiles(
        n, d_att + d_lru,
        lambda tm, tn: 2 * tm * (d_att + d_lru) * 2 + 4 * tm * tn * 4 + 2 * tm * LANES * 4)
    assert m % tm == 0 and seq % tm == 0 and gate_col0 % tn == 0
    gb = gate_col0 // tn
    kern = functools.partial(_out_kernel, tiles_per_batch=seq // tm, d_lru=d_lru)
    return pl.pallas_call(
        kern,
        grid=(n // tn, m // tm),
        in_specs=[pl.BlockSpec((tm, d_att), lambda j, i: (i, 0)),
                  pl.BlockSpec((tm, d_lru), lambda j, i: (i, 0)),
                  pl.BlockSpec((tm, ss.shape[1]), lambda j, i: (i, 0)),
                  _weight_spec(d_att, tn, w_buffers, lambda j, i: (0, j)),
                  _weight_spec(d_lru, tn, w_buffers, lambda j, i: (1, j)),
                  pl.BlockSpec((tm, tn), lambda j, i: (i, j)),
                  pl.BlockSpec((MOD_ROWS, tn), lambda j, i: (0, gb + j))],
        out_specs=pl.BlockSpec((tm, tn), lambda j, i: (i, j)),
        out_shape=jax.ShapeDtypeStruct((m, n), F32),
        scratch_shapes=[pltpu.VMEM((d_att, tn), BF16), pltpu.VMEM((d_lru, tn), BF16)],
        compiler_params=_params("parallel", "arbitrary"),
        name="out_proj",
    )(mix_att, y_lru, ss, w_out, w_out, x2, mod)


def _rope_tables(seq):
    pos = jnp.arange(seq)
    row = (pos // GRID_W).astype(F32)
    col = (pos % GRID_W).astype(F32)
    n_freq = HEAD_DIM // 4
    freqs = ROPE_THETA ** (-jnp.arange(n_freq, dtype=F32) / n_freq)
    ang_r = row[:, None] * freqs
    ang_c = col[:, None] * freqs
    cos = jnp.concatenate([jnp.cos(ang_r), jnp.cos(ang_r), jnp.cos(ang_c), jnp.cos(ang_c)], axis=1)
    sin = jnp.concatenate([-jnp.sin(ang_r), jnp.sin(ang_r), -jnp.sin(ang_c), jnp.sin(ang_c)], axis=1)
    return cos, sin


def kernel(x, c, ctx, c_ctx, w_ada, b_ada, norm_w, w_in, q_norm_w, k_norm_w, conv_w, conv_b,
           lru_wa, lru_ba, lru_wx, lru_bx, lru_lambda, out_norm_att, out_norm_lru, w_out):
    assert w_ada.shape[0] == 1, "single-layer kernel: only the latent stream is produced"
    b, seq, d = x.shape
    n_ctx = ctx.shape[1]
    d_att = out_norm_att.shape[1]
    d_lru = out_norm_lru.shape[1]
    d_kv = (w_in.shape[2] - 2 * d_att - 2 * d_lru) // 2
    n_lat = b * seq
    n_all = n_lat + b * n_ctx
    assert b + 1 <= MOD_ROWS
    k_col0, v_col0, ga_col0 = d_att, d_att + d_kv, d_att + 2 * d_kv
    xl_col0 = ga_col0 + d_att
    gl_col0 = xl_col0 + d_lru

    x2 = x.reshape(n_lat, d)
    c2 = ctx.reshape(b * n_ctx, d)

    cvec = jnp.concatenate([c, c_ctx[None, :], jnp.zeros((MOD_ROWS - b - 1, d), F32)], axis=0)
    mod = _modulation(cvec, w_ada[0], b_ada)

    h = _prenorm(x2, c2, norm_w, mod, seq)
    w_in0 = w_in[0]

    cos, sin = _rope_tables(seq)
    qn = _proj_heads("proj_q", h, w_in0, n_lat, 0, d_att, cos, sin, q_norm_w, seq, n_lat,
                     HEAD_DIM ** -0.5 * LOG2_E)
    kn = _proj_heads("proj_k", h, w_in0, n_all, k_col0, d_kv, cos, sin, k_norm_w, seq, n_lat, 1.0)
    vb = _proj_plain("proj_v", h, w_in0, n_all, [(v_col0, d_kv)], BF16)
    xl = _proj_plain("proj_xl", h, w_in0, n_all, [(xl_col0, d_lru)], F32)
    gates = _proj_plain("proj_gates", h, w_in0, n_lat, [(ga_col0, d_att), (gl_col0, d_lru)], F32)

    mix_att = _attention(qn, kn, vb, gates, out_norm_att, b, seq, n_ctx)

    n_blocks = d_lru // LRU_BLOCK_DIM
    w_cat = jnp.concatenate([lru_wa[0], lru_wx[0]], axis=-1).astype(BF16)
    bias_cat = 0.5 * jnp.concatenate([lru_ba[0].reshape(2, n_blocks, 1, LRU_BLOCK_DIM),
                                      lru_bx[0].reshape(2, n_blocks, 1, LRU_BLOCK_DIM)], axis=-1)
    y_lru, ss = _lru(xl, gates, conv_w[0], conv_b, w_cat, bias_cat, lru_lambda[0], out_norm_lru,
                     b, seq, n_ctx, d_att)

    out = _out_proj(mix_att, y_lru, ss, w_out[0], x2, mod, seq, 2 * d)
    return out.reshape(b, seq, d)
```

```python
import functools
import math

import jax
import jax.numpy as jnp
from jax import lax
from jax.experimental import pallas as pl
from jax.experimental.pallas import tpu as pltpu

F32 = jnp.float32
BF16 = jnp.bfloat16

HEAD_DIM = 128
GRID_W = 64
Q_PER_KV = 4
LRU_BLOCK_DIM = 128
CONV_WIDTH = 4
CONV_LEFT = 2
LRU_C = 8.0
ROPE_THETA = 10000.0
EPS = 1e-6
LOG2_E = 1.4426950408889634

LANES = 128
SUBLANES = 8
MXU_WIDTH = 256
MOD_ROWS = 8
V7X_VMEM_BYTES = 64 * 1024 * 1024
VMEM_LIMIT = V7X_VMEM_BYTES - 2 * 1024 * 1024
MATMUL_TEMP_BYTES = 4 * 1024 * 1024


def _params(*sem):
    return pltpu.CompilerParams(dimension_semantics=sem, vmem_limit_bytes=VMEM_LIMIT)


def _silu(x):
    half = 0.5 * x
    return half * jnp.tanh(half) + half


def _rms(x, w):
    ms = jnp.mean(x * x, axis=-1, keepdims=True)
    return x * lax.rsqrt(ms + EPS) * w


def _mod_kernel(c_ref, w_ref, b_ref, o_ref):
    s = _silu(c_ref[...]).astype(BF16)
    o_ref[...] = jnp.dot(s, w_ref[...].astype(BF16), preferred_element_type=F32) + b_ref[...]


def _modulation(cvec, w_ada, b_ada, tn=512):
    d, n = w_ada.shape
    return pl.pallas_call(
        _mod_kernel,
        grid=(n // tn,),
        in_specs=[pl.BlockSpec((MOD_ROWS, d), lambda j: (0, 0)),
                  pl.BlockSpec((d, tn), lambda j: (0, j)),
                  pl.BlockSpec((1, tn), lambda j: (0, j))],
        out_specs=pl.BlockSpec((MOD_ROWS, tn), lambda j: (0, j)),
        out_shape=jax.ShapeDtypeStruct((MOD_ROWS, n), F32),
        compiler_params=_params("parallel"),
        name="modulation",
    )(cvec, w_ada, b_ada)


def _prenorm_kernel(x_ref, c_ref, nw_ref, mod_ref, o_ref, *, d, n_x_tiles, tiles_per_batch, ctx_row):
    t = pl.program_id(0)

    def emit(src_ref, row):
        x = src_ref[...]
        shift = mod_ref[pl.ds(row, 1), 0:d]
        gain = nw_ref[...] * (1.0 + mod_ref[pl.ds(row, 1), d:2 * d])
        inv_rms = lax.rsqrt(jnp.mean(x * x, axis=-1, keepdims=True) + EPS)
        o_ref[...] = (x * inv_rms * gain + shift).astype(BF16)

    @pl.when(t < n_x_tiles)
    def _():
        emit(x_ref, t // tiles_per_batch)

    @pl.when(t >= n_x_tiles)
    def _():
        emit(c_ref, ctx_row)


def _prenorm(x2, c2, norm_w, mod, seq, tr=512):
    mx, d = x2.shape
    mc = c2.shape[0]
    nx, nc = mx // tr, mc // tr
    kern = functools.partial(_prenorm_kernel, d=d, n_x_tiles=nx, tiles_per_batch=seq // tr,
                             ctx_row=mx // seq)
    return pl.pallas_call(
        kern,
        grid=(nx + nc,),
        in_specs=[pl.BlockSpec((tr, d), lambda t: (jnp.minimum(t, nx - 1), 0)),
                  pl.BlockSpec((tr, d), lambda t: (jnp.maximum(t - nx, 0), 0)),
                  pl.BlockSpec((1, d), lambda t: (0, 0)),
                  pl.BlockSpec(mod.shape, lambda t: (0, 0))],
        out_specs=pl.BlockSpec((tr, d), lambda t: (t, 0)),
        out_shape=jax.ShapeDtypeStruct((mx + mc, d), BF16),
        compiler_params=_params("parallel"),
        name="prenorm",
    )(x2, c2, norm_w, mod)


WIDE_TN = 1024
NARROW_TN = 512


def _proj_tiles(n_cols, k, resident_bytes):
    tm, tn = (512, WIDE_TN) if n_cols % WIDE_TN == 0 else (1024, NARROW_TN)
    w_tile = k * tn * 4
    fits = resident_bytes(tm, tn) + k * tn * 2 + 2 * w_tile + MATMUL_TEMP_BYTES <= VMEM_LIMIT
    return tm, tn, 2 if fits else 1


def _weight_spec(k, tn, buffers, index_map):
    if buffers == 1:
        return pl.BlockSpec((k, tn), index_map, pipeline_mode=pl.Buffered(1))
    return pl.BlockSpec((k, tn), index_map)


def _cast_weights_once(w_ref, wb_ref):
    @pl.when(pl.program_id(1) == 0)
    def _():
        wb_ref[...] = w_ref[...].astype(BF16)


def _rope(y, cos, sin_signed):
    lane = lax.broadcasted_iota(jnp.int32, y.shape, 1)
    first_half = (lane & (HEAD_DIM // 4)) == 0
    partner = jnp.where(first_half, pltpu.roll(y, HEAD_DIM - HEAD_DIM // 4, 1),
                        pltpu.roll(y, HEAD_DIM // 4, 1))
    return y * cos + partner * sin_signed


def _proj_heads_kernel(h_ref, w_ref, cos_ref, sin_ref, nw_ref, o_ref, wb_ref, acc0_ref, acc1_ref,
                       *, n_row_tiles, head_cols, out_scale):
    i = pl.program_id(1)
    _cast_weights_once(w_ref, wb_ref)
    accs = (acc0_ref, acc1_ref)

    def matmul_into(acc_ref):
        acc_ref[...] = jnp.dot(h_ref[...], wb_ref[...], preferred_element_type=F32)

    def finish_from(acc_ref):
        norm_w = nw_ref[...] * out_scale
        for hh in range(head_cols // HEAD_DIM):
            sl = slice(hh * HEAD_DIM, (hh + 1) * HEAD_DIM)
            o_ref[:, sl] = _rope(_rms(acc_ref[:, sl], norm_w), cos_ref[...], sin_ref[...]).astype(BF16)
        if head_cols < o_ref.shape[1]:
            o_ref[:, head_cols:] = acc_ref[:, head_cols:].astype(BF16)

    @pl.when(i == 0)
    def _():
        matmul_into(accs[0])

    for parity in range(2):
        @pl.when((i >= 1) & (i < n_row_tiles) & (i % 2 == parity))
        def _(parity=parity):
            matmul_into(accs[parity])
            finish_from(accs[1 - parity])

    @pl.when(i == n_row_tiles)
    def _():
        finish_from(accs[(n_row_tiles - 1) % 2])


def _proj_heads(name, h, w_in, n_rows, col0, n_cols, plain_cols, cos, sin, norm_w, seq, rope_rows, out_scale):
    k = h.shape[1]
    tm, tn, w_buffers = _proj_tiles(
        n_cols, k, lambda tm, tn: 2 * tm * k * 2 + 2 * tm * tn * 4 + 2 * tm * tn * 2 + 4 * tm * HEAD_DIM * 4)
    assert n_rows % tm == 0 and seq % tm == 0 and rope_rows % tm == 0 and col0 % tn == 0
    assert plain_cols == 0 or n_cols == tn
    head_cols = tn - plain_cols
    n_row_tiles = n_rows // tm
    rope_tiles = rope_rows // tm
    spt = seq // tm
    cos = jnp.concatenate([cos, jnp.ones((tm, HEAD_DIM), F32)], axis=0)
    sin = jnp.concatenate([sin, jnp.zeros((tm, HEAD_DIM), F32)], axis=0)
    kern = functools.partial(_proj_heads_kernel, n_row_tiles=n_row_tiles, head_cols=head_cols,
                             out_scale=out_scale)
    done = lambda i: jnp.maximum(i - 1, 0)
    table = lambda j, i: (jnp.where(done(i) < rope_tiles, done(i) % spt, spt), 0)
    return pl.pallas_call(
        kern,
        grid=(n_cols // tn, n_row_tiles + 1),
        in_specs=[pl.BlockSpec((tm, k), lambda j, i: (jnp.minimum(i, n_row_tiles - 1), 0)),
                  _weight_spec(k, tn, w_buffers, lambda j, i: (0, col0 // tn + j)),
                  pl.BlockSpec((tm, HEAD_DIM), table),
                  pl.BlockSpec((tm, HEAD_DIM), table),
                  pl.BlockSpec((1, HEAD_DIM), lambda j, i: (0, 0))],
        out_specs=pl.BlockSpec((tm, tn), lambda j, i: (done(i), j)),
        out_shape=jax.ShapeDtypeStruct((n_rows, n_cols), BF16),
        scratch_shapes=[pltpu.VMEM((k, tn), BF16),
                        pltpu.VMEM((tm, tn), F32),
                        pltpu.VMEM((tm, tn), F32)],
        compiler_params=_params("parallel", "arbitrary"),
        name=name,
    )(h, w_in, cos, sin, norm_w)


def _proj_plain_kernel(h_ref, w_ref, o_ref, wb_ref):
    _cast_weights_once(w_ref, wb_ref)
    o_ref[...] = jnp.dot(h_ref[...], wb_ref[...], preferred_element_type=F32).astype(o_ref.dtype)


def _proj_plain(name, h, w_in, n_rows, col_ranges, out_dtype):
    k = h.shape[1]
    out_bytes = jnp.dtype(out_dtype).itemsize
    tm, tn, w_buffers = _proj_tiles(math.gcd(*[c for r in col_ranges for c in r]), k,
                                    lambda tm, tn: 2 * tm * k * 2 + 2 * tm * tn * out_bytes)
    assert n_rows % tm == 0
    tiles = [(start // tn, width // tn) for start, width in col_ranges]

    def col_tile(j):
        first, idx = 0, tiles[0][0] + j
        for r in range(1, len(tiles)):
            first += tiles[r - 1][1]
            idx = jnp.where(j >= first, tiles[r][0] + j - first, idx)
        return idx

    n_col_tiles = sum(n for _, n in tiles)
    return pl.pallas_call(
        _proj_plain_kernel,
        grid=(n_col_tiles, n_rows // tm),
        in_specs=[pl.BlockSpec((tm, k), lambda j, i: (i, 0)),
                  _weight_spec(k, tn, w_buffers, lambda j, i: (0, col_tile(j)))],
        out_specs=pl.BlockSpec((tm, tn), lambda j, i: (i, j)),
        out_shape=jax.ShapeDtypeStruct((n_rows, n_col_tiles * tn), out_dtype),
        scratch_shapes=[pltpu.VMEM((k, tn), BF16)],
        compiler_params=_params("parallel", "arbitrary"),
        name=name,
    )(h, w_in)


V_EXTRA_ROWS = 16


def _attn_kernel(q_ref, kl_ref, vl_ref, kc_ref, vc_ref, ga_ref, nw_ref, o_ref,
                 k_ref, vt_ref, st_ref, att_ref, *, n_q_tiles):
    tq = q_ref.shape[0]
    seq, n_ctx = kl_ref.shape[0], kc_ref.shape[0]
    n_kv = kl_ref.shape[1] // HEAD_DIM
    nt = (((1,), (1,)), ((), ()))
    i = pl.program_id(1)
    n_keys = seq + n_ctx

    @pl.when(i == 0)
    def _():
        k_ref[0:seq, :] = kl_ref[...]
        k_ref[seq:n_keys, :] = kc_ref[...]
        row = lax.broadcasted_iota(jnp.int32, (V_EXTRA_ROWS, n_keys), 0)
        tail = jnp.where(row == 0, 1.0, 0.0).astype(BF16)
        for kv in range(n_kv):
            ks = slice(kv * HEAD_DIM, (kv + 1) * HEAD_DIM)
            vt_ref[kv, 0:HEAD_DIM, 0:seq] = vl_ref[:, ks].T
            vt_ref[kv, 0:HEAD_DIM, seq:n_keys] = vc_ref[:, ks].T
            vt_ref[kv, HEAD_DIM:HEAD_DIM + V_EXTRA_ROWS, :] = tail

    def head_slices(kv):
        return [slice((kv * Q_PER_KV + g) * HEAD_DIM, (kv * Q_PER_KV + g + 1) * HEAD_DIM)
                for g in range(Q_PER_KV)]

    def scores(kv, slot):
        q4 = jnp.concatenate([q_ref[:, hs] for hs in head_slices(kv)], axis=0)
        st_ref[slot] = lax.dot_general(k_ref[:, kv * HEAD_DIM:(kv + 1) * HEAD_DIM], q4, nt,
                                       preferred_element_type=F32)

    def softmax_values(kv, slot, tile_slot):
        m = jnp.max(st_ref[slot], axis=0, keepdims=True)
        pt = jnp.exp2(st_ref[slot] - m).astype(BF16)
        res = jnp.dot(vt_ref[kv], pt, preferred_element_type=F32)
        ot = res[:HEAD_DIM, :] / res[HEAD_DIM:HEAD_DIM + 1, :]
        for g, hs in enumerate(head_slices(kv)):
            att_ref[tile_slot, :, hs] = ot[:, g * tq:(g + 1) * tq].T

    def attend(tile_slot):
        scores(0, 0)
        for kv in range(n_kv):
            if kv + 1 < n_kv:
                scores(kv + 1, (kv + 1) % 2)
            softmax_values(kv, kv % 2, tile_slot)

    def finish(tile_slot):
        att = att_ref[tile_slot]
        inv_rms = lax.rsqrt(jnp.mean(att * att, axis=-1, keepdims=True) + EPS)
        o_ref[...] = (att * inv_rms * (_silu(ga_ref[...]) * nw_ref[...])).astype(BF16)

    @pl.when(i == 0)
    def _():
        attend(0)

    @pl.when((i >= 1) & (i < n_q_tiles))
    def _():
        attend(i % 2)
        finish((i - 1) % 2)

    @pl.when(i == n_q_tiles)
    def _():
        finish((n_q_tiles - 1) % 2)


def _attention(qn, kv, gates, norm_w, b, seq, n_ctx, tq=256):
    n_lat, d_att = qn.shape
    d_kv = kv.shape[1] // 2
    qt = seq // tq
    ctx_blk0 = n_lat // n_ctx
    this = lambda bi, i: (bi * qt + jnp.minimum(i, qt - 1), 0)
    done = lambda bi, i: (bi * qt + jnp.maximum(i - 1, 0), 0)
    return pl.pallas_call(
        functools.partial(_attn_kernel, n_q_tiles=qt),
        grid=(b, qt + 1),
        in_specs=[pl.BlockSpec((tq, d_att), this),
                  pl.BlockSpec((seq, d_kv), lambda bi, i: (bi, 0), pipeline_mode=pl.Buffered(1)),
                  pl.BlockSpec((seq, d_kv), lambda bi, i: (bi, 1), pipeline_mode=pl.Buffered(1)),
                  pl.BlockSpec((n_ctx, d_kv), lambda bi, i: (ctx_blk0 + bi, 0), pipeline_mode=pl.Buffered(1)),
                  pl.BlockSpec((n_ctx, d_kv), lambda bi, i: (ctx_blk0 + bi, 1), pipeline_mode=pl.Buffered(1)),
                  pl.BlockSpec((tq, d_att), done),
                  pl.BlockSpec((1, d_att), lambda bi, i: (0, 0))],
        out_specs=pl.BlockSpec((tq, d_att), done),
        out_shape=jax.ShapeDtypeStruct((n_lat, d_att), BF16),
        scratch_shapes=[pltpu.VMEM((seq + n_ctx, d_kv), BF16),
                        pltpu.VMEM((d_kv // HEAD_DIM, HEAD_DIM + V_EXTRA_ROWS, seq + n_ctx), BF16),
                        pltpu.VMEM((2, seq + n_ctx, Q_PER_KV * tq), F32),
                        pltpu.VMEM((2, tq, d_att), F32)],
        compiler_params=_params("parallel", "arbitrary"),
        name="attention",
    )(qn, kv, kv, kv, kv, gates, norm_w)


CONV_PAD_ROWS = 8
CONV_PIECE = 32
LRU_SEGS = SUBLANES
GATE_ROWS = 256
SCAN_STEPS = 4


def _softplus(z):
    return jnp.maximum(z, 0.0) + jnp.log1p(jnp.exp(-jnp.abs(z)))


def _segment_len(total):
    seg = -(-total // LRU_SEGS)
    return seg + (SUBLANES // 2 - seg) % SUBLANES


def _scan8(a, bv, reverse):
    row = lax.broadcasted_iota(jnp.int32, a.shape, 0)
    for s in (1, 2, 4):
        if reverse:
            keep = row < SUBLANES - s
            shift = SUBLANES - s
        else:
            keep = row >= s
            shift = s
        a_prev = jnp.where(keep, pltpu.roll(a, shift, 0), 1.0)
        b_prev = jnp.where(keep, pltpu.roll(bv, shift, 0), 0.0)
        bv = bv + a * b_prev
        a = a * a_prev
    return a, bv


def _lru_kernel(xl_ref, xc_ref, cw_ref, cb_ref, w_ref, bias_ref, lam_ref, gl_ref, nw_ref, y_ref, ss_ref,
                xp_ref, u_ref, a_ref, b_ref, hs_ref, *, seq, n_ctx, seg_len):
    cg = y_ref.shape[1]
    n_blk = cg // LRU_BLOCK_DIM
    total = n_ctx + seq
    padded = LRU_SEGS * seg_len
    n_chunks = total // GATE_ROWS
    gate_unroll = next(u for u in (3, 2, 1) if n_chunks % u == 0)

    @pl.when(pl.program_id(1) == 0)
    def _():
        ss_ref[...] = jnp.zeros(ss_ref.shape, F32)

    for n in range(n_blk):
        a_ref[n, total:padded, :] = jnp.ones((padded - total, LRU_BLOCK_DIM), F32)
        b_ref[n, total:padded, :] = jnp.zeros((padded - total, LRU_BLOCK_DIM), F32)
    zeros_pad = jnp.zeros((CONV_PAD_ROWS, cg), F32)
    halo = CONV_PIECE + 2 * CONV_PAD_ROWS
    half_w = [0.5 * cw_ref[j:j + 1, :] for j in range(CONV_WIDTH)]
    half_b = 0.5 * cb_ref[...]

    def conv_into(src_ref, n_rows, u_rows):
        xp_ref[0:CONV_PAD_ROWS, :] = zeros_pad
        xp_ref[CONV_PAD_ROWS:CONV_PAD_ROWS + n_rows, :] = src_ref[...]
        xp_ref[CONV_PAD_ROWS + n_rows:2 * CONV_PAD_ROWS + n_rows, :] = zeros_pad
        for p in range(n_rows // CONV_PIECE):
            window = xp_ref[p * CONV_PIECE:p * CONV_PIECE + halo, :]
            acc = half_b
            for j in range(CONV_WIDTH):
                off = j - CONV_LEFT
                shifted = window if off == 0 else pltpu.roll(window, (-off) % halo, 0)
                acc = acc + half_w[j] * shifted[CONV_PAD_ROWS:CONV_PAD_ROWS + CONV_PIECE, :]
            for u0 in u_rows:
                u_ref[u0 + p * CONV_PIECE:u0 + (p + 1) * CONV_PIECE, :] = acc

    conv_into(xc_ref, n_ctx, (0, n_ctx + seq))
    conv_into(xl_ref, seq, (n_ctx,))

    for d in range(2):
        reverse = d == 1
        seq_row0 = n_ctx if reverse else 0
        half_decay = [0.5 * LRU_C * _softplus(-lam_ref[d:d + 1, n * LRU_BLOCK_DIM:(n + 1) * LRU_BLOCK_DIM])
                      for n in range(n_blk)]

        def gate_body(cb, _, d=d, seq_row0=seq_row0, half_decay=half_decay):
            for cc in range(gate_unroll):
                s0 = pl.multiple_of((cb * gate_unroll + cc) * GATE_ROWS, GATE_ROWS)
                hu = u_ref[pl.ds(pl.multiple_of(seq_row0 + s0, SUBLANES), GATE_ROWS), :]
                hub = hu.astype(BF16)
                for n in range(n_blk):
                    sl = slice(n * LRU_BLOCK_DIM, (n + 1) * LRU_BLOCK_DIM)
                    pre = jnp.dot(hub[:, sl], w_ref[d, n], preferred_element_type=F32) + bias_ref[d, n]
                    t_r = jnp.tanh(pre[:, :LRU_BLOCK_DIM])
                    t_i = jnp.tanh(pre[:, LRU_BLOCK_DIM:])
                    neg_log_a = t_r * half_decay[n] + half_decay[n]
                    a = jnp.exp2(neg_log_a * (-LOG2_E))
                    a_ref[n, pl.ds(s0, GATE_ROWS), :] = a
                    w = jnp.tanh(neg_log_a) * (1.0 + a * a)
                    mult = jnp.where(w > 0.0, w * lax.rsqrt(w), 0.0)
                    b_ref[n, pl.ds(s0, GATE_ROWS), :] = mult * ((t_i + 1.0) * hu[:, sl])
            return 0

        lax.fori_loop(0, n_chunks // gate_unroll, gate_body, 0)

        def load_steps(tb, reverse=reverse):
            rows = []
            base = (seg_len // SCAN_STEPS - 1 - tb) * SCAN_STEPS if reverse else tb * SCAN_STEPS
            for k in range(SCAN_STEPS):
                i = base + (SCAN_STEPS - 1 - k if reverse else k)
                idx = pl.ds(i, LRU_SEGS, stride=seg_len)
                rows.append((idx, [a_ref[n, idx, :] for n in range(n_blk)],
                             [b_ref[n, idx, :] for n in range(n_blk)]))
            return rows

        def local_body(tb, carry):
            hs, decs = list(carry[0]), list(carry[1])
            for _, a, bv in load_steps(tb):
                for n in range(n_blk):
                    hs[n] = a[n] * hs[n] + bv[n]
                    decs[n] = decs[n] * a[n]
            return tuple(hs), tuple(decs)

        zero = tuple(jnp.zeros((LRU_SEGS, LRU_BLOCK_DIM), F32) for _ in range(n_blk))
        one = tuple(jnp.ones((LRU_SEGS, LRU_BLOCK_DIM), F32) for _ in range(n_blk))
        h_end, dec_end = lax.fori_loop(0, seg_len // SCAN_STEPS, local_body, (zero, one))

        starts = []
        for n in range(n_blk):
            _, state = _scan8(dec_end[n], h_end[n], reverse)
            row = lax.broadcasted_iota(jnp.int32, state.shape, 0)
            if reverse:
                starts.append(jnp.where(row < LRU_SEGS - 1, pltpu.roll(state, LRU_SEGS - 1, 0), 0.0))
            else:
                starts.append(jnp.where(row >= 1, pltpu.roll(state, 1, 0), 0.0))

        def final_body(tb, hs):
            hs = list(hs)
            for idx, a, bv in load_steps(tb):
                for n in range(n_blk):
                    hs[n] = a[n] * hs[n] + bv[n]
                    hs_ref[d, n, idx, :] = hs[n]
            return tuple(hs)

        lax.fori_loop(0, seg_len // SCAN_STEPS, final_body, tuple(starts))

    for r0 in range(0, seq, GATE_ROWS):
        rows = slice(r0, r0 + GATE_ROWS)
        lru = jnp.concatenate(
            [hs_ref[0, n, n_ctx + r0:n_ctx + r0 + GATE_ROWS, :] + hs_ref[1, n, r0:r0 + GATE_ROWS, :]
             for n in range(n_blk)], axis=1)
        y_ref[rows, :] = (lru * nw_ref[...] * _silu(gl_ref[rows, :])).astype(BF16)
        ss_ref[rows, :] += jnp.broadcast_to(jnp.sum(lru * lru, axis=-1, keepdims=True), (GATE_ROWS, LANES))


def _lru(xl, gates, conv_w, conv_b, w_cat, bias_cat, lam, norm_w, b, seq, n_ctx, d_att, cg=512):
    n_lat = b * seq
    d_lru = xl.shape[1]
    n_blk = cg // LRU_BLOCK_DIM
    n_groups = d_lru // cg
    seg_len = _segment_len(n_ctx + seq)
    assert n_ctx % GATE_ROWS == 0 and seq % GATE_ROWS == 0 and GATE_ROWS % CONV_PIECE == 0
    assert seg_len % SCAN_STEPS == 0
    kern = functools.partial(_lru_kernel, seq=seq, n_ctx=n_ctx, seg_len=seg_len)
    ctx_blk0 = n_lat // n_ctx
    gl_blk0 = d_att // cg
    scan_rows = LRU_SEGS * seg_len
    return pl.pallas_call(
        kern,
        grid=(b, n_groups),
        in_specs=[pl.BlockSpec((seq, cg), lambda bi, g: (bi, g)),
                  pl.BlockSpec((n_ctx, cg), lambda bi, g: (ctx_blk0 + bi, g)),
                  pl.BlockSpec((CONV_WIDTH, cg), lambda bi, g: (0, g)),
                  pl.BlockSpec((1, cg), lambda bi, g: (0, g)),
                  pl.BlockSpec((2, n_blk, LRU_BLOCK_DIM, 2 * LRU_BLOCK_DIM), lambda bi, g: (0, g, 0, 0)),
                  pl.BlockSpec((2, n_blk, 1, 2 * LRU_BLOCK_DIM), lambda bi, g: (0, g, 0, 0)),
                  pl.BlockSpec((2, cg), lambda bi, g: (0, g)),
                  pl.BlockSpec((seq, cg), lambda bi, g: (bi, gl_blk0 + g)),
                  pl.BlockSpec((1, cg), lambda bi, g: (0, g))],
        out_specs=[pl.BlockSpec((seq, cg), lambda bi, g: (bi, g)),
                   pl.BlockSpec((seq, LANES), lambda bi, g: (bi, 0))],
        out_shape=[jax.ShapeDtypeStruct((n_lat, d_lru), BF16),
                   jax.ShapeDtypeStruct((n_lat, LANES), F32)],
        scratch_shapes=[pltpu.VMEM((seq + 2 * CONV_PAD_ROWS, cg), F32),
                        pltpu.VMEM((2 * n_ctx + seq, cg), F32),
                        pltpu.VMEM((n_blk, scan_rows, LRU_BLOCK_DIM), F32),
                        pltpu.VMEM((n_blk, scan_rows, LRU_BLOCK_DIM), F32),
                        pltpu.VMEM((2, n_blk, scan_rows, LRU_BLOCK_DIM), F32)],
        compiler_params=_params("parallel", "arbitrary"),
        name="rglru",
    )(xl, xl, conv_w, conv_b, w_cat, bias_cat, lam, gates, norm_w)


def _out_kernel(ma_ref, yl_ref, ss_ref, wa_ref, wl_ref, x_ref, gate_ref, o_ref, wab_ref, wlb_ref,
                *, tiles_per_batch, d_lru):
    @pl.when(pl.program_id(1) == 0)
    def _():
        wab_ref[...] = wa_ref[...].astype(BF16)
        wlb_ref[...] = wl_ref[...].astype(BF16)

    bi = pl.program_id(1) // tiles_per_batch
    gate = gate_ref[pl.ds(bi, 1), :]
    sumsq = ss_ref[:, 0:LANES]
    for g in range(1, ss_ref.shape[1] // LANES):
        sumsq = sumsq + ss_ref[:, g * LANES:(g + 1) * LANES]
    inv_rms = lax.rsqrt(sumsq * (1.0 / d_lru) + EPS)
    inv_rms = jnp.concatenate([inv_rms] * (o_ref.shape[1] // LANES), axis=1)
    acc = jnp.dot(ma_ref[...], wab_ref[...], preferred_element_type=F32)
    acc = acc + inv_rms * jnp.dot(yl_ref[...], wlb_ref[...], preferred_element_type=F32)
    o_ref[...] = x_ref[...] + gate * acc


def _out_proj(mix_att, y_lru, ss, w_out, x2, mod, seq, gate_col0):
    m, d_att = mix_att.shape
    d_lru = y_lru.shape[1]
    assert d_att == d_lru
    n = w_out.shape[1]
    tm, tn, w_buffers = _proj_tiles(
        n, d_att + d_lru,
        lambda tm, tn: 2 * tm * (d_att + d_lru) * 2 + 4 * tm * tn * 4 + 2 * tm * LANES * 4)
    assert m % tm == 0 and seq % tm == 0 and gate_col0 % tn == 0
    gb = gate_col0 // tn
    kern = functools.partial(_out_kernel, tiles_per_batch=seq // tm, d_lru=d_lru)
    return pl.pallas_call(
        kern,
        grid=(n // tn, m // tm),
        in_specs=[pl.BlockSpec((tm, d_att), lambda j, i: (i, 0)),
                  pl.BlockSpec((tm, d_lru), lambda j, i: (i, 0)),
                  pl.BlockSpec((tm, ss.shape[1]), lambda j, i: (i, 0)),
                  _weight_spec(d_att, tn, w_buffers, lambda j, i: (0, j)),
                  _weight_spec(d_lru, tn, w_buffers, lambda j, i: (1, j)),
                  pl.BlockSpec((tm, tn), lambda j, i: (i, j)),
                  pl.BlockSpec((MOD_ROWS, tn), lambda j, i: (0, gb + j))],
        out_specs=pl.BlockSpec((tm, tn), lambda j, i: (i, j)),
        out_shape=jax.ShapeDtypeStruct((m, n), F32),
        scratch_shapes=[pltpu.VMEM((d_att, tn), BF16), pltpu.VMEM((d_lru, tn), BF16)],
        compiler_params=_params("parallel", "arbitrary"),
        name="out_proj",
    )(mix_att, y_lru, ss, w_out, w_out, x2, mod)


def _rope_tables(seq):
    pos = jnp.arange(seq)
    row = (pos // GRID_W).astype(F32)
    col = (pos % GRID_W).astype(F32)
    n_freq = HEAD_DIM // 4
    freqs = ROPE_THETA ** (-jnp.arange(n_freq, dtype=F32) / n_freq)
    ang_r = row[:, None] * freqs
    ang_c = col[:, None] * freqs
    cos = jnp.concatenate([jnp.cos(ang_r), jnp.cos(ang_r), jnp.cos(ang_c), jnp.cos(ang_c)], axis=1)
    sin = jnp.concatenate([-jnp.sin(ang_r), jnp.sin(ang_r), -jnp.sin(ang_c), jnp.sin(ang_c)], axis=1)
    return cos, sin


def kernel(x, c, ctx, c_ctx, w_ada, b_ada, norm_w, w_in, q_norm_w, k_norm_w, conv_w, conv_b,
           lru_wa, lru_ba, lru_wx, lru_bx, lru_lambda, out_norm_att, out_norm_lru, w_out):
    assert w_ada.shape[0] == 1, "single-layer kernel: only the latent stream is produced"
    b, seq, d = x.shape
    n_ctx = ctx.shape[1]
    d_att = out_norm_att.shape[1]
    d_lru = out_norm_lru.shape[1]
    d_kv = (w_in.shape[2] - 2 * d_att - 2 * d_lru) // 2
    n_lat = b * seq
    n_all = n_lat + b * n_ctx
    assert b + 1 <= MOD_ROWS
    k_col0, v_col0, ga_col0 = d_att, d_att + d_kv, d_att + 2 * d_kv
    xl_col0 = ga_col0 + d_att
    gl_col0 = xl_col0 + d_lru

    x2 = x.reshape(n_lat, d)
    c2 = ctx.reshape(b * n_ctx, d)

    cvec = jnp.concatenate([c, c_ctx[None, :], jnp.zeros((MOD_ROWS - b - 1, d), F32)], axis=0)
    mod = _modulation(cvec, w_ada[0], b_ada)

    h = _prenorm(x2, c2, norm_w, mod, seq)
    w_in0 = w_in[0]

    cos, sin = _rope_tables(seq)
    qn = _proj_heads("proj_q", h, w_in0, n_lat, 0, d_att, 0, cos, sin, q_norm_w, seq, n_lat,
                     HEAD_DIM ** -0.5 * LOG2_E)
    assert v_col0 == k_col0 + d_kv
    kv = _proj_heads("proj_kv", h, w_in0, n_all, k_col0, 2 * d_kv, d_kv, cos, sin, k_norm_w, seq, n_lat, 1.0)
    xl = _proj_plain("proj_xl", h, w_in0, n_all, [(xl_col0, d_lru)], F32)
    gates = _proj_plain("proj_gates", h, w_in0, n_lat, [(ga_col0, d_att), (gl_col0, d_lru)], F32)

    mix_att = _attention(qn, kv, gates, out_norm_att, b, seq, n_ctx)

    n_blocks = d_lru // LRU_BLOCK_DIM
    w_cat = jnp.concatenate([lru_wa[0], lru_wx[0]], axis=-1).astype(BF16)
    bias_cat = 0.5 * jnp.concatenate([lru_ba[0].reshape(2, n_blocks, 1, LRU_BLOCK_DIM),
                                      lru_bx[0].reshape(2, n_blocks, 1, LRU_BLOCK_DIM)], axis=-1)
    y_lru, ss = _lru(xl, gates, conv_w[0], conv_b, w_cat, bias_cat, lru_lambda[0], out_norm_lru,
                     b, seq, n_ctx, d_att)

    out = _out_proj(mix_att, y_lru, ss, w_out[0], x2, mod, seq, 2 * d)
    return out.reshape(b, seq, d)
```

```python
import functools
import math

import jax
import jax.numpy as jnp
from jax import lax
from jax.experimental import pallas as pl
from jax.experimental.pallas import tpu as pltpu

F32 = jnp.float32
BF16 = jnp.bfloat16

HEAD_DIM = 128
GRID_W = 64
Q_PER_KV = 4
LRU_BLOCK_DIM = 128
CONV_WIDTH = 4
CONV_LEFT = 2
LRU_C = 8.0
ROPE_THETA = 10000.0
EPS = 1e-6
LOG2_E = 1.4426950408889634

LANES = 128
SUBLANES = 8
MXU_WIDTH = 256
MOD_ROWS = 8
V7X_VMEM_BYTES = 64 * 1024 * 1024
VMEM_LIMIT = V7X_VMEM_BYTES - 2 * 1024 * 1024
MATMUL_TEMP_BYTES = 4 * 1024 * 1024


def _params(*sem):
    return pltpu.CompilerParams(dimension_semantics=sem, vmem_limit_bytes=VMEM_LIMIT)


def _silu(x):
    half = 0.5 * x
    return half * jnp.tanh(half) + half


def _rms(x, w):
    ms = jnp.mean(x * x, axis=-1, keepdims=True)
    return x * lax.rsqrt(ms + EPS) * w


def _mod_kernel(c_ref, w_ref, b_ref, o_ref):
    s = _silu(c_ref[...]).astype(BF16)
    o_ref[...] = jnp.dot(s, w_ref[...].astype(BF16), preferred_element_type=F32) + b_ref[...]


def _modulation(cvec, w_ada, b_ada, tn=512):
    d, n = w_ada.shape
    return pl.pallas_call(
        _mod_kernel,
        grid=(n // tn,),
        in_specs=[pl.BlockSpec((MOD_ROWS, d), lambda j: (0, 0)),
                  pl.BlockSpec((d, tn), lambda j: (0, j)),
                  pl.BlockSpec((1, tn), lambda j: (0, j))],
        out_specs=pl.BlockSpec((MOD_ROWS, tn), lambda j: (0, j)),
        out_shape=jax.ShapeDtypeStruct((MOD_ROWS, n), F32),
        compiler_params=_params("parallel"),
        name="modulation",
    )(cvec, w_ada, b_ada)


def _prenorm_kernel(x_ref, c_ref, nw_ref, mod_ref, o_ref, *, d, n_x_tiles, tiles_per_batch, ctx_row):
    t = pl.program_id(0)

    def emit(src_ref, row):
        x = src_ref[...]
        shift = mod_ref[pl.ds(row, 1), 0:d]
        gain = nw_ref[...] * (1.0 + mod_ref[pl.ds(row, 1), d:2 * d])
        inv_rms = lax.rsqrt(jnp.mean(x * x, axis=-1, keepdims=True) + EPS)
        o_ref[...] = (x * inv_rms * gain + shift).astype(BF16)

    @pl.when(t < n_x_tiles)
    def _():
        emit(x_ref, t // tiles_per_batch)

    @pl.when(t >= n_x_tiles)
    def _():
        emit(c_ref, ctx_row)


def _prenorm(x2, c2, norm_w, mod, seq, tr=512):
    mx, d = x2.shape
    mc = c2.shape[0]
    nx, nc = mx // tr, mc // tr
    kern = functools.partial(_prenorm_kernel, d=d, n_x_tiles=nx, tiles_per_batch=seq // tr,
                             ctx_row=mx // seq)
    return pl.pallas_call(
        kern,
        grid=(nx + nc,),
        in_specs=[pl.BlockSpec((tr, d), lambda t: (jnp.minimum(t, nx - 1), 0)),
                  pl.BlockSpec((tr, d), lambda t: (jnp.maximum(t - nx, 0), 0)),
                  pl.BlockSpec((1, d), lambda t: (0, 0)),
                  pl.BlockSpec(mod.shape, lambda t: (0, 0))],
        out_specs=pl.BlockSpec((tr, d), lambda t: (t, 0)),
        out_shape=jax.ShapeDtypeStruct((mx + mc, d), BF16),
        compiler_params=_params("parallel"),
        name="prenorm",
    )(x2, c2, norm_w, mod)


WIDE_TN = 1024
NARROW_TN = 512


def _proj_tiles(n_cols, k, resident_bytes):
    shapes = [(1024, NARROW_TN)]
    if n_cols % WIDE_TN == 0:
        shapes.insert(0, (512, WIDE_TN))
    for tm, tn in shapes:
        if resident_bytes(tm, tn) + k * tn * 2 + 2 * k * tn * 4 + MATMUL_TEMP_BYTES <= VMEM_LIMIT:
            return tm, tn, 2
    return (*shapes[0], 1)


def _weight_spec(k, tn, buffers, index_map):
    if buffers == 1:
        return pl.BlockSpec((k, tn), index_map, pipeline_mode=pl.Buffered(1))
    return pl.BlockSpec((k, tn), index_map)


def _cast_weights_once(w_ref, wb_ref):
    @pl.when(pl.program_id(1) == 0)
    def _():
        wb_ref[...] = w_ref[...].astype(BF16)


def _rope(y, cos, sin_signed):
    lane = lax.broadcasted_iota(jnp.int32, y.shape, 1)
    first_half = (lane & (HEAD_DIM // 4)) == 0
    partner = jnp.where(first_half, pltpu.roll(y, HEAD_DIM - HEAD_DIM // 4, 1),
                        pltpu.roll(y, HEAD_DIM // 4, 1))
    return y * cos + partner * sin_signed


def _proj_heads_kernel(h_ref, w_ref, cos_ref, sin_ref, nw_ref, o_ref, wb_ref, acc0_ref, acc1_ref,
                       *, n_row_tiles, head_cols, out_scale):
    i = pl.program_id(1)
    _cast_weights_once(w_ref, wb_ref)
    accs = (acc0_ref, acc1_ref)

    def matmul_into(acc_ref):
        acc_ref[...] = jnp.dot(h_ref[...], wb_ref[...], preferred_element_type=F32)

    def finish_from(acc_ref):
        norm_w = nw_ref[...] * out_scale
        for hh in range(head_cols // HEAD_DIM):
            sl = slice(hh * HEAD_DIM, (hh + 1) * HEAD_DIM)
            o_ref[:, sl] = _rope(_rms(acc_ref[:, sl], norm_w), cos_ref[...], sin_ref[...]).astype(BF16)
        if head_cols < o_ref.shape[1]:
            o_ref[:, head_cols:] = acc_ref[:, head_cols:].astype(BF16)

    @pl.when(i == 0)
    def _():
        matmul_into(accs[0])

    for parity in range(2):
        @pl.when((i >= 1) & (i < n_row_tiles) & (i % 2 == parity))
        def _(parity=parity):
            matmul_into(accs[parity])
            finish_from(accs[1 - parity])

    @pl.when(i == n_row_tiles)
    def _():
        finish_from(accs[(n_row_tiles - 1) % 2])


def _proj_heads(name, h, w_in, n_rows, col0, n_cols, plain_cols, cos, sin, norm_w, seq, rope_rows, out_scale):
    k = h.shape[1]
    tm, tn, w_buffers = _proj_tiles(
        n_cols, k, lambda tm, tn: 2 * tm * k * 2 + 2 * tm * tn * 4 + 2 * tm * tn * 2 + 4 * tm * HEAD_DIM * 4)
    assert n_rows % tm == 0 and seq % tm == 0 and rope_rows % tm == 0 and col0 % tn == 0
    assert plain_cols == 0 or n_cols == tn
    head_cols = tn - plain_cols
    n_row_tiles = n_rows // tm
    rope_tiles = rope_rows // tm
    spt = seq // tm
    cos = jnp.concatenate([cos, jnp.ones((tm, HEAD_DIM), F32)], axis=0)
    sin = jnp.concatenate([sin, jnp.zeros((tm, HEAD_DIM), F32)], axis=0)
    kern = functools.partial(_proj_heads_kernel, n_row_tiles=n_row_tiles, head_cols=head_cols,
                             out_scale=out_scale)
    done = lambda i: jnp.maximum(i - 1, 0)
    table = lambda j, i: (jnp.where(done(i) < rope_tiles, done(i) % spt, spt), 0)
    return pl.pallas_call(
        kern,
        grid=(n_cols // tn, n_row_tiles + 1),
        in_specs=[pl.BlockSpec((tm, k), lambda j, i: (jnp.minimum(i, n_row_tiles - 1), 0)),
                  _weight_spec(k, tn, w_buffers, lambda j, i: (0, col0 // tn + j)),
                  pl.BlockSpec((tm, HEAD_DIM), table),
                  pl.BlockSpec((tm, HEAD_DIM), table),
                  pl.BlockSpec((1, HEAD_DIM), lambda j, i: (0, 0))],
        out_specs=pl.BlockSpec((tm, tn), lambda j, i: (done(i), j)),
        out_shape=jax.ShapeDtypeStruct((n_rows, n_cols), BF16),
        scratch_shapes=[pltpu.VMEM((k, tn), BF16),
                        pltpu.VMEM((tm, tn), F32),
                        pltpu.VMEM((tm, tn), F32)],
        compiler_params=_params("parallel", "arbitrary"),
        name=name,
    )(h, w_in, cos, sin, norm_w)


CONV_PAD_ROWS = 8
CONV_PIECE = 32
ACC_SLOTS = 3


def _half_conv_rows(window, half_w, half_b):
    halo = window.shape[0]
    acc = half_b
    for j in range(CONV_WIDTH):
        off = j - CONV_LEFT
        shifted = window if off == 0 else pltpu.roll(window, (-off) % halo, 0)
        acc = acc + half_w[j] * shifted[CONV_PAD_ROWS:CONV_PAD_ROWS + CONV_PIECE, :]
    return acc


def _proj_conv_kernel(h_ref, w_ref, cw_ref, cb_ref, o_ref, wb_ref, acc0_ref, acc1_ref, acc2_ref, tail_ref,
                      *, n_row_tiles, conv_tiles, tiles_per_seq):
    i = pl.program_id(1)
    tm, tn = o_ref.shape
    _cast_weights_once(w_ref, wb_ref)
    half_w = [0.5 * cw_ref[j:j + 1, :] for j in range(CONV_WIDTH)]
    half_b = 0.5 * cb_ref[...]
    zeros = jnp.zeros((CONV_PAD_ROWS, tn), F32)
    body = slice(CONV_PAD_ROWS, CONV_PAD_ROWS + tm)
    accs = (acc0_ref, acc1_ref, acc2_ref)

    @pl.when(i == 0)
    def _():
        tail_ref[...] = zeros

    def matmul_into(acc_ref):
        acc_ref[body, :] = jnp.dot(h_ref[...], wb_ref[...], preferred_element_type=F32)

    def finish(t, acc_ref, next_ref):
        in_seq = t % tiles_per_seq
        acc_ref[0:CONV_PAD_ROWS, :] = jnp.where(in_seq == 0, zeros, tail_ref[...])
        acc_ref[CONV_PAD_ROWS + tm:2 * CONV_PAD_ROWS + tm, :] = jnp.where(
            in_seq == tiles_per_seq - 1, zeros, next_ref[CONV_PAD_ROWS:2 * CONV_PAD_ROWS, :])
        for p in range(tm // CONV_PIECE):
            window = acc_ref[p * CONV_PIECE:p * CONV_PIECE + CONV_PIECE + 2 * CONV_PAD_ROWS, :]
            conv = _half_conv_rows(window, half_w, half_b)
            conv = pltpu.roll(pltpu.roll(conv, LANES // 2, 1), tn - LANES // 2, 1)
            o_ref[p * CONV_PIECE:(p + 1) * CONV_PIECE, :] = jnp.where(
                t < conv_tiles, conv, window[CONV_PAD_ROWS:CONV_PAD_ROWS + CONV_PIECE, :])
        tail_ref[...] = acc_ref[tm:tm + CONV_PAD_ROWS, :]

    for r in range(2):
        @pl.when(i == r)
        def _(r=r):
            matmul_into(accs[r])

    for r in range(ACC_SLOTS):
        @pl.when((i >= 2) & (i < n_row_tiles) & (i % ACC_SLOTS == r))
        def _(r=r):
            finish(i - 2, accs[(r + 1) % ACC_SLOTS], accs[(r + 2) % ACC_SLOTS])
            matmul_into(accs[r])

    for t in (n_row_tiles - 2, n_row_tiles - 1):
        @pl.when(i == t + 2)
        def _(t=t):
            finish(t, accs[t % ACC_SLOTS], accs[(t + 1) % ACC_SLOTS])


def _proj_conv(name, h, w_in, n_rows, col0, n_cols, conv_w, conv_b, conv_rows, seq):
    k = h.shape[1]
    tm, tn, w_buffers = _proj_tiles(
        n_cols, k, lambda tm, tn: 2 * tm * k * 2 + 2 * tm * tn * 4
        + ACC_SLOTS * (tm + 2 * CONV_PAD_ROWS) * tn * 4)
    assert n_rows % tm == 0 and seq % tm == 0 and conv_rows % seq == 0 and col0 % tn == 0
    assert tm % CONV_PIECE == 0
    n_row_tiles = n_rows // tm
    assert n_row_tiles >= ACC_SLOTS
    kern = functools.partial(_proj_conv_kernel, n_row_tiles=n_row_tiles, conv_tiles=conv_rows // tm,
                             tiles_per_seq=seq // tm)
    done = lambda i: jnp.maximum(i - 2, 0)
    return pl.pallas_call(
        kern,
        grid=(n_cols // tn, n_row_tiles + 2),
        in_specs=[pl.BlockSpec((tm, k), lambda j, i: (jnp.minimum(i, n_row_tiles - 1), 0)),
                  _weight_spec(k, tn, w_buffers, lambda j, i: (0, col0 // tn + j)),
                  pl.BlockSpec((CONV_WIDTH, tn), lambda j, i: (0, j)),
                  pl.BlockSpec((1, tn), lambda j, i: (0, j))],
        out_specs=pl.BlockSpec((tm, tn), lambda j, i: (done(i), j)),
        out_shape=jax.ShapeDtypeStruct((n_rows, n_cols), F32),
        scratch_shapes=[pltpu.VMEM((k, tn), BF16)]
        + [pltpu.VMEM((tm + 2 * CONV_PAD_ROWS, tn), F32) for _ in range(ACC_SLOTS)]
        + [pltpu.VMEM((CONV_PAD_ROWS, tn), F32)],
        compiler_params=_params("parallel", "arbitrary"),
        name=name,
    )(h, w_in, conv_w, conv_b)


def _proj_plain_kernel(h_ref, w_ref, o_ref, wb_ref):
    _cast_weights_once(w_ref, wb_ref)
    o_ref[...] = jnp.dot(h_ref[...], wb_ref[...], preferred_element_type=F32).astype(o_ref.dtype)


def _proj_plain(name, h, w_in, n_rows, col_ranges, out_dtype):
    k = h.shape[1]
    out_bytes = jnp.dtype(out_dtype).itemsize
    tm, tn, w_buffers = _proj_tiles(math.gcd(*[c for r in col_ranges for c in r]), k,
                                    lambda tm, tn: 2 * tm * k * 2 + 2 * tm * tn * out_bytes)
    assert n_rows % tm == 0
    tiles = [(start // tn, width // tn) for start, width in col_ranges]

    def col_tile(j):
        first, idx = 0, tiles[0][0] + j
        for r in range(1, len(tiles)):
            first += tiles[r - 1][1]
            idx = jnp.where(j >= first, tiles[r][0] + j - first, idx)
        return idx

    n_col_tiles = sum(n for _, n in tiles)
    return pl.pallas_call(
        _proj_plain_kernel,
        grid=(n_col_tiles, n_rows // tm),
        in_specs=[pl.BlockSpec((tm, k), lambda j, i: (i, 0)),
                  _weight_spec(k, tn, w_buffers, lambda j, i: (0, col_tile(j)))],
        out_specs=pl.BlockSpec((tm, tn), lambda j, i: (i, j)),
        out_shape=jax.ShapeDtypeStruct((n_rows, n_col_tiles * tn), out_dtype),
        scratch_shapes=[pltpu.VMEM((k, tn), BF16)],
        compiler_params=_params("parallel", "arbitrary"),
        name=name,
    )(h, w_in)


V_EXTRA_ROWS = 16


def _attn_kernel(q_ref, kl_ref, vl_ref, kc_ref, vc_ref, ga_ref, nw_ref, o_ref,
                 k_ref, vt_ref, st_ref, att_ref, *, n_q_tiles):
    tq = q_ref.shape[0]
    seq, n_ctx = kl_ref.shape[0], kc_ref.shape[0]
    n_kv = kl_ref.shape[1] // HEAD_DIM
    nt = (((1,), (1,)), ((), ()))
    i = pl.program_id(1)
    n_keys = seq + n_ctx

    @pl.when(i == 0)
    def _():
        k_ref[0:seq, :] = kl_ref[...]
        k_ref[seq:n_keys, :] = kc_ref[...]
        row = lax.broadcasted_iota(jnp.int32, (V_EXTRA_ROWS, n_keys), 0)
        tail = jnp.where(row == 0, 1.0, 0.0).astype(BF16)
        for kv in range(n_kv):
            ks = slice(kv * HEAD_DIM, (kv + 1) * HEAD_DIM)
            vt_ref[kv, 0:HEAD_DIM, 0:seq] = vl_ref[:, ks].T
            vt_ref[kv, 0:HEAD_DIM, seq:n_keys] = vc_ref[:, ks].T
            vt_ref[kv, HEAD_DIM:HEAD_DIM + V_EXTRA_ROWS, :] = tail

    def head_slices(kv):
        return [slice((kv * Q_PER_KV + g) * HEAD_DIM, (kv * Q_PER_KV + g + 1) * HEAD_DIM)
                for g in range(Q_PER_KV)]

    def scores(kv, slot):
        q4 = jnp.concatenate([q_ref[:, hs] for hs in head_slices(kv)], axis=0)
        st_ref[slot] = lax.dot_general(k_ref[:, kv * HEAD_DIM:(kv + 1) * HEAD_DIM], q4, nt,
                                       preferred_element_type=F32)

    def softmax_values(kv, slot, tile_slot):
        m = jnp.max(st_ref[slot], axis=0, keepdims=True)
        pt = jnp.exp2(st_ref[slot] - m).astype(BF16)
        res = jnp.dot(vt_ref[kv], pt, preferred_element_type=F32)
        ot = res[:HEAD_DIM, :] / res[HEAD_DIM:HEAD_DIM + 1, :]
        for g, hs in enumerate(head_slices(kv)):
            att_ref[tile_slot, :, hs] = ot[:, g * tq:(g + 1) * tq].T

    def attend(tile_slot):
        scores(0, 0)
        for kv in range(n_kv):
            if kv + 1 < n_kv:
                scores(kv + 1, (kv + 1) % 2)
            softmax_values(kv, kv % 2, tile_slot)

    def finish(tile_slot):
        att = att_ref[tile_slot]
        inv_rms = lax.rsqrt(jnp.mean(att * att, axis=-1, keepdims=True) + EPS)
        o_ref[...] = (att * inv_rms * (_silu(ga_ref[...]) * nw_ref[...])).astype(BF16)

    @pl.when(i == 0)
    def _():
        attend(0)

    @pl.when((i >= 1) & (i < n_q_tiles))
    def _():
        attend(i % 2)
        finish((i - 1) % 2)

    @pl.when(i == n_q_tiles)
    def _():
        finish((n_q_tiles - 1) % 2)


def _attention(qn, kv, gates, norm_w, b, seq, n_ctx, tq=256):
    n_lat, d_att = qn.shape
    d_kv = kv.shape[1] // 2
    qt = seq // tq
    ctx_blk0 = n_lat // n_ctx
    this = lambda bi, i: (bi * qt + jnp.minimum(i, qt - 1), 0)
    done = lambda bi, i: (bi * qt + jnp.maximum(i - 1, 0), 0)
    return pl.pallas_call(
        functools.partial(_attn_kernel, n_q_tiles=qt),
        grid=(b, qt + 1),
        in_specs=[pl.BlockSpec((tq, d_att), this),
                  pl.BlockSpec((seq, d_kv), lambda bi, i: (bi, 0), pipeline_mode=pl.Buffered(1)),
                  pl.BlockSpec((seq, d_kv), lambda bi, i: (bi, 1), pipeline_mode=pl.Buffered(1)),
                  pl.BlockSpec((n_ctx, d_kv), lambda bi, i: (ctx_blk0 + bi, 0), pipeline_mode=pl.Buffered(1)),
                  pl.BlockSpec((n_ctx, d_kv), lambda bi, i: (ctx_blk0 + bi, 1), pipeline_mode=pl.Buffered(1)),
                  pl.BlockSpec((tq, d_att), done),
                  pl.BlockSpec((1, d_att), lambda bi, i: (0, 0))],
        out_specs=pl.BlockSpec((tq, d_att), done),
        out_shape=jax.ShapeDtypeStruct((n_lat, d_att), BF16),
        scratch_shapes=[pltpu.VMEM((seq + n_ctx, d_kv), BF16),
                        pltpu.VMEM((d_kv // HEAD_DIM, HEAD_DIM + V_EXTRA_ROWS, seq + n_ctx), BF16),
                        pltpu.VMEM((2, seq + n_ctx, Q_PER_KV * tq), F32),
                        pltpu.VMEM((2, tq, d_att), F32)],
        compiler_params=_params("parallel", "arbitrary"),
        name="attention",
    )(qn, kv, kv, kv, kv, gates, norm_w)


LRU_SEGS = SUBLANES
GATE_ROWS = 256
SCAN_STEPS = 4


def _softplus(z):
    return jnp.maximum(z, 0.0) + jnp.log1p(jnp.exp(-jnp.abs(z)))


def _segment_len(total):
    seg = -(-total // LRU_SEGS)
    return seg + (SUBLANES // 2 - seg) % SUBLANES


def _scan8(a, bv, reverse):
    row = lax.broadcasted_iota(jnp.int32, a.shape, 0)
    for s in (1, 2, 4):
        if reverse:
            keep = row < SUBLANES - s
            shift = SUBLANES - s
        else:
            keep = row >= s
            shift = s
        a_prev = jnp.where(keep, pltpu.roll(a, shift, 0), 1.0)
        b_prev = jnp.where(keep, pltpu.roll(bv, shift, 0), 0.0)
        bv = bv + a * b_prev
        a = a * a_prev
    return a, bv


def _lru_kernel(ul_ref, xc_ref, cw_ref, cb_ref, w_ref, bias_ref, lam_ref, gl_ref, nw_ref, y_ref, ss_ref,
                xp_ref, u_ref, a_ref, b_ref, hs_ref, *, seq, n_ctx, seg_len):
    cg = y_ref.shape[1]
    n_blk = cg // LRU_BLOCK_DIM
    total = n_ctx + seq
    padded = LRU_SEGS * seg_len
    n_chunks = total // GATE_ROWS
    gate_unroll = next(u for u in (3, 2, 1) if n_chunks % u == 0)

    @pl.when(pl.program_id(1) == 0)
    def _():
        ss_ref[...] = jnp.zeros(ss_ref.shape, F32)

    for n in range(n_blk):
        a_ref[n, total:padded, :] = jnp.ones((padded - total, LRU_BLOCK_DIM), F32)
        b_ref[n, total:padded, :] = jnp.zeros((padded - total, LRU_BLOCK_DIM), F32)
    zeros_pad = jnp.zeros((CONV_PAD_ROWS, cg), F32)
    halo = CONV_PIECE + 2 * CONV_PAD_ROWS
    half_w = [0.5 * cw_ref[j:j + 1, :] for j in range(CONV_WIDTH)]
    half_b = 0.5 * cb_ref[...]
    xp_ref[0:CONV_PAD_ROWS, :] = zeros_pad
    xp_ref[CONV_PAD_ROWS:CONV_PAD_ROWS + n_ctx, :] = xc_ref[...]
    xp_ref[CONV_PAD_ROWS + n_ctx:2 * CONV_PAD_ROWS + n_ctx, :] = zeros_pad
    for p in range(n_ctx // CONV_PIECE):
        piece = _half_conv_rows(xp_ref[p * CONV_PIECE:p * CONV_PIECE + halo, :], half_w, half_b)
        for u0 in (0, n_ctx + seq):
            u_ref[u0 + p * CONV_PIECE:u0 + (p + 1) * CONV_PIECE, :] = piece
    u_ref[n_ctx:n_ctx + seq, :] = ul_ref[...]

    for d in range(2):
        reverse = d == 1
        seq_row0 = n_ctx if reverse else 0
        half_decay = [0.5 * LRU_C * _softplus(-lam_ref[d:d + 1, n * LRU_BLOCK_DIM:(n + 1) * LRU_BLOCK_DIM])
                      for n in range(n_blk)]

        def gate_body(cb, _, d=d, seq_row0=seq_row0, half_decay=half_decay):
            for cc in range(gate_unroll):
                s0 = pl.multiple_of((cb * gate_unroll + cc) * GATE_ROWS, GATE_ROWS)
                hu = u_ref[pl.ds(pl.multiple_of(seq_row0 + s0, SUBLANES), GATE_ROWS), :]
                hub = hu.astype(BF16)
                for n in range(n_blk):
                    sl = slice(n * LRU_BLOCK_DIM, (n + 1) * LRU_BLOCK_DIM)
                    pre = jnp.dot(hub[:, sl], w_ref[d, n], preferred_element_type=F32) + bias_ref[d, n]
                    t_r = jnp.tanh(pre[:, :LRU_BLOCK_DIM])
                    t_i = jnp.tanh(pre[:, LRU_BLOCK_DIM:])
                    neg_log_a = t_r * half_decay[n] + half_decay[n]
                    a = jnp.exp2(neg_log_a * (-LOG2_E))
                    a_ref[n, pl.ds(s0, GATE_ROWS), :] = a
                    w = jnp.tanh(neg_log_a) * (1.0 + a * a)
                    mult = jnp.where(w > 0.0, w * lax.rsqrt(w), 0.0)
                    b_ref[n, pl.ds(s0, GATE_ROWS), :] = mult * ((t_i + 1.0) * hu[:, sl])
            return 0

        lax.fori_loop(0, n_chunks // gate_unroll, gate_body, 0)

        def load_steps(tb, reverse=reverse):
            rows = []
            base = (seg_len // SCAN_STEPS - 1 - tb) * SCAN_STEPS if reverse else tb * SCAN_STEPS
            for k in range(SCAN_STEPS):
                i = base + (SCAN_STEPS - 1 - k if reverse else k)
                idx = pl.ds(i, LRU_SEGS, stride=seg_len)
                rows.append((idx, [a_ref[n, idx, :] for n in range(n_blk)],
                             [b_ref[n, idx, :] for n in range(n_blk)]))
            return rows

        def local_body(tb, carry):
            hs, decs = list(carry[0]), list(carry[1])
            for _, a, bv in load_steps(tb):
                for n in range(n_blk):
                    hs[n] = a[n] * hs[n] + bv[n]
                    decs[n] = decs[n] * a[n]
            return tuple(hs), tuple(decs)

        zero = tuple(jnp.zeros((LRU_SEGS, LRU_BLOCK_DIM), F32) for _ in range(n_blk))
        one = tuple(jnp.ones((LRU_SEGS, LRU_BLOCK_DIM), F32) for _ in range(n_blk))
        h_end, dec_end = lax.fori_loop(0, seg_len // SCAN_STEPS, local_body, (zero, one))

        starts = []
        for n in range(n_blk):
            _, state = _scan8(dec_end[n], h_end[n], reverse)
            row = lax.broadcasted_iota(jnp.int32, state.shape, 0)
            if reverse:
                starts.append(jnp.where(row < LRU_SEGS - 1, pltpu.roll(state, LRU_SEGS - 1, 0), 0.0))
            else:
                starts.append(jnp.where(row >= 1, pltpu.roll(state, 1, 0), 0.0))

        def final_body(tb, hs):
            hs = list(hs)
            for idx, a, bv in load_steps(tb):
                for n in range(n_blk):
                    hs[n] = a[n] * hs[n] + bv[n]
                    hs_ref[d, n, idx, :] = hs[n]
            return tuple(hs)

        lax.fori_loop(0, seg_len // SCAN_STEPS, final_body, tuple(starts))

    for r0 in range(0, seq, GATE_ROWS):
        rows = slice(r0, r0 + GATE_ROWS)
        lru = jnp.concatenate(
            [hs_ref[0, n, n_ctx + r0:n_ctx + r0 + GATE_ROWS, :] + hs_ref[1, n, r0:r0 + GATE_ROWS, :]
             for n in range(n_blk)], axis=1)
        y_ref[rows, :] = (lru * nw_ref[...] * _silu(gl_ref[rows, :])).astype(BF16)
        ss_ref[rows, :] += jnp.broadcast_to(jnp.sum(lru * lru, axis=-1, keepdims=True), (GATE_ROWS, LANES))


def _lru(xl, gates, conv_w, conv_b, w_cat, bias_cat, lam, norm_w, b, seq, n_ctx, d_att, cg=512):
    n_lat = b * seq
    d_lru = xl.shape[1]
    n_blk = cg // LRU_BLOCK_DIM
    n_groups = d_lru // cg
    seg_len = _segment_len(n_ctx + seq)
    assert n_ctx % GATE_ROWS == 0 and seq % GATE_ROWS == 0 and GATE_ROWS % CONV_PIECE == 0
    assert seg_len % SCAN_STEPS == 0
    kern = functools.partial(_lru_kernel, seq=seq, n_ctx=n_ctx, seg_len=seg_len)
    ctx_blk0 = n_lat // n_ctx
    gl_blk0 = d_att // cg
    scan_rows = LRU_SEGS * seg_len
    return pl.pallas_call(
        kern,
        grid=(b, n_groups),
        in_specs=[pl.BlockSpec((seq, cg), lambda bi, g: (bi, g)),
                  pl.BlockSpec((n_ctx, cg), lambda bi, g: (ctx_blk0 + bi, g)),
                  pl.BlockSpec((CONV_WIDTH, cg), lambda bi, g: (0, g)),
                  pl.BlockSpec((1, cg), lambda bi, g: (0, g)),
                  pl.BlockSpec((2, n_blk, LRU_BLOCK_DIM, 2 * LRU_BLOCK_DIM), lambda bi, g: (0, g, 0, 0)),
                  pl.BlockSpec((2, n_blk, 1, 2 * LRU_BLOCK_DIM), lambda bi, g: (0, g, 0, 0)),
                  pl.BlockSpec((2, cg), lambda bi, g: (0, g)),
                  pl.BlockSpec((seq, cg), lambda bi, g: (bi, gl_blk0 + g)),
                  pl.BlockSpec((1, cg), lambda bi, g: (0, g))],
        out_specs=[pl.BlockSpec((seq, cg), lambda bi, g: (bi, g)),
                   pl.BlockSpec((seq, LANES), lambda bi, g: (bi, 0))],
        out_shape=[jax.ShapeDtypeStruct((n_lat, d_lru), BF16),
                   jax.ShapeDtypeStruct((n_lat, LANES), F32)],
        scratch_shapes=[pltpu.VMEM((n_ctx + 2 * CONV_PAD_ROWS, cg), F32),
                        pltpu.VMEM((2 * n_ctx + seq, cg), F32),
                        pltpu.VMEM((n_blk, scan_rows, LRU_BLOCK_DIM), F32),
                        pltpu.VMEM((n_blk, scan_rows, LRU_BLOCK_DIM), F32),
                        pltpu.VMEM((2, n_blk, scan_rows, LRU_BLOCK_DIM), F32)],
        compiler_params=_params("parallel", "arbitrary"),
        name="rglru",
    )(xl, xl, conv_w, conv_b, w_cat, bias_cat, lam, gates, norm_w)


def _out_kernel(ma_ref, yl_ref, ss_ref, wa_ref, wl_ref, x_ref, gate_ref, o_ref, wab_ref, wlb_ref,
                *, tiles_per_batch, d_lru):
    @pl.when(pl.program_id(1) == 0)
    def _():
        wab_ref[...] = wa_ref[...].astype(BF16)
        wlb_ref[...] = wl_ref[...].astype(BF16)

    bi = pl.program_id(1) // tiles_per_batch
    gate = gate_ref[pl.ds(bi, 1), :]
    sumsq = ss_ref[:, 0:LANES]
    for g in range(1, ss_ref.shape[1] // LANES):
        sumsq = sumsq + ss_ref[:, g * LANES:(g + 1) * LANES]
    inv_rms = lax.rsqrt(sumsq * (1.0 / d_lru) + EPS)
    inv_rms = jnp.concatenate([inv_rms] * (o_ref.shape[1] // LANES), axis=1)
    acc = jnp.dot(ma_ref[...], wab_ref[...], preferred_element_type=F32)
    acc = acc + inv_rms * jnp.dot(yl_ref[...], wlb_ref[...], preferred_element_type=F32)
    o_ref[...] = x_ref[...] + gate * acc


def _out_proj(mix_att, y_lru, ss, w_out, x2, mod, seq, gate_col0):
    m, d_att = mix_att.shape
    d_lru = y_lru.shape[1]
    assert d_att == d_lru
    n = w_out.shape[1]
    tm, tn, w_buffers = _proj_tiles(
        n, d_att + d_lru,
        lambda tm, tn: 2 * tm * (d_att + d_lru) * 2 + 4 * tm * tn * 4 + 2 * tm * LANES * 4)
    assert m % tm == 0 and seq % tm == 0 and gate_col0 % tn == 0
    gb = gate_col0 // tn
    kern = functools.partial(_out_kernel, tiles_per_batch=seq // tm, d_lru=d_lru)
    return pl.pallas_call(
        kern,
        grid=(n // tn, m // tm),
        in_specs=[pl.BlockSpec((tm, d_att), lambda j, i: (i, 0)),
                  pl.BlockSpec((tm, d_lru), lambda j, i: (i, 0)),
                  pl.BlockSpec((tm, ss.shape[1]), lambda j, i: (i, 0)),
                  _weight_spec(d_att, tn, w_buffers, lambda j, i: (0, j)),
                  _weight_spec(d_lru, tn, w_buffers, lambda j, i: (1, j)),
                  pl.BlockSpec((tm, tn), lambda j, i: (i, j)),
                  pl.BlockSpec((MOD_ROWS, tn), lambda j, i: (0, gb + j))],
        out_specs=pl.BlockSpec((tm, tn), lambda j, i: (i, j)),
        out_shape=jax.ShapeDtypeStruct((m, n), F32),
        scratch_shapes=[pltpu.VMEM((d_att, tn), BF16), pltpu.VMEM((d_lru, tn), BF16)],
        compiler_params=_params("parallel", "arbitrary"),
        name="out_proj",
    )(mix_att, y_lru, ss, w_out, w_out, x2, mod)


def _rope_tables(seq):
    pos = jnp.arange(seq)
    row = (pos // GRID_W).astype(F32)
    col = (pos % GRID_W).astype(F32)
    n_freq = HEAD_DIM // 4
    freqs = ROPE_THETA ** (-jnp.arange(n_freq, dtype=F32) / n_freq)
    ang_r = row[:, None] * freqs
    ang_c = col[:, None] * freqs
    cos = jnp.concatenate([jnp.cos(ang_r), jnp.cos(ang_r), jnp.cos(ang_c), jnp.cos(ang_c)], axis=1)
    sin = jnp.concatenate([-jnp.sin(ang_r), jnp.sin(ang_r), -jnp.sin(ang_c), jnp.sin(ang_c)], axis=1)
    return cos, sin


def kernel(x, c, ctx, c_ctx, w_ada, b_ada, norm_w, w_in, q_norm_w, k_norm_w, conv_w, conv_b,
           lru_wa, lru_ba, lru_wx, lru_bx, lru_lambda, out_norm_att, out_norm_lru, w_out):
    assert w_ada.shape[0] == 1, "single-layer kernel: only the latent stream is produced"
    b, seq, d = x.shape
    n_ctx = ctx.shape[1]
    d_att = out_norm_att.shape[1]
    d_lru = out_norm_lru.shape[1]
    d_kv = (w_in.shape[2] - 2 * d_att - 2 * d_lru) // 2
    n_lat = b * seq
    n_all = n_lat + b * n_ctx
    assert b + 1 <= MOD_ROWS
    k_col0, v_col0, ga_col0 = d_att, d_att + d_kv, d_att + 2 * d_kv
    xl_col0 = ga_col0 + d_att
    gl_col0 = xl_col0 + d_lru

    x2 = x.reshape(n_lat, d)
    c2 = ctx.reshape(b * n_ctx, d)

    cvec = jnp.concatenate([c, c_ctx[None, :], jnp.zeros((MOD_ROWS - b - 1, d), F32)], axis=0)
    mod = _modulation(cvec, w_ada[0], b_ada)

    h = _prenorm(x2, c2, norm_w, mod, seq)
    w_in0 = w_in[0]

    cos, sin = _rope_tables(seq)
    qn = _proj_heads("proj_q", h, w_in0, n_lat, 0, d_att, 0, cos, sin, q_norm_w, seq, n_lat,
                     HEAD_DIM ** -0.5 * LOG2_E)
    assert v_col0 == k_col0 + d_kv
    kv = _proj_heads("proj_kv", h, w_in0, n_all, k_col0, 2 * d_kv, d_kv, cos, sin, k_norm_w, seq, n_lat, 1.0)
    xl = _proj_conv("proj_xl", h, w_in0, n_all, xl_col0, d_lru, conv_w[0], conv_b, n_lat, seq)
    gates = _proj_plain("proj_gates", h, w_in0, n_lat, [(ga_col0, d_att), (gl_col0, d_lru)], F32)

    mix_att = _attention(qn, kv, gates, out_norm_att, b, seq, n_ctx)

    n_blocks = d_lru // LRU_BLOCK_DIM
    w_cat = jnp.concatenate([lru_wa[0], lru_wx[0]], axis=-1).astype(BF16)
    bias_cat = 0.5 * jnp.concatenate([lru_ba[0].reshape(2, n_blocks, 1, LRU_BLOCK_DIM),
                                      lru_bx[0].reshape(2, n_blocks, 1, LRU_BLOCK_DIM)], axis=-1)
    y_lru, ss = _lru(xl, gates, conv_w[0], conv_b, w_cat, bias_cat, lru_lambda[0], out_norm_lru,
                     b, seq, n_ctx, d_att)

    out = _out_proj(mix_att, y_lru, ss, w_out[0], x2, mod, seq, 2 * d)
    return out.reshape(b, seq, d)
```

```python
import functools
import math

import jax
import jax.numpy as jnp
from jax import lax
from jax.experimental import pallas as pl
from jax.experimental.pallas import tpu as pltpu

F32 = jnp.float32
BF16 = jnp.bfloat16

HEAD_DIM = 128
GRID_W = 64
Q_PER_KV = 4
LRU_BLOCK_DIM = 128
CONV_WIDTH = 4
CONV_LEFT = 2
LRU_C = 8.0
ROPE_THETA = 10000.0
EPS = 1e-6
LOG2_E = 1.4426950408889634

LANES = 128
SUBLANES = 8
MXU_WIDTH = 256
MOD_ROWS = 8
V7X_VMEM_BYTES = 64 * 1024 * 1024
VMEM_LIMIT = V7X_VMEM_BYTES - 2 * 1024 * 1024
MATMUL_TEMP_BYTES = 4 * 1024 * 1024


def _params(*sem):
    return pltpu.CompilerParams(dimension_semantics=sem, vmem_limit_bytes=VMEM_LIMIT)


def _silu(x):
    half = 0.5 * x
    return half * jnp.tanh(half) + half


def _rms(x, w):
    ms = jnp.mean(x * x, axis=-1, keepdims=True)
    return x * lax.rsqrt(ms + EPS) * w


def _mod_kernel(c_ref, w_ref, b_ref, o_ref):
    s = _silu(c_ref[...]).astype(BF16)
    o_ref[...] = jnp.dot(s, w_ref[...].astype(BF16), preferred_element_type=F32) + b_ref[...]


def _modulation(cvec, w_ada, b_ada, tn=512):
    d, n = w_ada.shape
    return pl.pallas_call(
        _mod_kernel,
        grid=(n // tn,),
        in_specs=[pl.BlockSpec((MOD_ROWS, d), lambda j: (0, 0)),
                  pl.BlockSpec((d, tn), lambda j: (0, j)),
                  pl.BlockSpec((1, tn), lambda j: (0, j))],
        out_specs=pl.BlockSpec((MOD_ROWS, tn), lambda j: (0, j)),
        out_shape=jax.ShapeDtypeStruct((MOD_ROWS, n), F32),
        compiler_params=_params("parallel"),
        name="modulation",
    )(cvec, w_ada, b_ada)


def _prenorm_kernel(x_ref, c_ref, nw_ref, mod_ref, o_ref, *, d, n_x_tiles, tiles_per_batch, ctx_row):
    t = pl.program_id(0)

    def emit(src_ref, row):
        x = src_ref[...]
        shift = mod_ref[pl.ds(row, 1), 0:d]
        gain = nw_ref[...] * (1.0 + mod_ref[pl.ds(row, 1), d:2 * d])
        inv_rms = lax.rsqrt(jnp.mean(x * x, axis=-1, keepdims=True) + EPS)
        o_ref[...] = (x * inv_rms * gain + shift).astype(BF16)

    @pl.when(t < n_x_tiles)
    def _():
        emit(x_ref, t // tiles_per_batch)

    @pl.when(t >= n_x_tiles)
    def _():
        emit(c_ref, ctx_row)


def _prenorm(x2, c2, norm_w, mod, seq, tr=512):
    mx, d = x2.shape
    mc = c2.shape[0]
    nx, nc = mx // tr, mc // tr
    kern = functools.partial(_prenorm_kernel, d=d, n_x_tiles=nx, tiles_per_batch=seq // tr,
                             ctx_row=mx // seq)
    return pl.pallas_call(
        kern,
        grid=(nx + nc,),
        in_specs=[pl.BlockSpec((tr, d), lambda t: (jnp.minimum(t, nx - 1), 0)),
                  pl.BlockSpec((tr, d), lambda t: (jnp.maximum(t - nx, 0), 0)),
                  pl.BlockSpec((1, d), lambda t: (0, 0)),
                  pl.BlockSpec(mod.shape, lambda t: (0, 0))],
        out_specs=pl.BlockSpec((tr, d), lambda t: (t, 0)),
        out_shape=jax.ShapeDtypeStruct((mx + mc, d), BF16),
        compiler_params=_params("parallel"),
        name="prenorm",
    )(x2, c2, norm_w, mod)


WIDE_TN = 1024
NARROW_TN = 512


def _proj_tiles(n_cols, k, resident_bytes):
    shapes = [(1024, NARROW_TN)]
    if n_cols % WIDE_TN == 0:
        shapes.insert(0, (512, WIDE_TN))
    for tm, tn in shapes:
        if resident_bytes(tm, tn) + k * tn * 2 + 2 * k * tn * 4 + MATMUL_TEMP_BYTES <= VMEM_LIMIT:
            return tm, tn, 2
    return (*shapes[0], 1)


def _weight_spec(k, tn, buffers, index_map):
    if buffers == 1:
        return pl.BlockSpec((k, tn), index_map, pipeline_mode=pl.Buffered(1))
    return pl.BlockSpec((k, tn), index_map)


def _cast_weights_once(w_ref, wb_ref):
    @pl.when(pl.program_id(1) == 0)
    def _():
        wb_ref[...] = w_ref[...].astype(BF16)


def _rope(y, cos, sin_signed):
    lane = lax.broadcasted_iota(jnp.int32, y.shape, 1)
    first_half = (lane & (HEAD_DIM // 4)) == 0
    partner = jnp.where(first_half, pltpu.roll(y, HEAD_DIM - HEAD_DIM // 4, 1),
                        pltpu.roll(y, HEAD_DIM // 4, 1))
    return y * cos + partner * sin_signed


def _proj_heads_kernel(h_ref, w_ref, cos_ref, sin_ref, nw_ref, o_ref, wb_ref, acc0_ref, acc1_ref,
                       *, n_row_tiles, head_cols, out_scale):
    i = pl.program_id(1)
    _cast_weights_once(w_ref, wb_ref)
    accs = (acc0_ref, acc1_ref)

    def matmul_into(acc_ref):
        acc_ref[...] = jnp.dot(h_ref[...], wb_ref[...], preferred_element_type=F32)

    def finish_from(acc_ref):
        norm_w = nw_ref[...] * out_scale
        for hh in range(head_cols // HEAD_DIM):
            sl = slice(hh * HEAD_DIM, (hh + 1) * HEAD_DIM)
            o_ref[:, sl] = _rope(_rms(acc_ref[:, sl], norm_w), cos_ref[...], sin_ref[...]).astype(BF16)
        if head_cols < o_ref.shape[1]:
            o_ref[:, head_cols:] = acc_ref[:, head_cols:].astype(BF16)

    @pl.when(i == 0)
    def _():
        matmul_into(accs[0])

    for parity in range(2):
        @pl.when((i >= 1) & (i < n_row_tiles) & (i % 2 == parity))
        def _(parity=parity):
            matmul_into(accs[parity])
            finish_from(accs[1 - parity])

    @pl.when(i == n_row_tiles)
    def _():
        finish_from(accs[(n_row_tiles - 1) % 2])


def _proj_heads(name, h, w_in, n_rows, col0, n_cols, plain_cols, cos, sin, norm_w, seq, rope_rows, out_scale):
    k = h.shape[1]
    tm, tn, w_buffers = _proj_tiles(
        n_cols, k, lambda tm, tn: 2 * tm * k * 2 + 2 * tm * tn * 4 + 2 * tm * tn * 2 + 4 * tm * HEAD_DIM * 4)
    assert n_rows % tm == 0 and seq % tm == 0 and rope_rows % tm == 0 and col0 % tn == 0
    assert plain_cols == 0 or n_cols == tn
    head_cols = tn - plain_cols
    n_row_tiles = n_rows // tm
    rope_tiles = rope_rows // tm
    spt = seq // tm
    cos = jnp.concatenate([cos, jnp.ones((tm, HEAD_DIM), F32)], axis=0)
    sin = jnp.concatenate([sin, jnp.zeros((tm, HEAD_DIM), F32)], axis=0)
    kern = functools.partial(_proj_heads_kernel, n_row_tiles=n_row_tiles, head_cols=head_cols,
                             out_scale=out_scale)
    done = lambda i: jnp.maximum(i - 1, 0)
    table = lambda j, i: (jnp.where(done(i) < rope_tiles, done(i) % spt, spt), 0)
    return pl.pallas_call(
        kern,
        grid=(n_cols // tn, n_row_tiles + 1),
        in_specs=[pl.BlockSpec((tm, k), lambda j, i: (jnp.minimum(i, n_row_tiles - 1), 0)),
                  _weight_spec(k, tn, w_buffers, lambda j, i: (0, col0 // tn + j)),
                  pl.BlockSpec((tm, HEAD_DIM), table),
                  pl.BlockSpec((tm, HEAD_DIM), table),
                  pl.BlockSpec((1, HEAD_DIM), lambda j, i: (0, 0))],
        out_specs=pl.BlockSpec((tm, tn), lambda j, i: (done(i), j)),
        out_shape=jax.ShapeDtypeStruct((n_rows, n_cols), BF16),
        scratch_shapes=[pltpu.VMEM((k, tn), BF16),
                        pltpu.VMEM((tm, tn), F32),
                        pltpu.VMEM((tm, tn), F32)],
        compiler_params=_params("parallel", "arbitrary"),
        name=name,
    )(h, w_in, cos, sin, norm_w)


CONV_PAD_ROWS = 8
CONV_PIECE = 32


def _half_conv_rows(window, half_w, half_b):
    halo = window.shape[0]
    acc = half_b
    for j in range(CONV_WIDTH):
        off = j - CONV_LEFT
        shifted = window if off == 0 else pltpu.roll(window, (-off) % halo, 0)
        acc = acc + half_w[j] * shifted[CONV_PAD_ROWS:CONV_PAD_ROWS + CONV_PIECE, :]
    return acc


def _proj_conv_side_kernel(h_ref, w_ref, x_ref, top_ref, bot_ref, cw_ref, cb_ref, o_ref, u_ref, wb_ref,
                           *, tiles_per_seq):
    i = pl.program_id(1)
    _cast_weights_once(w_ref, wb_ref)
    o_ref[...] = jnp.dot(h_ref[...], wb_ref[...], preferred_element_type=F32)
    tm, cw = x_ref.shape
    half_w = [0.5 * cw_ref[j:j + 1, :] for j in range(CONV_WIDTH)]
    half_b = 0.5 * cb_ref[...]
    in_seq = i % tiles_per_seq
    top = jnp.where(in_seq == 0, 0.0, top_ref[...])
    bot = jnp.where(in_seq == tiles_per_seq - 1, 0.0, bot_ref[...])
    for p in range(tm // CONV_PIECE):
        lo, hi = p * CONV_PIECE - CONV_PAD_ROWS, (p + 1) * CONV_PIECE + CONV_PAD_ROWS
        parts = [x_ref[max(lo, 0):min(hi, tm), :]]
        if lo < 0:
            parts.insert(0, top)
        if hi > tm:
            parts.append(bot)
        window = parts[0] if len(parts) == 1 else jnp.concatenate(parts, axis=0)
        conv = _half_conv_rows(window, half_w, half_b)
        conv = pltpu.roll(pltpu.roll(conv, LANES // 2, 1), cw - LANES // 2, 1)
        u_ref[p * CONV_PIECE:(p + 1) * CONV_PIECE, :] = conv


def _proj_conv_side(name, h, w_in, n_rows, col_ranges, x, conv_w, conv_b, seq):
    k = h.shape[1]
    d_x = x.shape[1]
    n_cols = sum(width for _, width in col_ranges)
    conv_cols = lambda tn: d_x * tn // n_cols
    tm, tn, w_buffers = _proj_tiles(
        math.gcd(*[c for r in col_ranges for c in r]), k,
        lambda tm, tn: 2 * tm * k * 2 + 2 * tm * tn * 4 + 4 * tm * conv_cols(tn) * 4)
    tiles = [(start // tn, width // tn) for start, width in col_ranges]
    n_col_tiles = n_cols // tn
    cw = conv_cols(tn)
    assert n_rows % tm == 0 and seq % tm == 0 and tm % CONV_PIECE == 0 and cw * n_col_tiles == d_x
    assert x.shape[0] >= n_rows + CONV_PAD_ROWS
    halo_blocks = tm // CONV_PAD_ROWS

    def col_tile(j):
        first, idx = 0, tiles[0][0] + j
        for r in range(1, len(tiles)):
            first += tiles[r - 1][1]
            idx = jnp.where(j >= first, tiles[r][0] + j - first, idx)
        return idx

    kern = functools.partial(_proj_conv_side_kernel, tiles_per_seq=seq // tm)
    return pl.pallas_call(
        kern,
        grid=(n_col_tiles, n_rows // tm),
        in_specs=[pl.BlockSpec((tm, k), lambda j, i: (i, 0)),
                  _weight_spec(k, tn, w_buffers, lambda j, i: (0, col_tile(j))),
                  pl.BlockSpec((tm, cw), lambda j, i: (i, j)),
                  pl.BlockSpec((CONV_PAD_ROWS, cw), lambda j, i: (jnp.maximum(i * halo_blocks - 1, 0), j)),
                  pl.BlockSpec((CONV_PAD_ROWS, cw), lambda j, i: ((i + 1) * halo_blocks, j)),
                  pl.BlockSpec((CONV_WIDTH, cw), lambda j, i: (0, j)),
                  pl.BlockSpec((1, cw), lambda j, i: (0, j))],
        out_specs=[pl.BlockSpec((tm, tn), lambda j, i: (i, j)),
                   pl.BlockSpec((tm, cw), lambda j, i: (i, j))],
        out_shape=[jax.ShapeDtypeStruct((n_rows, n_cols), F32),
                   jax.ShapeDtypeStruct((n_rows, d_x), F32)],
        scratch_shapes=[pltpu.VMEM((k, tn), BF16)],
        compiler_params=_params("parallel", "arbitrary"),
        name=name,
    )(h, w_in, x, x, x, conv_w, conv_b)


def _proj_plain_kernel(h_ref, w_ref, o_ref, wb_ref):
    _cast_weights_once(w_ref, wb_ref)
    o_ref[...] = jnp.dot(h_ref[...], wb_ref[...], preferred_element_type=F32).astype(o_ref.dtype)


def _proj_plain(name, h, w_in, n_rows, col_ranges, out_dtype):
    k = h.shape[1]
    out_bytes = jnp.dtype(out_dtype).itemsize
    tm, tn, w_buffers = _proj_tiles(math.gcd(*[c for r in col_ranges for c in r]), k,
                                    lambda tm, tn: 2 * tm * k * 2 + 2 * tm * tn * out_bytes)
    assert n_rows % tm == 0
    tiles = [(start // tn, width // tn) for start, width in col_ranges]

    def col_tile(j):
        first, idx = 0, tiles[0][0] + j
        for r in range(1, len(tiles)):
            first += tiles[r - 1][1]
            idx = jnp.where(j >= first, tiles[r][0] + j - first, idx)
        return idx

    n_col_tiles = sum(n for _, n in tiles)
    return pl.pallas_call(
        _proj_plain_kernel,
        grid=(n_col_tiles, n_rows // tm),
        in_specs=[pl.BlockSpec((tm, k), lambda j, i: (i, 0)),
                  _weight_spec(k, tn, w_buffers, lambda j, i: (0, col_tile(j)))],
        out_specs=pl.BlockSpec((tm, tn), lambda j, i: (i, j)),
        out_shape=jax.ShapeDtypeStruct((n_rows, n_col_tiles * tn), out_dtype),
        scratch_shapes=[pltpu.VMEM((k, tn), BF16)],
        compiler_params=_params("parallel", "arbitrary"),
        name=name,
    )(h, w_in)


V_EXTRA_ROWS = 16


def _attn_kernel(q_ref, kl_ref, vl_ref, kc_ref, vc_ref, ga_ref, nw_ref, o_ref,
                 k_ref, vt_ref, st_ref, att_ref, *, n_q_tiles):
    tq = q_ref.shape[0]
    seq, n_ctx = kl_ref.shape[0], kc_ref.shape[0]
    n_kv = kl_ref.shape[1] // HEAD_DIM
    nt = (((1,), (1,)), ((), ()))
    i = pl.program_id(1)
    n_keys = seq + n_ctx

    @pl.when(i == 0)
    def _():
        k_ref[0:seq, :] = kl_ref[...]
        k_ref[seq:n_keys, :] = kc_ref[...]
        row = lax.broadcasted_iota(jnp.int32, (V_EXTRA_ROWS, n_keys), 0)
        tail = jnp.where(row == 0, 1.0, 0.0).astype(BF16)
        for kv in range(n_kv):
            ks = slice(kv * HEAD_DIM, (kv + 1) * HEAD_DIM)
            vt_ref[kv, 0:HEAD_DIM, 0:seq] = vl_ref[:, ks].T
            vt_ref[kv, 0:HEAD_DIM, seq:n_keys] = vc_ref[:, ks].T
            vt_ref[kv, HEAD_DIM:HEAD_DIM + V_EXTRA_ROWS, :] = tail

    def head_slices(kv):
        return [slice((kv * Q_PER_KV + g) * HEAD_DIM, (kv * Q_PER_KV + g + 1) * HEAD_DIM)
                for g in range(Q_PER_KV)]

    def scores(kv, slot):
        q4 = jnp.concatenate([q_ref[:, hs] for hs in head_slices(kv)], axis=0)
        st_ref[slot] = lax.dot_general(k_ref[:, kv * HEAD_DIM:(kv + 1) * HEAD_DIM], q4, nt,
                                       preferred_element_type=F32)

    def softmax_values(kv, slot, tile_slot):
        m = jnp.max(st_ref[slot], axis=0, keepdims=True)
        pt = jnp.exp2(st_ref[slot] - m).astype(BF16)
        res = jnp.dot(vt_ref[kv], pt, preferred_element_type=F32)
        ot = res[:HEAD_DIM, :] / res[HEAD_DIM:HEAD_DIM + 1, :]
        for g, hs in enumerate(head_slices(kv)):
            att_ref[tile_slot, :, hs] = ot[:, g * tq:(g + 1) * tq].T

    def attend(tile_slot):
        scores(0, 0)
        for kv in range(n_kv):
            if kv + 1 < n_kv:
                scores(kv + 1, (kv + 1) % 2)
            softmax_values(kv, kv % 2, tile_slot)

    def finish(tile_slot):
        att = att_ref[tile_slot]
        inv_rms = lax.rsqrt(jnp.mean(att * att, axis=-1, keepdims=True) + EPS)
        o_ref[...] = (att * inv_rms * (_silu(ga_ref[...]) * nw_ref[...])).astype(BF16)

    @pl.when(i == 0)
    def _():
        attend(0)

    @pl.when((i >= 1) & (i < n_q_tiles))
    def _():
        attend(i % 2)
        finish((i - 1) % 2)

    @pl.when(i == n_q_tiles)
    def _():
        finish((n_q_tiles - 1) % 2)


def _attention(qn, kv, gates, norm_w, b, seq, n_ctx, tq=256):
    n_lat, d_att = qn.shape
    d_kv = kv.shape[1] // 2
    qt = seq // tq
    ctx_blk0 = n_lat // n_ctx
    this = lambda bi, i: (bi * qt + jnp.minimum(i, qt - 1), 0)
    done = lambda bi, i: (bi * qt + jnp.maximum(i - 1, 0), 0)
    return pl.pallas_call(
        functools.partial(_attn_kernel, n_q_tiles=qt),
        grid=(b, qt + 1),
        in_specs=[pl.BlockSpec((tq, d_att), this),
                  pl.BlockSpec((seq, d_kv), lambda bi, i: (bi, 0), pipeline_mode=pl.Buffered(1)),
                  pl.BlockSpec((seq, d_kv), lambda bi, i: (bi, 1), pipeline_mode=pl.Buffered(1)),
                  pl.BlockSpec((n_ctx, d_kv), lambda bi, i: (ctx_blk0 + bi, 0), pipeline_mode=pl.Buffered(1)),
                  pl.BlockSpec((n_ctx, d_kv), lambda bi, i: (ctx_blk0 + bi, 1), pipeline_mode=pl.Buffered(1)),
                  pl.BlockSpec((tq, d_att), done),
                  pl.BlockSpec((1, d_att), lambda bi, i: (0, 0))],
        out_specs=pl.BlockSpec((tq, d_att), done),
        out_shape=jax.ShapeDtypeStruct((n_lat, d_att), BF16),
        scratch_shapes=[pltpu.VMEM((seq + n_ctx, d_kv), BF16),
                        pltpu.VMEM((d_kv // HEAD_DIM, HEAD_DIM + V_EXTRA_ROWS, seq + n_ctx), BF16),
                        pltpu.VMEM((2, seq + n_ctx, Q_PER_KV * tq), F32),
                        pltpu.VMEM((2, tq, d_att), F32)],
        compiler_params=_params("parallel", "arbitrary"),
        name="attention",
    )(qn, kv, kv, kv, kv, gates, norm_w)


LRU_SEGS = SUBLANES
GATE_ROWS = 256
SCAN_STEPS = 4


def _softplus(z):
    return jnp.maximum(z, 0.0) + jnp.log1p(jnp.exp(-jnp.abs(z)))


def _segment_len(total):
    seg = -(-total // LRU_SEGS)
    return seg + (SUBLANES // 2 - seg) % SUBLANES


def _scan8(a, bv, reverse):
    row = lax.broadcasted_iota(jnp.int32, a.shape, 0)
    for s in (1, 2, 4):
        if reverse:
            keep = row < SUBLANES - s
            shift = SUBLANES - s
        else:
            keep = row >= s
            shift = s
        a_prev = jnp.where(keep, pltpu.roll(a, shift, 0), 1.0)
        b_prev = jnp.where(keep, pltpu.roll(bv, shift, 0), 0.0)
        bv = bv + a * b_prev
        a = a * a_prev
    return a, bv


def _lru_kernel(ul_ref, xc_ref, cw_ref, cb_ref, w_ref, bias_ref, lam_ref, gl_ref, nw_ref, y_ref, ss_ref,
                xp_ref, u_ref, a_ref, b_ref, hs_ref, *, seq, n_ctx, seg_len):
    cg = y_ref.shape[1]
    n_blk = cg // LRU_BLOCK_DIM
    total = n_ctx + seq
    padded = LRU_SEGS * seg_len
    n_chunks = total // GATE_ROWS
    gate_unroll = next(u for u in (3, 2, 1) if n_chunks % u == 0)

    @pl.when(pl.program_id(1) == 0)
    def _():
        ss_ref[...] = jnp.zeros(ss_ref.shape, F32)

    for n in range(n_blk):
        a_ref[n, total:padded, :] = jnp.ones((padded - total, LRU_BLOCK_DIM), F32)
        b_ref[n, total:padded, :] = jnp.zeros((padded - total, LRU_BLOCK_DIM), F32)
    zeros_pad = jnp.zeros((CONV_PAD_ROWS, cg), F32)
    halo = CONV_PIECE + 2 * CONV_PAD_ROWS
    half_w = [0.5 * cw_ref[j:j + 1, :] for j in range(CONV_WIDTH)]
    half_b = 0.5 * cb_ref[...]
    xp_ref[0:CONV_PAD_ROWS, :] = zeros_pad
    xp_ref[CONV_PAD_ROWS:CONV_PAD_ROWS + n_ctx, :] = xc_ref[...]
    xp_ref[CONV_PAD_ROWS + n_ctx:2 * CONV_PAD_ROWS + n_ctx, :] = zeros_pad
    for p in range(n_ctx // CONV_PIECE):
        piece = _half_conv_rows(xp_ref[p * CONV_PIECE:p * CONV_PIECE + halo, :], half_w, half_b)
        for u0 in (0, n_ctx + seq):
            u_ref[u0 + p * CONV_PIECE:u0 + (p + 1) * CONV_PIECE, :] = piece
    u_ref[n_ctx:n_ctx + seq, :] = ul_ref[...]

    for d in range(2):
        reverse = d == 1
        seq_row0 = n_ctx if reverse else 0
        half_decay = [0.5 * LRU_C * _softplus(-lam_ref[d:d + 1, n * LRU_BLOCK_DIM:(n + 1) * LRU_BLOCK_DIM])
                      for n in range(n_blk)]

        def gate_body(cb, _, d=d, seq_row0=seq_row0, half_decay=half_decay):
            for cc in range(gate_unroll):
                s0 = pl.multiple_of((cb * gate_unroll + cc) * GATE_ROWS, GATE_ROWS)
                hu = u_ref[pl.ds(pl.multiple_of(seq_row0 + s0, SUBLANES), GATE_ROWS), :]
                hub = hu.astype(BF16)
                for n in range(n_blk):
                    sl = slice(n * LRU_BLOCK_DIM, (n + 1) * LRU_BLOCK_DIM)
                    pre = jnp.dot(hub[:, sl], w_ref[d, n], preferred_element_type=F32) + bias_ref[d, n]
                    t_r = jnp.tanh(pre[:, :LRU_BLOCK_DIM])
                    t_i = jnp.tanh(pre[:, LRU_BLOCK_DIM:])
                    neg_log_a = t_r * half_decay[n] + half_decay[n]
                    a = jnp.exp2(neg_log_a * (-LOG2_E))
                    a_ref[n, pl.ds(s0, GATE_ROWS), :] = a
                    w = jnp.tanh(neg_log_a) * (1.0 + a * a)
                    mult = jnp.where(w > 0.0, w * lax.rsqrt(w), 0.0)
                    b_ref[n, pl.ds(s0, GATE_ROWS), :] = mult * ((t_i + 1.0) * hu[:, sl])
            return 0

        lax.fori_loop(0, n_chunks // gate_unroll, gate_body, 0)

        def load_steps(tb, reverse=reverse):
            rows = []
            base = (seg_len // SCAN_STEPS - 1 - tb) * SCAN_STEPS if reverse else tb * SCAN_STEPS
            for k in range(SCAN_STEPS):
                i = base + (SCAN_STEPS - 1 - k if reverse else k)
                idx = pl.ds(i, LRU_SEGS, stride=seg_len)
                rows.append((idx, [a_ref[n, idx, :] for n in range(n_blk)],
                             [b_ref[n, idx, :] for n in range(n_blk)]))
            return rows

        def local_body(tb, carry):
            hs, decs = list(carry[0]), list(carry[1])
            for _, a, bv in load_steps(tb):
                for n in range(n_blk):
                    hs[n] = a[n] * hs[n] + bv[n]
                    decs[n] = decs[n] * a[n]
            return tuple(hs), tuple(decs)

        zero = tuple(jnp.zeros((LRU_SEGS, LRU_BLOCK_DIM), F32) for _ in range(n_blk))
        one = tuple(jnp.ones((LRU_SEGS, LRU_BLOCK_DIM), F32) for _ in range(n_blk))
        h_end, dec_end = lax.fori_loop(0, seg_len // SCAN_STEPS, local_body, (zero, one))

        starts = []
        for n in range(n_blk):
            _, state = _scan8(dec_end[n], h_end[n], reverse)
            row = lax.broadcasted_iota(jnp.int32, state.shape, 0)
            if reverse:
                starts.append(jnp.where(row < LRU_SEGS - 1, pltpu.roll(state, LRU_SEGS - 1, 0), 0.0))
            else:
                starts.append(jnp.where(row >= 1, pltpu.roll(state, 1, 0), 0.0))

        def final_body(tb, hs):
            hs = list(hs)
            for idx, a, bv in load_steps(tb):
                for n in range(n_blk):
                    hs[n] = a[n] * hs[n] + bv[n]
                    hs_ref[d, n, idx, :] = hs[n]
            return tuple(hs)

        lax.fori_loop(0, seg_len // SCAN_STEPS, final_body, tuple(starts))

    for r0 in range(0, seq, GATE_ROWS):
        rows = slice(r0, r0 + GATE_ROWS)
        lru = jnp.concatenate(
            [hs_ref[0, n, n_ctx + r0:n_ctx + r0 + GATE_ROWS, :] + hs_ref[1, n, r0:r0 + GATE_ROWS, :]
             for n in range(n_blk)], axis=1)
        y_ref[rows, :] = (lru * nw_ref[...] * _silu(gl_ref[rows, :])).astype(BF16)
        ss_ref[rows, :] += jnp.broadcast_to(jnp.sum(lru * lru, axis=-1, keepdims=True), (GATE_ROWS, LANES))


def _lru(hu, xl, gates, conv_w, conv_b, w_cat, bias_cat, lam, norm_w, b, seq, n_ctx, d_att, cg=512):
    n_lat = b * seq
    d_lru = xl.shape[1]
    n_blk = cg // LRU_BLOCK_DIM
    n_groups = d_lru // cg
    seg_len = _segment_len(n_ctx + seq)
    assert n_ctx % GATE_ROWS == 0 and seq % GATE_ROWS == 0 and GATE_ROWS % CONV_PIECE == 0
    assert seg_len % SCAN_STEPS == 0
    kern = functools.partial(_lru_kernel, seq=seq, n_ctx=n_ctx, seg_len=seg_len)
    ctx_blk0 = n_lat // n_ctx
    gl_blk0 = d_att // cg
    scan_rows = LRU_SEGS * seg_len
    return pl.pallas_call(
        kern,
        grid=(b, n_groups),
        in_specs=[pl.BlockSpec((seq, cg), lambda bi, g: (bi, g)),
                  pl.BlockSpec((n_ctx, cg), lambda bi, g: (ctx_blk0 + bi, g)),
                  pl.BlockSpec((CONV_WIDTH, cg), lambda bi, g: (0, g)),
                  pl.BlockSpec((1, cg), lambda bi, g: (0, g)),
                  pl.BlockSpec((2, n_blk, LRU_BLOCK_DIM, 2 * LRU_BLOCK_DIM), lambda bi, g: (0, g, 0, 0)),
                  pl.BlockSpec((2, n_blk, 1, 2 * LRU_BLOCK_DIM), lambda bi, g: (0, g, 0, 0)),
                  pl.BlockSpec((2, cg), lambda bi, g: (0, g)),
                  pl.BlockSpec((seq, cg), lambda bi, g: (bi, gl_blk0 + g)),
                  pl.BlockSpec((1, cg), lambda bi, g: (0, g))],
        out_specs=[pl.BlockSpec((seq, cg), lambda bi, g: (bi, g)),
                   pl.BlockSpec((seq, LANES), lambda bi, g: (bi, 0))],
        out_shape=[jax.ShapeDtypeStruct((n_lat, d_lru), BF16),
                   jax.ShapeDtypeStruct((n_lat, LANES), F32)],
        scratch_shapes=[pltpu.VMEM((n_ctx + 2 * CONV_PAD_ROWS, cg), F32),
                        pltpu.VMEM((2 * n_ctx + seq, cg), F32),
                        pltpu.VMEM((n_blk, scan_rows, LRU_BLOCK_DIM), F32),
                        pltpu.VMEM((n_blk, scan_rows, LRU_BLOCK_DIM), F32),
                        pltpu.VMEM((2, n_blk, scan_rows, LRU_BLOCK_DIM), F32)],
        compiler_params=_params("parallel", "arbitrary"),
        name="rglru",
    )(hu, xl, conv_w, conv_b, w_cat, bias_cat, lam, gates, norm_w)


def _out_kernel(ma_ref, yl_ref, ss_ref, wa_ref, wl_ref, x_ref, gate_ref, o_ref, wab_ref, wlb_ref,
                *, tiles_per_batch, d_lru):
    @pl.when(pl.program_id(1) == 0)
    def _():
        wab_ref[...] = wa_ref[...].astype(BF16)
        wlb_ref[...] = wl_ref[...].astype(BF16)

    bi = pl.program_id(1) // tiles_per_batch
    gate = gate_ref[pl.ds(bi, 1), :]
    sumsq = ss_ref[:, 0:LANES]
    for g in range(1, ss_ref.shape[1] // LANES):
        sumsq = sumsq + ss_ref[:, g * LANES:(g + 1) * LANES]
    inv_rms = lax.rsqrt(sumsq * (1.0 / d_lru) + EPS)
    inv_rms = jnp.concatenate([inv_rms] * (o_ref.shape[1] // LANES), axis=1)
    acc = jnp.dot(ma_ref[...], wab_ref[...], preferred_element_type=F32)
    acc = acc + inv_rms * jnp.dot(yl_ref[...], wlb_ref[...], preferred_element_type=F32)
    o_ref[...] = x_ref[...] + gate * acc


def _out_proj(mix_att, y_lru, ss, w_out, x2, mod, seq, gate_col0):
    m, d_att = mix_att.shape
    d_lru = y_lru.shape[1]
    assert d_att == d_lru
    n = w_out.shape[1]
    tm, tn, w_buffers = _proj_tiles(
        n, d_att + d_lru,
        lambda tm, tn: 2 * tm * (d_att + d_lru) * 2 + 4 * tm * tn * 4 + 2 * tm * LANES * 4)
    assert m % tm == 0 and seq % tm == 0 and gate_col0 % tn == 0
    gb = gate_col0 // tn
    kern = functools.partial(_out_kernel, tiles_per_batch=seq // tm, d_lru=d_lru)
    return pl.pallas_call(
        kern,
        grid=(n // tn, m // tm),
        in_specs=[pl.BlockSpec((tm, d_att), lambda j, i: (i, 0)),
                  pl.BlockSpec((tm, d_lru), lambda j, i: (i, 0)),
                  pl.BlockSpec((tm, ss.shape[1]), lambda j, i: (i, 0)),
                  _weight_spec(d_att, tn, w_buffers, lambda j, i: (0, j)),
                  _weight_spec(d_lru, tn, w_buffers, lambda j, i: (1, j)),
                  pl.BlockSpec((tm, tn), lambda j, i: (i, j)),
                  pl.BlockSpec((MOD_ROWS, tn), lambda j, i: (0, gb + j))],
        out_specs=pl.BlockSpec((tm, tn), lambda j, i: (i, j)),
        out_shape=jax.ShapeDtypeStruct((m, n), F32),
        scratch_shapes=[pltpu.VMEM((d_att, tn), BF16), pltpu.VMEM((d_lru, tn), BF16)],
        compiler_params=_params("parallel", "arbitrary"),
        name="out_proj",
    )(mix_att, y_lru, ss, w_out, w_out, x2, mod)


def _rope_tables(seq):
    pos = jnp.arange(seq)
    row = (pos // GRID_W).astype(F32)
    col = (pos % GRID_W).astype(F32)
    n_freq = HEAD_DIM // 4
    freqs = ROPE_THETA ** (-jnp.arange(n_freq, dtype=F32) / n_freq)
    ang_r = row[:, None] * freqs
    ang_c = col[:, None] * freqs
    cos = jnp.concatenate([jnp.cos(ang_r), jnp.cos(ang_r), jnp.cos(ang_c), jnp.cos(ang_c)], axis=1)
    sin = jnp.concatenate([-jnp.sin(ang_r), jnp.sin(ang_r), -jnp.sin(ang_c), jnp.sin(ang_c)], axis=1)
    return cos, sin


def kernel(x, c, ctx, c_ctx, w_ada, b_ada, norm_w, w_in, q_norm_w, k_norm_w, conv_w, conv_b,
           lru_wa, lru_ba, lru_wx, lru_bx, lru_lambda, out_norm_att, out_norm_lru, w_out):
    assert w_ada.shape[0] == 1, "single-layer kernel: only the latent stream is produced"
    b, seq, d = x.shape
    n_ctx = ctx.shape[1]
    d_att = out_norm_att.shape[1]
    d_lru = out_norm_lru.shape[1]
    d_kv = (w_in.shape[2] - 2 * d_att - 2 * d_lru) // 2
    n_lat = b * seq
    n_all = n_lat + b * n_ctx
    assert b + 1 <= MOD_ROWS
    k_col0, v_col0, ga_col0 = d_att, d_att + d_kv, d_att + 2 * d_kv
    xl_col0 = ga_col0 + d_att
    gl_col0 = xl_col0 + d_lru

    x2 = x.reshape(n_lat, d)
    c2 = ctx.reshape(b * n_ctx, d)

    cvec = jnp.concatenate([c, c_ctx[None, :], jnp.zeros((MOD_ROWS - b - 1, d), F32)], axis=0)
    mod = _modulation(cvec, w_ada[0], b_ada)

    h = _prenorm(x2, c2, norm_w, mod, seq)
    w_in0 = w_in[0]

    cos, sin = _rope_tables(seq)
    qn = _proj_heads("proj_q", h, w_in0, n_lat, 0, d_att, 0, cos, sin, q_norm_w, seq, n_lat,
                     HEAD_DIM ** -0.5 * LOG2_E)
    assert v_col0 == k_col0 + d_kv
    kv = _proj_heads("proj_kv", h, w_in0, n_all, k_col0, 2 * d_kv, d_kv, cos, sin, k_norm_w, seq, n_lat, 1.0)
    xl = _proj_plain("proj_xl", h, w_in0, n_all, [(xl_col0, d_lru)], F32)
    gates, hu = _proj_conv_side("proj_gates", h, w_in0, n_lat, [(ga_col0, d_att), (gl_col0, d_lru)],
                                xl, conv_w[0], conv_b, seq)

    mix_att = _attention(qn, kv, gates, out_norm_att, b, seq, n_ctx)

    n_blocks = d_lru // LRU_BLOCK_DIM
    w_cat = jnp.concatenate([lru_wa[0], lru_wx[0]], axis=-1).astype(BF16)
    bias_cat = 0.5 * jnp.concatenate([lru_ba[0].reshape(2, n_blocks, 1, LRU_BLOCK_DIM),
                                      lru_bx[0].reshape(2, n_blocks, 1, LRU_BLOCK_DIM)], axis=-1)
    y_lru, ss = _lru(hu, xl, gates, conv_w[0], conv_b, w_cat, bias_cat, lru_lambda[0], out_norm_lru,
                     b, seq, n_ctx, d_att)

    out = _out_proj(mix_att, y_lru, ss, w_out[0], x2, mod, seq, 2 * d)
    return out.reshape(b, seq, d)
```

```python
import functools
import math

import jax
import jax.numpy as jnp
from jax import lax
from jax.experimental import pallas as pl
from jax.experimental.pallas import tpu as pltpu

F32 = jnp.float32
BF16 = jnp.bfloat16

HEAD_DIM = 128
GRID_W = 64
Q_PER_KV = 4
LRU_BLOCK_DIM = 128
CONV_WIDTH = 4
CONV_LEFT = 2
LRU_C = 8.0
ROPE_THETA = 10000.0
EPS = 1e-6
LOG2_E = 1.4426950408889634

LANES = 128
SUBLANES = 8
MXU_WIDTH = 256
MOD_ROWS = 8
V7X_VMEM_BYTES = 64 * 1024 * 1024
VMEM_LIMIT = V7X_VMEM_BYTES - 2 * 1024 * 1024
MATMUL_TEMP_BYTES = 4 * 1024 * 1024


def _params(*sem):
    return pltpu.CompilerParams(dimension_semantics=sem, vmem_limit_bytes=VMEM_LIMIT)


def _silu(x):
    half = 0.5 * x
    return half * jnp.tanh(half) + half


def _rms(x, w):
    ms = jnp.mean(x * x, axis=-1, keepdims=True)
    return x * lax.rsqrt(ms + EPS) * w


def _mod_kernel(c_ref, w_ref, b_ref, o_ref):
    s = _silu(c_ref[...]).astype(BF16)
    o_ref[...] = jnp.dot(s, w_ref[...].astype(BF16), preferred_element_type=F32) + b_ref[...]


def _modulation(cvec, w_ada, b_ada, tn=512):
    d, n = w_ada.shape
    return pl.pallas_call(
        _mod_kernel,
        grid=(n // tn,),
        in_specs=[pl.BlockSpec((MOD_ROWS, d), lambda j: (0, 0)),
                  pl.BlockSpec((d, tn), lambda j: (0, j)),
                  pl.BlockSpec((1, tn), lambda j: (0, j))],
        out_specs=pl.BlockSpec((MOD_ROWS, tn), lambda j: (0, j)),
        out_shape=jax.ShapeDtypeStruct((MOD_ROWS, n), F32),
        compiler_params=_params("parallel"),
        name="modulation",
    )(cvec, w_ada, b_ada)


def _prenorm_kernel(x_ref, c_ref, nw_ref, mod_ref, o_ref, *, d, n_x_tiles, tiles_per_batch, ctx_row):
    t = pl.program_id(0)

    def emit(src_ref, row):
        x = src_ref[...]
        shift = mod_ref[pl.ds(row, 1), 0:d]
        gain = nw_ref[...] * (1.0 + mod_ref[pl.ds(row, 1), d:2 * d])
        inv_rms = lax.rsqrt(jnp.mean(x * x, axis=-1, keepdims=True) + EPS)
        o_ref[...] = (x * inv_rms * gain + shift).astype(BF16)

    @pl.when(t < n_x_tiles)
    def _():
        emit(x_ref, t // tiles_per_batch)

    @pl.when(t >= n_x_tiles)
    def _():
        emit(c_ref, ctx_row)


def _prenorm(x2, c2, norm_w, mod, seq, tr=512):
    mx, d = x2.shape
    mc = c2.shape[0]
    nx, nc = mx // tr, mc // tr
    kern = functools.partial(_prenorm_kernel, d=d, n_x_tiles=nx, tiles_per_batch=seq // tr,
                             ctx_row=mx // seq)
    return pl.pallas_call(
        kern,
        grid=(nx + nc,),
        in_specs=[pl.BlockSpec((tr, d), lambda t: (jnp.minimum(t, nx - 1), 0)),
                  pl.BlockSpec((tr, d), lambda t: (jnp.maximum(t - nx, 0), 0)),
                  pl.BlockSpec((1, d), lambda t: (0, 0)),
                  pl.BlockSpec(mod.shape, lambda t: (0, 0))],
        out_specs=pl.BlockSpec((tr, d), lambda t: (t, 0)),
        out_shape=jax.ShapeDtypeStruct((mx + mc, d), BF16),
        compiler_params=_params("parallel"),
        name="prenorm",
    )(x2, c2, norm_w, mod)


WIDE_TN = 1024
NARROW_TN = 512


def _proj_tiles(n_cols, k, resident_bytes):
    shapes = [(1024, NARROW_TN)]
    if n_cols % WIDE_TN == 0:
        shapes.insert(0, (512, WIDE_TN))
    for tm, tn in shapes:
        if resident_bytes(tm, tn) + k * tn * 2 + 2 * k * tn * 4 + MATMUL_TEMP_BYTES <= VMEM_LIMIT:
            return tm, tn, 2
    return (*shapes[0], 1)


def _weight_spec(k, tn, buffers, index_map):
    if buffers == 1:
        return pl.BlockSpec((k, tn), index_map, pipeline_mode=pl.Buffered(1))
    return pl.BlockSpec((k, tn), index_map)


def _cast_weights_once(w_ref, wb_ref):
    @pl.when(pl.program_id(1) == 0)
    def _():
        wb_ref[...] = w_ref[...].astype(BF16)


def _rope(y, cos, sin_signed):
    lane = lax.broadcasted_iota(jnp.int32, y.shape, 1)
    first_half = (lane & (HEAD_DIM // 4)) == 0
    partner = jnp.where(first_half, pltpu.roll(y, HEAD_DIM - HEAD_DIM // 4, 1),
                        pltpu.roll(y, HEAD_DIM // 4, 1))
    return y * cos + partner * sin_signed


def _proj_heads_kernel(h_ref, w_ref, cos_ref, sin_ref, nw_ref, o_ref, wb_ref, acc0_ref, acc1_ref,
                       *, n_row_tiles, head_cols, out_scale):
    i = pl.program_id(1)
    _cast_weights_once(w_ref, wb_ref)
    accs = (acc0_ref, acc1_ref)

    def matmul_into(acc_ref):
        acc_ref[...] = jnp.dot(h_ref[...], wb_ref[...], preferred_element_type=F32)

    def finish_from(acc_ref):
        norm_w = nw_ref[...] * out_scale
        for hh in range(head_cols // HEAD_DIM):
            sl = slice(hh * HEAD_DIM, (hh + 1) * HEAD_DIM)
            o_ref[:, sl] = _rope(_rms(acc_ref[:, sl], norm_w), cos_ref[...], sin_ref[...]).astype(BF16)
        if head_cols < o_ref.shape[1]:
            o_ref[:, head_cols:] = acc_ref[:, head_cols:].astype(BF16)

    @pl.when(i == 0)
    def _():
        matmul_into(accs[0])

    for parity in range(2):
        @pl.when((i >= 1) & (i < n_row_tiles) & (i % 2 == parity))
        def _(parity=parity):
            matmul_into(accs[parity])
            finish_from(accs[1 - parity])

    @pl.when(i == n_row_tiles)
    def _():
        finish_from(accs[(n_row_tiles - 1) % 2])


def _proj_heads(name, h, w_in, n_rows, col0, n_cols, plain_cols, cos, sin, norm_w, seq, rope_rows, out_scale):
    k = h.shape[1]
    tm, tn, w_buffers = _proj_tiles(
        n_cols, k, lambda tm, tn: 2 * tm * k * 2 + 2 * tm * tn * 4 + 2 * tm * tn * 2 + 4 * tm * HEAD_DIM * 4)
    assert n_rows % tm == 0 and seq % tm == 0 and rope_rows % tm == 0 and col0 % tn == 0
    assert plain_cols == 0 or n_cols == tn
    head_cols = tn - plain_cols
    n_row_tiles = n_rows // tm
    rope_tiles = rope_rows // tm
    spt = seq // tm
    cos = jnp.concatenate([cos, jnp.ones((tm, HEAD_DIM), F32)], axis=0)
    sin = jnp.concatenate([sin, jnp.zeros((tm, HEAD_DIM), F32)], axis=0)
    kern = functools.partial(_proj_heads_kernel, n_row_tiles=n_row_tiles, head_cols=head_cols,
                             out_scale=out_scale)
    done = lambda i: jnp.maximum(i - 1, 0)
    table = lambda j, i: (jnp.where(done(i) < rope_tiles, done(i) % spt, spt), 0)
    return pl.pallas_call(
        kern,
        grid=(n_cols // tn, n_row_tiles + 1),
        in_specs=[pl.BlockSpec((tm, k), lambda j, i: (jnp.minimum(i, n_row_tiles - 1), 0)),
                  _weight_spec(k, tn, w_buffers, lambda j, i: (0, col0 // tn + j)),
                  pl.BlockSpec((tm, HEAD_DIM), table),
                  pl.BlockSpec((tm, HEAD_DIM), table),
                  pl.BlockSpec((1, HEAD_DIM), lambda j, i: (0, 0))],
        out_specs=pl.BlockSpec((tm, tn), lambda j, i: (done(i), j)),
        out_shape=jax.ShapeDtypeStruct((n_rows, n_cols), BF16),
        scratch_shapes=[pltpu.VMEM((k, tn), BF16),
                        pltpu.VMEM((tm, tn), F32),
                        pltpu.VMEM((tm, tn), F32)],
        compiler_params=_params("parallel", "arbitrary"),
        name=name,
    )(h, w_in, cos, sin, norm_w)


CONV_PAD_ROWS = 8
CONV_PIECE = 32


def _half_conv_rows(window, half_w, half_b):
    halo = window.shape[0]
    acc = half_b
    for j in range(CONV_WIDTH):
        off = j - CONV_LEFT
        shifted = window if off == 0 else pltpu.roll(window, (-off) % halo, 0)
        acc = acc + half_w[j] * shifted[CONV_PAD_ROWS:CONV_PAD_ROWS + CONV_PIECE, :]
    return acc


def _proj_conv_side_kernel(h_ref, w_ref, x_ref, top_ref, bot_ref, cw_ref, cb_ref, o_ref, u_ref, wb_ref,
                           *, tiles_per_seq):
    i = pl.program_id(1)
    _cast_weights_once(w_ref, wb_ref)
    o_ref[...] = jnp.dot(h_ref[...], wb_ref[...], preferred_element_type=F32)
    tm, cw = x_ref.shape
    half_w = [0.5 * cw_ref[j:j + 1, :] for j in range(CONV_WIDTH)]
    half_b = 0.5 * cb_ref[...]
    in_seq = i % tiles_per_seq
    top = jnp.where(in_seq == 0, 0.0, top_ref[...])
    bot = jnp.where(in_seq == tiles_per_seq - 1, 0.0, bot_ref[...])
    for p in range(tm // CONV_PIECE):
        lo, hi = p * CONV_PIECE - CONV_PAD_ROWS, (p + 1) * CONV_PIECE + CONV_PAD_ROWS
        parts = [x_ref[max(lo, 0):min(hi, tm), :]]
        if lo < 0:
            parts.insert(0, top)
        if hi > tm:
            parts.append(bot)
        window = parts[0] if len(parts) == 1 else jnp.concatenate(parts, axis=0)
        conv = _half_conv_rows(window, half_w, half_b)
        conv = pltpu.roll(pltpu.roll(conv, LANES // 2, 1), cw - LANES // 2, 1)
        u_ref[p * CONV_PIECE:(p + 1) * CONV_PIECE, :] = conv


def _proj_conv_side(name, h, w_in, n_rows, col_ranges, x, conv_w, conv_b, seq):
    k = h.shape[1]
    d_x = x.shape[1]
    n_cols = sum(width for _, width in col_ranges)
    conv_cols = lambda tn: d_x * tn // n_cols
    tm, tn, w_buffers = _proj_tiles(
        math.gcd(*[c for r in col_ranges for c in r]), k,
        lambda tm, tn: 2 * tm * k * 2 + 2 * tm * tn * 4 + 4 * tm * conv_cols(tn) * 4)
    tiles = [(start // tn, width // tn) for start, width in col_ranges]
    n_col_tiles = n_cols // tn
    cw = conv_cols(tn)
    assert n_rows % tm == 0 and seq % tm == 0 and tm % CONV_PIECE == 0 and cw * n_col_tiles == d_x
    assert x.shape[0] >= n_rows + CONV_PAD_ROWS
    halo_blocks = tm // CONV_PAD_ROWS

    def col_tile(j):
        first, idx = 0, tiles[0][0] + j
        for r in range(1, len(tiles)):
            first += tiles[r - 1][1]
            idx = jnp.where(j >= first, tiles[r][0] + j - first, idx)
        return idx

    kern = functools.partial(_proj_conv_side_kernel, tiles_per_seq=seq // tm)
    return pl.pallas_call(
        kern,
        grid=(n_col_tiles, n_rows // tm),
        in_specs=[pl.BlockSpec((tm, k), lambda j, i: (i, 0)),
                  _weight_spec(k, tn, w_buffers, lambda j, i: (0, col_tile(j))),
                  pl.BlockSpec((tm, cw), lambda j, i: (i, j)),
                  pl.BlockSpec((CONV_PAD_ROWS, cw), lambda j, i: (jnp.maximum(i * halo_blocks - 1, 0), j)),
                  pl.BlockSpec((CONV_PAD_ROWS, cw), lambda j, i: ((i + 1) * halo_blocks, j)),
                  pl.BlockSpec((CONV_WIDTH, cw), lambda j, i: (0, j)),
                  pl.BlockSpec((1, cw), lambda j, i: (0, j))],
        out_specs=[pl.BlockSpec((tm, tn), lambda j, i: (i, j)),
                   pl.BlockSpec((tm, cw), lambda j, i: (i, j))],
        out_shape=[jax.ShapeDtypeStruct((n_rows, n_cols), F32),
                   jax.ShapeDtypeStruct((n_rows, d_x), F32)],
        scratch_shapes=[pltpu.VMEM((k, tn), BF16)],
        compiler_params=_params("parallel", "arbitrary"),
        name=name,
    )(h, w_in, x, x, x, conv_w, conv_b)


def _proj_plain_kernel(h_ref, w_ref, o_ref, wb_ref):
    _cast_weights_once(w_ref, wb_ref)
    o_ref[...] = jnp.dot(h_ref[...], wb_ref[...], preferred_element_type=F32).astype(o_ref.dtype)


def _proj_plain(name, h, w_in, n_rows, col_ranges, out_dtype):
    k = h.shape[1]
    out_bytes = jnp.dtype(out_dtype).itemsize
    tm, tn, w_buffers = _proj_tiles(math.gcd(*[c for r in col_ranges for c in r]), k,
                                    lambda tm, tn: 2 * tm * k * 2 + 2 * tm * tn * out_bytes)
    assert n_rows % tm == 0
    tiles = [(start // tn, width // tn) for start, width in col_ranges]

    def col_tile(j):
        first, idx = 0, tiles[0][0] + j
        for r in range(1, len(tiles)):
            first += tiles[r - 1][1]
            idx = jnp.where(j >= first, tiles[r][0] + j - first, idx)
        return idx

    n_col_tiles = sum(n for _, n in tiles)
    return pl.pallas_call(
        _proj_plain_kernel,
        grid=(n_col_tiles, n_rows // tm),
        in_specs=[pl.BlockSpec((tm, k), lambda j, i: (i, 0)),
                  _weight_spec(k, tn, w_buffers, lambda j, i: (0, col_tile(j)))],
        out_specs=pl.BlockSpec((tm, tn), lambda j, i: (i, j)),
        out_shape=jax.ShapeDtypeStruct((n_rows, n_col_tiles * tn), out_dtype),
        scratch_shapes=[pltpu.VMEM((k, tn), BF16)],
        compiler_params=_params("parallel", "arbitrary"),
        name=name,
    )(h, w_in)


V_EXTRA_ROWS = 16


def _attn_kernel(q_ref, kl_ref, vl_ref, kc_ref, vc_ref, ga_ref, nw_ref, o_ref,
                 k_ref, vt_ref, st_ref, att_ref, *, n_q_tiles):
    tq = q_ref.shape[0]
    seq, n_ctx = kl_ref.shape[0], kc_ref.shape[0]
    n_kv = kl_ref.shape[1] // HEAD_DIM
    nt = (((1,), (1,)), ((), ()))
    i = pl.program_id(1)
    n_keys = seq + n_ctx

    @pl.when(i == 0)
    def _():
        k_ref[0:seq, :] = kl_ref[...]
        k_ref[seq:n_keys, :] = kc_ref[...]
        row = lax.broadcasted_iota(jnp.int32, (V_EXTRA_ROWS, n_keys), 0)
        tail = jnp.where(row == 0, 1.0, 0.0).astype(BF16)
        for kv in range(n_kv):
            ks = slice(kv * HEAD_DIM, (kv + 1) * HEAD_DIM)
            vt_ref[kv, 0:HEAD_DIM, 0:seq] = vl_ref[:, ks].T
            vt_ref[kv, 0:HEAD_DIM, seq:n_keys] = vc_ref[:, ks].T
            vt_ref[kv, HEAD_DIM:HEAD_DIM + V_EXTRA_ROWS, :] = tail

    def head_slices(kv):
        return [slice((kv * Q_PER_KV + g) * HEAD_DIM, (kv * Q_PER_KV + g + 1) * HEAD_DIM)
                for g in range(Q_PER_KV)]

    def scores(kv, slot):
        q4 = jnp.concatenate([q_ref[:, hs] for hs in head_slices(kv)], axis=0)
        st_ref[slot] = lax.dot_general(k_ref[:, kv * HEAD_DIM:(kv + 1) * HEAD_DIM], q4, nt,
                                       preferred_element_type=F32)

    def softmax_values(kv, slot, tile_slot):
        m = jnp.max(st_ref[slot], axis=0, keepdims=True)
        pt = jnp.exp2(st_ref[slot] - m).astype(BF16)
        res = jnp.dot(vt_ref[kv], pt, preferred_element_type=F32)
        ot = res[:HEAD_DIM, :] / res[HEAD_DIM:HEAD_DIM + 1, :]
        for g, hs in enumerate(head_slices(kv)):
            att_ref[tile_slot, :, hs] = ot[:, g * tq:(g + 1) * tq].T

    def attend(tile_slot):
        scores(0, 0)
        for kv in range(n_kv):
            if kv + 1 < n_kv:
                scores(kv + 1, (kv + 1) % 2)
            softmax_values(kv, kv % 2, tile_slot)

    def finish(tile_slot):
        att = att_ref[tile_slot]
        inv_rms = lax.rsqrt(jnp.mean(att * att, axis=-1, keepdims=True) + EPS)
        o_ref[...] = (att * inv_rms * (_silu(ga_ref[...]) * nw_ref[...])).astype(BF16)

    @pl.when(i == 0)
    def _():
        attend(0)

    @pl.when((i >= 1) & (i < n_q_tiles))
    def _():
        attend(i % 2)
        finish((i - 1) % 2)

    @pl.when(i == n_q_tiles)
    def _():
        finish((n_q_tiles - 1) % 2)


def _attention(qn, kv, gates, norm_w, b, seq, n_ctx, tq=256):
    n_lat, d_att = qn.shape
    d_kv = kv.shape[1] // 2
    qt = seq // tq
    ctx_blk0 = n_lat // n_ctx
    this = lambda bi, i: (bi * qt + jnp.minimum(i, qt - 1), 0)
    done = lambda bi, i: (bi * qt + jnp.maximum(i - 1, 0), 0)
    return pl.pallas_call(
        functools.partial(_attn_kernel, n_q_tiles=qt),
        grid=(b, qt + 1),
        in_specs=[pl.BlockSpec((tq, d_att), this),
                  pl.BlockSpec((seq, d_kv), lambda bi, i: (bi, 0), pipeline_mode=pl.Buffered(1)),
                  pl.BlockSpec((seq, d_kv), lambda bi, i: (bi, 1), pipeline_mode=pl.Buffered(1)),
                  pl.BlockSpec((n_ctx, d_kv), lambda bi, i: (ctx_blk0 + bi, 0), pipeline_mode=pl.Buffered(1)),
                  pl.BlockSpec((n_ctx, d_kv), lambda bi, i: (ctx_blk0 + bi, 1), pipeline_mode=pl.Buffered(1)),
                  pl.BlockSpec((tq, d_att), done),
                  pl.BlockSpec((1, d_att), lambda bi, i: (0, 0))],
        out_specs=pl.BlockSpec((tq, d_att), done),
        out_shape=jax.ShapeDtypeStruct((n_lat, d_att), BF16),
        scratch_shapes=[pltpu.VMEM((seq + n_ctx, d_kv), BF16),
                        pltpu.VMEM((d_kv // HEAD_DIM, HEAD_DIM + V_EXTRA_ROWS, seq + n_ctx), BF16),
                        pltpu.VMEM((2, seq + n_ctx, Q_PER_KV * tq), F32),
                        pltpu.VMEM((2, tq, d_att), F32)],
        compiler_params=_params("parallel", "arbitrary"),
        name="attention",
    )(qn, kv, kv, kv, kv, gates, norm_w)


LRU_SEGS = SUBLANES
GATE_ROWS = 256
SCAN_STEPS = 4
FINAL_STEPS = 2


def _softplus(z):
    return jnp.maximum(z, 0.0) + jnp.log1p(jnp.exp(-jnp.abs(z)))


def _segment_len(total):
    seg = -(-total // LRU_SEGS)
    return seg + (SUBLANES // 2 - seg) % SUBLANES


def _scan8(a, bv, reverse):
    row = lax.broadcasted_iota(jnp.int32, a.shape, 0)
    for s in (1, 2, 4):
        if reverse:
            keep = row < SUBLANES - s
            shift = SUBLANES - s
        else:
            keep = row >= s
            shift = s
        a_prev = jnp.where(keep, pltpu.roll(a, shift, 0), 1.0)
        b_prev = jnp.where(keep, pltpu.roll(bv, shift, 0), 0.0)
        bv = bv + a * b_prev
        a = a * a_prev
    return a, bv


def _lru_kernel(ul_ref, xc_ref, cw_ref, cb_ref, w_ref, bias_ref, lam_ref, gl_ref, nw_ref, y_ref, ss_ref,
                xp_ref, u_ref, a_ref, b_ref, hs_ref, *, seq, n_ctx, seg_len):
    cg = y_ref.shape[1]
    n_blk = cg // LRU_BLOCK_DIM
    total = n_ctx + seq
    padded = LRU_SEGS * seg_len
    n_chunks = total // GATE_ROWS
    gate_unroll = next(u for u in (3, 2, 1) if n_chunks % u == 0)

    @pl.when(pl.program_id(1) == 0)
    def _():
        ss_ref[...] = jnp.zeros(ss_ref.shape, F32)

    for d in range(2):
        for n in range(n_blk):
            a_ref[d, n, total:padded, :] = jnp.ones((padded - total, LRU_BLOCK_DIM), F32)
            b_ref[d, n, total:padded, :] = jnp.zeros((padded - total, LRU_BLOCK_DIM), F32)
    zeros_pad = jnp.zeros((CONV_PAD_ROWS, cg), F32)
    halo = CONV_PIECE + 2 * CONV_PAD_ROWS
    half_w = [0.5 * cw_ref[j:j + 1, :] for j in range(CONV_WIDTH)]
    half_b = 0.5 * cb_ref[...]
    xp_ref[0:CONV_PAD_ROWS, :] = zeros_pad
    xp_ref[CONV_PAD_ROWS:CONV_PAD_ROWS + n_ctx, :] = xc_ref[...]
    xp_ref[CONV_PAD_ROWS + n_ctx:2 * CONV_PAD_ROWS + n_ctx, :] = zeros_pad
    for p in range(n_ctx // CONV_PIECE):
        piece = _half_conv_rows(xp_ref[p * CONV_PIECE:p * CONV_PIECE + halo, :], half_w, half_b)
        for u0 in (0, n_ctx + seq):
            u_ref[u0 + p * CONV_PIECE:u0 + (p + 1) * CONV_PIECE, :] = piece
    u_ref[n_ctx:n_ctx + seq, :] = ul_ref[...]

    for d in range(2):
        reverse = d == 1
        seq_row0 = n_ctx if reverse else 0
        half_decay = [0.5 * LRU_C * _softplus(-lam_ref[d:d + 1, n * LRU_BLOCK_DIM:(n + 1) * LRU_BLOCK_DIM])
                      for n in range(n_blk)]

        def gate_body(cb, _, d=d, seq_row0=seq_row0, half_decay=half_decay):
            for cc in range(gate_unroll):
                s0 = pl.multiple_of((cb * gate_unroll + cc) * GATE_ROWS, GATE_ROWS)
                hu = u_ref[pl.ds(pl.multiple_of(seq_row0 + s0, SUBLANES), GATE_ROWS), :]
                hub = hu.astype(BF16)
                for n in range(n_blk):
                    sl = slice(n * LRU_BLOCK_DIM, (n + 1) * LRU_BLOCK_DIM)
                    pre = jnp.dot(hub[:, sl], w_ref[d, n], preferred_element_type=F32) + bias_ref[d, n]
                    t_r = jnp.tanh(pre[:, :LRU_BLOCK_DIM])
                    t_i = jnp.tanh(pre[:, LRU_BLOCK_DIM:])
                    neg_log_a = t_r * half_decay[n] + half_decay[n]
                    a = jnp.exp2(neg_log_a * (-LOG2_E))
                    a_ref[d, n, pl.ds(s0, GATE_ROWS), :] = a
                    w = jnp.tanh(neg_log_a) * (1.0 + a * a)
                    mult = jnp.where(w > 0.0, w * lax.rsqrt(w), 0.0)
                    b_ref[d, n, pl.ds(s0, GATE_ROWS), :] = mult * ((t_i + 1.0) * hu[:, sl])
            return 0

        lax.fori_loop(0, n_chunks // gate_unroll, gate_body, 0)

    def step_rows(tb, k, steps, d):
        i = (seg_len // steps - 1 - tb) * steps + steps - 1 - k if d == 1 else tb * steps + k
        return pl.ds(i, LRU_SEGS, stride=seg_len)

    def local_body(tb, carry):
        hs, decs = [list(c) for c in carry[0]], [list(c) for c in carry[1]]
        for k in range(SCAN_STEPS):
            for d in range(2):
                idx = step_rows(tb, k, SCAN_STEPS, d)
                for n in range(n_blk):
                    a = a_ref[d, n, idx, :]
                    hs[d][n] = a * hs[d][n] + b_ref[d, n, idx, :]
                    decs[d][n] = decs[d][n] * a
        return tuple(map(tuple, hs)), tuple(map(tuple, decs))

    zero = tuple(tuple(jnp.zeros((LRU_SEGS, LRU_BLOCK_DIM), F32) for _ in range(n_blk)) for _ in range(2))
    one = tuple(tuple(jnp.ones((LRU_SEGS, LRU_BLOCK_DIM), F32) for _ in range(n_blk)) for _ in range(2))
    h_end, dec_end = lax.fori_loop(0, seg_len // SCAN_STEPS, local_body, (zero, one))

    starts = []
    for d in range(2):
        starts.append([])
        for n in range(n_blk):
            _, state = _scan8(dec_end[d][n], h_end[d][n], d == 1)
            row = lax.broadcasted_iota(jnp.int32, state.shape, 0)
            if d == 1:
                starts[d].append(jnp.where(row < LRU_SEGS - 1, pltpu.roll(state, LRU_SEGS - 1, 0), 0.0))
            else:
                starts[d].append(jnp.where(row >= 1, pltpu.roll(state, 1, 0), 0.0))

    def final_body(tb, hs):
        hs = [list(h) for h in hs]
        loaded = [(d, idx, [a_ref[d, n, idx, :] for n in range(n_blk)], [b_ref[d, n, idx, :] for n in range(n_blk)])
                  for k in range(FINAL_STEPS) for d in range(2)
                  for idx in [step_rows(tb, k, FINAL_STEPS, d)]]
        for d, idx, a, bv in loaded:
            for n in range(n_blk):
                hs[d][n] = a[n] * hs[d][n] + bv[n]
                hs_ref[d, n, idx, :] = hs[d][n]
        return tuple(map(tuple, hs))

    lax.fori_loop(0, seg_len // FINAL_STEPS, final_body, tuple(map(tuple, starts)))

    for r0 in range(0, seq, GATE_ROWS):
        rows = slice(r0, r0 + GATE_ROWS)
        lru = jnp.concatenate(
            [hs_ref[0, n, n_ctx + r0:n_ctx + r0 + GATE_ROWS, :] + hs_ref[1, n, r0:r0 + GATE_ROWS, :]
             for n in range(n_blk)], axis=1)
        y_ref[rows, :] = (lru * nw_ref[...] * _silu(gl_ref[rows, :])).astype(BF16)
        ss_ref[rows, :] += jnp.broadcast_to(jnp.sum(lru * lru, axis=-1, keepdims=True), (GATE_ROWS, LANES))


def _lru(hu, xl, gates, conv_w, conv_b, w_cat, bias_cat, lam, norm_w, b, seq, n_ctx, d_att, cg=512):
    n_lat = b * seq
    d_lru = xl.shape[1]
    n_blk = cg // LRU_BLOCK_DIM
    n_groups = d_lru // cg
    seg_len = _segment_len(n_ctx + seq)
    assert n_ctx % GATE_ROWS == 0 and seq % GATE_ROWS == 0 and GATE_ROWS % CONV_PIECE == 0
    assert seg_len % SCAN_STEPS == 0 and seg_len % FINAL_STEPS == 0
    kern = functools.partial(_lru_kernel, seq=seq, n_ctx=n_ctx, seg_len=seg_len)
    ctx_blk0 = n_lat // n_ctx
    gl_blk0 = d_att // cg
    scan_rows = LRU_SEGS * seg_len
    return pl.pallas_call(
        kern,
        grid=(b, n_groups),
        in_specs=[pl.BlockSpec((seq, cg), lambda bi, g: (bi, g)),
                  pl.BlockSpec((n_ctx, cg), lambda bi, g: (ctx_blk0 + bi, g)),
                  pl.BlockSpec((CONV_WIDTH, cg), lambda bi, g: (0, g)),
                  pl.BlockSpec((1, cg), lambda bi, g: (0, g)),
                  pl.BlockSpec((2, n_blk, LRU_BLOCK_DIM, 2 * LRU_BLOCK_DIM), lambda bi, g: (0, g, 0, 0)),
                  pl.BlockSpec((2, n_blk, 1, 2 * LRU_BLOCK_DIM), lambda bi, g: (0, g, 0, 0)),
                  pl.BlockSpec((2, cg), lambda bi, g: (0, g)),
                  pl.BlockSpec((seq, cg), lambda bi, g: (bi, gl_blk0 + g)),
                  pl.BlockSpec((1, cg), lambda bi, g: (0, g))],
        out_specs=[pl.BlockSpec((seq, cg), lambda bi, g: (bi, g)),
                   pl.BlockSpec((seq, LANES), lambda bi, g: (bi, 0))],
        out_shape=[jax.ShapeDtypeStruct((n_lat, d_lru), BF16),
                   jax.ShapeDtypeStruct((n_lat, LANES), F32)],
        scratch_shapes=[pltpu.VMEM((n_ctx + 2 * CONV_PAD_ROWS, cg), F32),
                        pltpu.VMEM((2 * n_ctx + seq, cg), F32),
                        pltpu.VMEM((2, n_blk, scan_rows, LRU_BLOCK_DIM), F32),
                        pltpu.VMEM((2, n_blk, scan_rows, LRU_BLOCK_DIM), F32),
                        pltpu.VMEM((2, n_blk, scan_rows, LRU_BLOCK_DIM), F32)],
        compiler_params=_params("parallel", "arbitrary"),
        name="rglru",
    )(hu, xl, conv_w, conv_b, w_cat, bias_cat, lam, gates, norm_w)


def _out_kernel(ma_ref, yl_ref, ss_ref, wa_ref, wl_ref, x_ref, gate_ref, o_ref, wab_ref, wlb_ref,
                *, tiles_per_batch, d_lru):
    @pl.when(pl.program_id(1) == 0)
    def _():
        wab_ref[...] = wa_ref[...].astype(BF16)
        wlb_ref[...] = wl_ref[...].astype(BF16)

    bi = pl.program_id(1) // tiles_per_batch
    gate = gate_ref[pl.ds(bi, 1), :]
    sumsq = ss_ref[:, 0:LANES]
    for g in range(1, ss_ref.shape[1] // LANES):
        sumsq = sumsq + ss_ref[:, g * LANES:(g + 1) * LANES]
    inv_rms = lax.rsqrt(sumsq * (1.0 / d_lru) + EPS)
    inv_rms = jnp.concatenate([inv_rms] * (o_ref.shape[1] // LANES), axis=1)
    acc = jnp.dot(ma_ref[...], wab_ref[...], preferred_element_type=F32)
    acc = acc + inv_rms * jnp.dot(yl_ref[...], wlb_ref[...], preferred_element_type=F32)
    o_ref[...] = x_ref[...] + gate * acc


def _out_proj(mix_att, y_lru, ss, w_out, x2, mod, seq, gate_col0):
    m, d_att = mix_att.shape
    d_lru = y_lru.shape[1]
    assert d_att == d_lru
    n = w_out.shape[1]
    tm, tn, w_buffers = _proj_tiles(
        n, d_att + d_lru,
        lambda tm, tn: 2 * tm * (d_att + d_lru) * 2 + 4 * tm * tn * 4 + 2 * tm * LANES * 4)
    assert m % tm == 0 and seq % tm == 0 and gate_col0 % tn == 0
    gb = gate_col0 // tn
    kern = functools.partial(_out_kernel, tiles_per_batch=seq // tm, d_lru=d_lru)
    return pl.pallas_call(
        kern,
        grid=(n // tn, m // tm),
        in_specs=[pl.BlockSpec((tm, d_att), lambda j, i: (i, 0)),
                  pl.BlockSpec((tm, d_lru), lambda j, i: (i, 0)),
                  pl.BlockSpec((tm, ss.shape[1]), lambda j, i: (i, 0)),
                  _weight_spec(d_att, tn, w_buffers, lambda j, i: (0, j)),
                  _weight_spec(d_lru, tn, w_buffers, lambda j, i: (1, j)),
                  pl.BlockSpec((tm, tn), lambda j, i: (i, j)),
                  pl.BlockSpec((MOD_ROWS, tn), lambda j, i: (0, gb + j))],
        out_specs=pl.BlockSpec((tm, tn), lambda j, i: (i, j)),
        out_shape=jax.ShapeDtypeStruct((m, n), F32),
        scratch_shapes=[pltpu.VMEM((d_att, tn), BF16), pltpu.VMEM((d_lru, tn), BF16)],
        compiler_params=_params("parallel", "arbitrary"),
        name="out_proj",
    )(mix_att, y_lru, ss, w_out, w_out, x2, mod)


def _rope_tables(seq):
    pos = jnp.arange(seq)
    row = (pos // GRID_W).astype(F32)
    col = (pos % GRID_W).astype(F32)
    n_freq = HEAD_DIM // 4
    freqs = ROPE_THETA ** (-jnp.arange(n_freq, dtype=F32) / n_freq)
    ang_r = row[:, None] * freqs
    ang_c = col[:, None] * freqs
    cos = jnp.concatenate([jnp.cos(ang_r), jnp.cos(ang_r), jnp.cos(ang_c), jnp.cos(ang_c)], axis=1)
    sin = jnp.concatenate([-jnp.sin(ang_r), jnp.sin(ang_r), -jnp.sin(ang_c), jnp.sin(ang_c)], axis=1)
    return cos, sin


def kernel(x, c, ctx, c_ctx, w_ada, b_ada, norm_w, w_in, q_norm_w, k_norm_w, conv_w, conv_b,
           lru_wa, lru_ba, lru_wx, lru_bx, lru_lambda, out_norm_att, out_norm_lru, w_out):
    assert w_ada.shape[0] == 1, "single-layer kernel: only the latent stream is produced"
    b, seq, d = x.shape
    n_ctx = ctx.shape[1]
    d_att = out_norm_att.shape[1]
    d_lru = out_norm_lru.shape[1]
    d_kv = (w_in.shape[2] - 2 * d_att - 2 * d_lru) // 2
    n_lat = b * seq
    n_all = n_lat + b * n_ctx
    assert b + 1 <= MOD_ROWS
    k_col0, v_col0, ga_col0 = d_att, d_att + d_kv, d_att + 2 * d_kv
    xl_col0 = ga_col0 + d_att
    gl_col0 = xl_col0 + d_lru

    x2 = x.reshape(n_lat, d)
    c2 = ctx.reshape(b * n_ctx, d)

    cvec = jnp.concatenate([c, c_ctx[None, :], jnp.zeros((MOD_ROWS - b - 1, d), F32)], axis=0)
    mod = _modulation(cvec, w_ada[0], b_ada)

    h = _prenorm(x2, c2, norm_w, mod, seq)
    w_in0 = w_in[0]

    cos, sin = _rope_tables(seq)
    qn = _proj_heads("proj_q", h, w_in0, n_lat, 0, d_att, 0, cos, sin, q_norm_w, seq, n_lat,
                     HEAD_DIM ** -0.5 * LOG2_E)
    assert v_col0 == k_col0 + d_kv
    kv = _proj_heads("proj_kv", h, w_in0, n_all, k_col0, 2 * d_kv, d_kv, cos, sin, k_norm_w, seq, n_lat, 1.0)
    xl = _proj_plain("proj_xl", h, w_in0, n_all, [(xl_col0, d_lru)], F32)
    gates, hu = _proj_conv_side("proj_gates", h, w_in0, n_lat, [(ga_col0, d_att), (gl_col0, d_lru)],
                                xl, conv_w[0], conv_b, seq)

    mix_att = _attention(qn, kv, gates, out_norm_att, b, seq, n_ctx)

    n_blocks = d_lru // LRU_BLOCK_DIM
    w_cat = jnp.concatenate([lru_wa[0], lru_wx[0]], axis=-1).astype(BF16)
    bias_cat = 0.5 * jnp.concatenate([lru_ba[0].reshape(2, n_blocks, 1, LRU_BLOCK_DIM),
                                      lru_bx[0].reshape(2, n_blocks, 1, LRU_BLOCK_DIM)], axis=-1)
    y_lru, ss = _lru(hu, xl, gates, conv_w[0], conv_b, w_cat, bias_cat, lru_lambda[0], out_norm_lru,
                     b, seq, n_ctx, d_att)

    out = _out_proj(mix_att, y_lru, ss, w_out[0], x2, mod, seq, 2 * d)
    return out.reshape(b, seq, d)
```

```python
import functools
import math

import jax
import jax.numpy as jnp
from jax import lax
from jax.experimental import pallas as pl
from jax.experimental.pallas import tpu as pltpu

F32 = jnp.float32
BF16 = jnp.bfloat16

HEAD_DIM = 128
GRID_W = 64
Q_PER_KV = 4
LRU_BLOCK_DIM = 128
CONV_WIDTH = 4
CONV_LEFT = 2
LRU_C = 8.0
ROPE_THETA = 10000.0
EPS = 1e-6
LOG2_E = 1.4426950408889634

LANES = 128
SUBLANES = 8
MXU_WIDTH = 256
MOD_ROWS = 8
V7X_VMEM_BYTES = 64 * 1024 * 1024
VMEM_LIMIT = V7X_VMEM_BYTES - 2 * 1024 * 1024
MATMUL_TEMP_BYTES = 4 * 1024 * 1024


def _params(*sem):
    return pltpu.CompilerParams(dimension_semantics=sem, vmem_limit_bytes=VMEM_LIMIT)


def _silu(x):
    half = 0.5 * x
    return half * jnp.tanh(half) + half


def _rms(x, w):
    ms = jnp.mean(x * x, axis=-1, keepdims=True)
    return x * lax.rsqrt(ms + EPS) * w


def _mod_kernel(c_ref, w_ref, b_ref, o_ref):
    s = _silu(c_ref[...]).astype(BF16)
    o_ref[...] = jnp.dot(s, w_ref[...].astype(BF16), preferred_element_type=F32) + b_ref[...]


def _modulation(cvec, w_ada, b_ada, tn=512):
    d, n = w_ada.shape
    return pl.pallas_call(
        _mod_kernel,
        grid=(n // tn,),
        in_specs=[pl.BlockSpec((MOD_ROWS, d), lambda j: (0, 0)),
                  pl.BlockSpec((d, tn), lambda j: (0, j)),
                  pl.BlockSpec((1, tn), lambda j: (0, j))],
        out_specs=pl.BlockSpec((MOD_ROWS, tn), lambda j: (0, j)),
        out_shape=jax.ShapeDtypeStruct((MOD_ROWS, n), F32),
        compiler_params=_params("parallel"),
        name="modulation",
    )(cvec, w_ada, b_ada)


def _prenorm_kernel(x_ref, c_ref, nw_ref, mod_ref, o_ref, *, d, n_x_tiles, tiles_per_batch, ctx_row):
    t = pl.program_id(0)

    def emit(src_ref, row):
        x = src_ref[...]
        shift = mod_ref[pl.ds(row, 1), 0:d]
        gain = nw_ref[...] * (1.0 + mod_ref[pl.ds(row, 1), d:2 * d])
        inv_rms = lax.rsqrt(jnp.mean(x * x, axis=-1, keepdims=True) + EPS)
        o_ref[...] = (x * inv_rms * gain + shift).astype(BF16)

    @pl.when(t < n_x_tiles)
    def _():
        emit(x_ref, t // tiles_per_batch)

    @pl.when(t >= n_x_tiles)
    def _():
        emit(c_ref, ctx_row)


def _prenorm(x2, c2, norm_w, mod, seq, tr=512):
    mx, d = x2.shape
    mc = c2.shape[0]
    nx, nc = mx // tr, mc // tr
    kern = functools.partial(_prenorm_kernel, d=d, n_x_tiles=nx, tiles_per_batch=seq // tr,
                             ctx_row=mx // seq)
    return pl.pallas_call(
        kern,
        grid=(nx + nc,),
        in_specs=[pl.BlockSpec((tr, d), lambda t: (jnp.minimum(t, nx - 1), 0)),
                  pl.BlockSpec((tr, d), lambda t: (jnp.maximum(t - nx, 0), 0)),
                  pl.BlockSpec((1, d), lambda t: (0, 0)),
                  pl.BlockSpec(mod.shape, lambda t: (0, 0))],
        out_specs=pl.BlockSpec((tr, d), lambda t: (t, 0)),
        out_shape=jax.ShapeDtypeStruct((mx + mc, d), BF16),
        compiler_params=_params("parallel"),
        name="prenorm",
    )(x2, c2, norm_w, mod)


WIDE_TN = 1024
NARROW_TN = 512


def _proj_tiles(n_cols, k, resident_bytes):
    shapes = [(1024, NARROW_TN)]
    if n_cols % WIDE_TN == 0:
        shapes.insert(0, (512, WIDE_TN))
    for tm, tn in shapes:
        if resident_bytes(tm, tn) + k * tn * 2 + 2 * k * tn * 4 + MATMUL_TEMP_BYTES <= VMEM_LIMIT:
            return tm, tn, 2
    return (*shapes[0], 1)


def _weight_spec(k, tn, buffers, index_map):
    if buffers == 1:
        return pl.BlockSpec((k, tn), index_map, pipeline_mode=pl.Buffered(1))
    return pl.BlockSpec((k, tn), index_map)


def _cast_weights_once(w_ref, wb_ref):
    @pl.when(pl.program_id(1) == 0)
    def _():
        wb_ref[...] = w_ref[...].astype(BF16)


def _rope(y, cos, sin_signed):
    lane = lax.broadcasted_iota(jnp.int32, y.shape, 1)
    first_half = (lane & (HEAD_DIM // 4)) == 0
    partner = jnp.where(first_half, pltpu.roll(y, HEAD_DIM - HEAD_DIM // 4, 1),
                        pltpu.roll(y, HEAD_DIM // 4, 1))
    return y * cos + partner * sin_signed


def _proj_heads_kernel(h_ref, w_ref, cos_ref, sin_ref, nw_ref, o_ref, wb_ref, acc0_ref, acc1_ref,
                       *, n_row_tiles, head_cols, out_scale):
    i = pl.program_id(1)
    _cast_weights_once(w_ref, wb_ref)
    accs = (acc0_ref, acc1_ref)

    def matmul_into(acc_ref):
        acc_ref[...] = jnp.dot(h_ref[...], wb_ref[...], preferred_element_type=F32)

    def finish_from(acc_ref):
        norm_w = nw_ref[...] * out_scale
        for hh in range(head_cols // HEAD_DIM):
            sl = slice(hh * HEAD_DIM, (hh + 1) * HEAD_DIM)
            o_ref[:, sl] = _rope(_rms(acc_ref[:, sl], norm_w), cos_ref[...], sin_ref[...]).astype(BF16)
        if head_cols < o_ref.shape[1]:
            o_ref[:, head_cols:] = acc_ref[:, head_cols:].astype(BF16)

    @pl.when(i == 0)
    def _():
        matmul_into(accs[0])

    for parity in range(2):
        @pl.when((i >= 1) & (i < n_row_tiles) & (i % 2 == parity))
        def _(parity=parity):
            matmul_into(accs[parity])
            finish_from(accs[1 - parity])

    @pl.when(i == n_row_tiles)
    def _():
        finish_from(accs[(n_row_tiles - 1) % 2])


def _proj_heads(name, h, w_in, n_rows, col0, n_cols, plain_cols, cos, sin, norm_w, seq, rope_rows, out_scale):
    k = h.shape[1]
    tm, tn, w_buffers = _proj_tiles(
        n_cols, k, lambda tm, tn: 2 * tm * k * 2 + 2 * tm * tn * 4 + 2 * tm * tn * 2 + 4 * tm * HEAD_DIM * 4)
    assert n_rows % tm == 0 and seq % tm == 0 and rope_rows % tm == 0 and col0 % tn == 0
    assert plain_cols == 0 or n_cols == tn
    head_cols = tn - plain_cols
    n_row_tiles = n_rows // tm
    rope_tiles = rope_rows // tm
    spt = seq // tm
    cos = jnp.concatenate([cos, jnp.ones((tm, HEAD_DIM), F32)], axis=0)
    sin = jnp.concatenate([sin, jnp.zeros((tm, HEAD_DIM), F32)], axis=0)
    kern = functools.partial(_proj_heads_kernel, n_row_tiles=n_row_tiles, head_cols=head_cols,
                             out_scale=out_scale)
    done = lambda i: jnp.maximum(i - 1, 0)
    table = lambda j, i: (jnp.where(done(i) < rope_tiles, done(i) % spt, spt), 0)
    return pl.pallas_call(
        kern,
        grid=(n_cols // tn, n_row_tiles + 1),
        in_specs=[pl.BlockSpec((tm, k), lambda j, i: (jnp.minimum(i, n_row_tiles - 1), 0)),
                  _weight_spec(k, tn, w_buffers, lambda j, i: (0, col0 // tn + j)),
                  pl.BlockSpec((tm, HEAD_DIM), table),
                  pl.BlockSpec((tm, HEAD_DIM), table),
                  pl.BlockSpec((1, HEAD_DIM), lambda j, i: (0, 0))],
        out_specs=pl.BlockSpec((tm, tn), lambda j, i: (done(i), j)),
        out_shape=jax.ShapeDtypeStruct((n_rows, n_cols), BF16),
        scratch_shapes=[pltpu.VMEM((k, tn), BF16),
                        pltpu.VMEM((tm, tn), F32),
                        pltpu.VMEM((tm, tn), F32)],
        compiler_params=_params("parallel", "arbitrary"),
        name=name,
    )(h, w_in, cos, sin, norm_w)


CONV_PAD_ROWS = 8
CONV_PIECE = 32


def _half_conv_rows(window, half_w, half_b):
    halo = window.shape[0]
    acc = half_b
    for j in range(CONV_WIDTH):
        off = j - CONV_LEFT
        shifted = window if off == 0 else pltpu.roll(window, (-off) % halo, 0)
        acc = acc + half_w[j] * shifted[CONV_PAD_ROWS:CONV_PAD_ROWS + CONV_PIECE, :]
    return acc


def _proj_conv_side_kernel(h_ref, w_ref, x_ref, top_ref, bot_ref, cw_ref, cb_ref, o_ref, u_ref, wb_ref,
                           *, tiles_per_seq):
    i = pl.program_id(1)
    _cast_weights_once(w_ref, wb_ref)
    o_ref[...] = jnp.dot(h_ref[...], wb_ref[...], preferred_element_type=F32)
    tm, cw = x_ref.shape
    half_w = [0.5 * cw_ref[j:j + 1, :] for j in range(CONV_WIDTH)]
    half_b = 0.5 * cb_ref[...]
    in_seq = i % tiles_per_seq
    top = jnp.where(in_seq == 0, 0.0, top_ref[...])
    bot = jnp.where(in_seq == tiles_per_seq - 1, 0.0, bot_ref[...])
    for p in range(tm // CONV_PIECE):
        lo, hi = p * CONV_PIECE - CONV_PAD_ROWS, (p + 1) * CONV_PIECE + CONV_PAD_ROWS
        parts = [x_ref[max(lo, 0):min(hi, tm), :]]
        if lo < 0:
            parts.insert(0, top)
        if hi > tm:
            parts.append(bot)
        window = parts[0] if len(parts) == 1 else jnp.concatenate(parts, axis=0)
        conv = _half_conv_rows(window, half_w, half_b)
        conv = pltpu.roll(pltpu.roll(conv, LANES // 2, 1), cw - LANES // 2, 1)
        u_ref[p * CONV_PIECE:(p + 1) * CONV_PIECE, :] = conv


def _proj_conv_side(name, h, w_in, n_rows, col_ranges, x, conv_w, conv_b, seq):
    k = h.shape[1]
    d_x = x.shape[1]
    n_cols = sum(width for _, width in col_ranges)
    conv_cols = lambda tn: d_x * tn // n_cols
    tm, tn, w_buffers = _proj_tiles(
        math.gcd(*[c for r in col_ranges for c in r]), k,
        lambda tm, tn: 2 * tm * k * 2 + 2 * tm * tn * 4 + 4 * tm * conv_cols(tn) * 4)
    tiles = [(start // tn, width // tn) for start, width in col_ranges]
    n_col_tiles = n_cols // tn
    cw = conv_cols(tn)
    assert n_rows % tm == 0 and seq % tm == 0 and tm % CONV_PIECE == 0 and cw * n_col_tiles == d_x
    assert x.shape[0] >= n_rows + CONV_PAD_ROWS
    halo_blocks = tm // CONV_PAD_ROWS

    def col_tile(j):
        first, idx = 0, tiles[0][0] + j
        for r in range(1, len(tiles)):
            first += tiles[r - 1][1]
            idx = jnp.where(j >= first, tiles[r][0] + j - first, idx)
        return idx

    kern = functools.partial(_proj_conv_side_kernel, tiles_per_seq=seq // tm)
    return pl.pallas_call(
        kern,
        grid=(n_col_tiles, n_rows // tm),
        in_specs=[pl.BlockSpec((tm, k), lambda j, i: (i, 0)),
                  _weight_spec(k, tn, w_buffers, lambda j, i: (0, col_tile(j))),
                  pl.BlockSpec((tm, cw), lambda j, i: (i, j)),
                  pl.BlockSpec((CONV_PAD_ROWS, cw), lambda j, i: (jnp.maximum(i * halo_blocks - 1, 0), j)),
                  pl.BlockSpec((CONV_PAD_ROWS, cw), lambda j, i: ((i + 1) * halo_blocks, j)),
                  pl.BlockSpec((CONV_WIDTH, cw), lambda j, i: (0, j)),
                  pl.BlockSpec((1, cw), lambda j, i: (0, j))],
        out_specs=[pl.BlockSpec((tm, tn), lambda j, i: (i, j)),
                   pl.BlockSpec((tm, cw), lambda j, i: (i, j))],
        out_shape=[jax.ShapeDtypeStruct((n_rows, n_cols), F32),
                   jax.ShapeDtypeStruct((n_rows, d_x), F32)],
        scratch_shapes=[pltpu.VMEM((k, tn), BF16)],
        compiler_params=_params("parallel", "arbitrary"),
        name=name,
    )(h, w_in, x, x, x, conv_w, conv_b)


def _proj_plain_kernel(h_ref, w_ref, o_ref, wb_ref):
    _cast_weights_once(w_ref, wb_ref)
    o_ref[...] = jnp.dot(h_ref[...], wb_ref[...], preferred_element_type=F32).astype(o_ref.dtype)


def _proj_plain(name, h, w_in, n_rows, col_ranges, out_dtype):
    k = h.shape[1]
    out_bytes = jnp.dtype(out_dtype).itemsize
    tm, tn, w_buffers = _proj_tiles(math.gcd(*[c for r in col_ranges for c in r]), k,
                                    lambda tm, tn: 2 * tm * k * 2 + 2 * tm * tn * out_bytes)
    assert n_rows % tm == 0
    tiles = [(start // tn, width // tn) for start, width in col_ranges]

    def col_tile(j):
        first, idx = 0, tiles[0][0] + j
        for r in range(1, len(tiles)):
            first += tiles[r - 1][1]
            idx = jnp.where(j >= first, tiles[r][0] + j - first, idx)
        return idx

    n_col_tiles = sum(n for _, n in tiles)
    return pl.pallas_call(
        _proj_plain_kernel,
        grid=(n_col_tiles, n_rows // tm),
        in_specs=[pl.BlockSpec((tm, k), lambda j, i: (i, 0)),
                  _weight_spec(k, tn, w_buffers, lambda j, i: (0, col_tile(j)))],
        out_specs=pl.BlockSpec((tm, tn), lambda j, i: (i, j)),
        out_shape=jax.ShapeDtypeStruct((n_rows, n_col_tiles * tn), out_dtype),
        scratch_shapes=[pltpu.VMEM((k, tn), BF16)],
        compiler_params=_params("parallel", "arbitrary"),
        name=name,
    )(h, w_in)


V_EXTRA_ROWS = 16


def _attn_kernel(q_ref, kl_ref, vl_ref, kc_ref, vc_ref, ga_ref, nw_ref, o_ref,
                 k_ref, vt_ref, st_ref, att_ref, *, n_q_tiles):
    tq = q_ref.shape[0]
    seq, n_ctx = kl_ref.shape[0], kc_ref.shape[0]
    n_kv = kl_ref.shape[1] // HEAD_DIM
    nt = (((1,), (1,)), ((), ()))
    i = pl.program_id(1)
    n_keys = seq + n_ctx

    @pl.when(i == 0)
    def _():
        k_ref[0:seq, :] = kl_ref[...]
        k_ref[seq:n_keys, :] = kc_ref[...]
        row = lax.broadcasted_iota(jnp.int32, (V_EXTRA_ROWS, n_keys), 0)
        tail = jnp.where(row == 0, 1.0, 0.0).astype(BF16)
        for kv in range(n_kv):
            ks = slice(kv * HEAD_DIM, (kv + 1) * HEAD_DIM)
            vt_ref[kv, 0:HEAD_DIM, 0:seq] = vl_ref[:, ks].T
            vt_ref[kv, 0:HEAD_DIM, seq:n_keys] = vc_ref[:, ks].T
            vt_ref[kv, HEAD_DIM:HEAD_DIM + V_EXTRA_ROWS, :] = tail

    def head_slices(kv):
        return [slice((kv * Q_PER_KV + g) * HEAD_DIM, (kv * Q_PER_KV + g + 1) * HEAD_DIM)
                for g in range(Q_PER_KV)]

    def scores(kv, slot):
        q4 = jnp.concatenate([q_ref[:, hs] for hs in head_slices(kv)], axis=0)
        st_ref[slot] = lax.dot_general(k_ref[:, kv * HEAD_DIM:(kv + 1) * HEAD_DIM], q4, nt,
                                       preferred_element_type=F32)

    def softmax_values(kv, slot, tile_slot):
        m = jnp.max(st_ref[slot], axis=0, keepdims=True)
        pt = jnp.exp2(st_ref[slot] - m).astype(BF16)
        res = jnp.dot(vt_ref[kv], pt, preferred_element_type=F32)
        ot = res[:HEAD_DIM, :] / res[HEAD_DIM:HEAD_DIM + 1, :]
        for g, hs in enumerate(head_slices(kv)):
            att_ref[tile_slot, :, hs] = ot[:, g * tq:(g + 1) * tq].T

    def attend(tile_slot):
        scores(0, 0)
        for kv in range(n_kv):
            if kv + 1 < n_kv:
                scores(kv + 1, (kv + 1) % 2)
            softmax_values(kv, kv % 2, tile_slot)

    def finish(tile_slot):
        att = att_ref[tile_slot]
        inv_rms = lax.rsqrt(jnp.mean(att * att, axis=-1, keepdims=True) + EPS)
        o_ref[...] = (att * inv_rms * (_silu(ga_ref[...]) * nw_ref[...])).astype(BF16)

    @pl.when(i == 0)
    def _():
        attend(0)

    @pl.when((i >= 1) & (i < n_q_tiles))
    def _():
        attend(i % 2)
        finish((i - 1) % 2)

    @pl.when(i == n_q_tiles)
    def _():
        finish((n_q_tiles - 1) % 2)


def _attention(qn, kv, gates, norm_w, b, seq, n_ctx, tq=256):
    n_lat, d_att = qn.shape
    d_kv = kv.shape[1] // 2
    qt = seq // tq
    ctx_blk0 = n_lat // n_ctx
    this = lambda bi, i: (bi * qt + jnp.minimum(i, qt - 1), 0)
    done = lambda bi, i: (bi * qt + jnp.maximum(i - 1, 0), 0)
    return pl.pallas_call(
        functools.partial(_attn_kernel, n_q_tiles=qt),
        grid=(b, qt + 1),
        in_specs=[pl.BlockSpec((tq, d_att), this),
                  pl.BlockSpec((seq, d_kv), lambda bi, i: (bi, 0), pipeline_mode=pl.Buffered(1)),
                  pl.BlockSpec((seq, d_kv), lambda bi, i: (bi, 1), pipeline_mode=pl.Buffered(1)),
                  pl.BlockSpec((n_ctx, d_kv), lambda bi, i: (ctx_blk0 + bi, 0), pipeline_mode=pl.Buffered(1)),
                  pl.BlockSpec((n_ctx, d_kv), lambda bi, i: (ctx_blk0 + bi, 1), pipeline_mode=pl.Buffered(1)),
                  pl.BlockSpec((tq, d_att), done),
                  pl.BlockSpec((1, d_att), lambda bi, i: (0, 0))],
        out_specs=pl.BlockSpec((tq, d_att), done),
        out_shape=jax.ShapeDtypeStruct((n_lat, d_att), BF16),
        scratch_shapes=[pltpu.VMEM((seq + n_ctx, d_kv), BF16),
                        pltpu.VMEM((d_kv // HEAD_DIM, HEAD_DIM + V_EXTRA_ROWS, seq + n_ctx), BF16),
                        pltpu.VMEM((2, seq + n_ctx, Q_PER_KV * tq), F32),
                        pltpu.VMEM((2, tq, d_att), F32)],
        compiler_params=_params("parallel", "arbitrary"),
        name="attention",
    )(qn, kv, kv, kv, kv, gates, norm_w)


LRU_SEGS = SUBLANES
GATE_ROWS = 256
SCAN_STEPS = 4
FINAL_STEPS = 2


def _softplus(z):
    return jnp.maximum(z, 0.0) + jnp.log1p(jnp.exp(-jnp.abs(z)))


def _segment_len(total):
    seg = -(-total // LRU_SEGS)
    return seg + (SUBLANES // 2 - seg) % SUBLANES


def _scan8(a, bv, reverse):
    row = lax.broadcasted_iota(jnp.int32, a.shape, 0)
    for s in (1, 2, 4):
        if reverse:
            keep = row < SUBLANES - s
            shift = SUBLANES - s
        else:
            keep = row >= s
            shift = s
        a_prev = jnp.where(keep, pltpu.roll(a, shift, 0), 1.0)
        b_prev = jnp.where(keep, pltpu.roll(bv, shift, 0), 0.0)
        bv = bv + a * b_prev
        a = a * a_prev
    return a, bv


def _lru_kernel(ul_ref, xc_ref, cw_ref, cb_ref, w_ref, bias_ref, lam_ref, gl_ref, nw_ref, y_ref, ss_ref,
                xp_ref, u_ref, a_ref, b_ref, hs_ref, *, seq, n_ctx, seg_len):
    cg = y_ref.shape[1]
    n_blk = cg // LRU_BLOCK_DIM
    total = n_ctx + seq
    padded = LRU_SEGS * seg_len
    n_chunks = total // GATE_ROWS
    gate_unroll = next(u for u in (9, 3, 2, 1) if n_chunks % u == 0)

    @pl.when(pl.program_id(1) == 0)
    def _():
        ss_ref[...] = jnp.zeros(ss_ref.shape, F32)

    for d in range(2):
        for n in range(n_blk):
            a_ref[d, n, total:padded, :] = jnp.ones((padded - total, LRU_BLOCK_DIM), F32)
            b_ref[d, n, total:padded, :] = jnp.zeros((padded - total, LRU_BLOCK_DIM), F32)
    zeros_pad = jnp.zeros((CONV_PAD_ROWS, cg), F32)
    halo = CONV_PIECE + 2 * CONV_PAD_ROWS
    half_w = [0.5 * cw_ref[j:j + 1, :] for j in range(CONV_WIDTH)]
    half_b = 0.5 * cb_ref[...]
    xp_ref[0:CONV_PAD_ROWS, :] = zeros_pad
    xp_ref[CONV_PAD_ROWS:CONV_PAD_ROWS + n_ctx, :] = xc_ref[...]
    xp_ref[CONV_PAD_ROWS + n_ctx:2 * CONV_PAD_ROWS + n_ctx, :] = zeros_pad
    for p in range(n_ctx // CONV_PIECE):
        piece = _half_conv_rows(xp_ref[p * CONV_PIECE:p * CONV_PIECE + halo, :], half_w, half_b)
        for u0 in (0, n_ctx + seq):
            u_ref[u0 + p * CONV_PIECE:u0 + (p + 1) * CONV_PIECE, :] = piece
    u_ref[n_ctx:n_ctx + seq, :] = ul_ref[...]

    for d in range(2):
        reverse = d == 1
        seq_row0 = n_ctx if reverse else 0
        half_decay = [0.5 * LRU_C * _softplus(-lam_ref[d:d + 1, n * LRU_BLOCK_DIM:(n + 1) * LRU_BLOCK_DIM])
                      for n in range(n_blk)]

        def gate_body(cb, _, d=d, seq_row0=seq_row0, half_decay=half_decay):
            for cc in range(gate_unroll):
                s0 = pl.multiple_of((cb * gate_unroll + cc) * GATE_ROWS, GATE_ROWS)
                hu = u_ref[pl.ds(pl.multiple_of(seq_row0 + s0, SUBLANES), GATE_ROWS), :]
                hub = hu.astype(BF16)
                for n in range(n_blk):
                    sl = slice(n * LRU_BLOCK_DIM, (n + 1) * LRU_BLOCK_DIM)
                    pre = jnp.dot(hub[:, sl], w_ref[d, n], preferred_element_type=F32) + bias_ref[d, n]
                    t_r = jnp.tanh(pre[:, :LRU_BLOCK_DIM])
                    t_i = jnp.tanh(pre[:, LRU_BLOCK_DIM:])
                    neg_log_a = t_r * half_decay[n] + half_decay[n]
                    a = jnp.exp2(neg_log_a * (-LOG2_E))
                    a_ref[d, n, pl.ds(s0, GATE_ROWS), :] = a
                    w = jnp.tanh(neg_log_a) * (1.0 + a * a)
                    mult = jnp.where(w > 0.0, w * lax.rsqrt(w), 0.0)
                    b_ref[d, n, pl.ds(s0, GATE_ROWS), :] = mult * ((t_i + 1.0) * hu[:, sl])
            return 0

        lax.fori_loop(0, n_chunks // gate_unroll, gate_body, 0)

    def step_rows(tb, k, steps, d):
        i = (seg_len // steps - 1 - tb) * steps + steps - 1 - k if d == 1 else tb * steps + k
        return pl.ds(i, LRU_SEGS, stride=seg_len)

    def local_body(tb, carry):
        hs, decs = [list(c) for c in carry[0]], [list(c) for c in carry[1]]
        for k in range(SCAN_STEPS):
            for d in range(2):
                idx = step_rows(tb, k, SCAN_STEPS, d)
                for n in range(n_blk):
                    a = a_ref[d, n, idx, :]
                    hs[d][n] = a * hs[d][n] + b_ref[d, n, idx, :]
                    decs[d][n] = decs[d][n] * a
        return tuple(map(tuple, hs)), tuple(map(tuple, decs))

    zero = tuple(tuple(jnp.zeros((LRU_SEGS, LRU_BLOCK_DIM), F32) for _ in range(n_blk)) for _ in range(2))
    one = tuple(tuple(jnp.ones((LRU_SEGS, LRU_BLOCK_DIM), F32) for _ in range(n_blk)) for _ in range(2))
    h_end, dec_end = lax.fori_loop(0, seg_len // SCAN_STEPS, local_body, (zero, one))

    starts = []
    for d in range(2):
        starts.append([])
        for n in range(n_blk):
            _, state = _scan8(dec_end[d][n], h_end[d][n], d == 1)
            row = lax.broadcasted_iota(jnp.int32, state.shape, 0)
            if d == 1:
                starts[d].append(jnp.where(row < LRU_SEGS - 1, pltpu.roll(state, LRU_SEGS - 1, 0), 0.0))
            else:
                starts[d].append(jnp.where(row >= 1, pltpu.roll(state, 1, 0), 0.0))

    def final_body(tb, hs):
        hs = [list(h) for h in hs]
        loaded = [(d, idx, [a_ref[d, n, idx, :] for n in range(n_blk)], [b_ref[d, n, idx, :] for n in range(n_blk)])
                  for k in range(FINAL_STEPS) for d in range(2)
                  for idx in [step_rows(tb, k, FINAL_STEPS, d)]]
        for d, idx, a, bv in loaded:
            for n in range(n_blk):
                hs[d][n] = a[n] * hs[d][n] + bv[n]
                hs_ref[d, n, idx, :] = hs[d][n]
        return tuple(map(tuple, hs))

    lax.fori_loop(0, seg_len // FINAL_STEPS, final_body, tuple(map(tuple, starts)))

    for r0 in range(0, seq, GATE_ROWS):
        rows = slice(r0, r0 + GATE_ROWS)
        lru = jnp.concatenate(
            [hs_ref[0, n, n_ctx + r0:n_ctx + r0 + GATE_ROWS, :] + hs_ref[1, n, r0:r0 + GATE_ROWS, :]
             for n in range(n_blk)], axis=1)
        y_ref[rows, :] = (lru * nw_ref[...] * _silu(gl_ref[rows, :])).astype(BF16)
        ss_ref[rows, :] += jnp.broadcast_to(jnp.sum(lru * lru, axis=-1, keepdims=True), (GATE_ROWS, LANES))


def _lru(hu, xl, gates, conv_w, conv_b, w_cat, bias_cat, lam, norm_w, b, seq, n_ctx, d_att, cg=512):
    n_lat = b * seq
    d_lru = xl.shape[1]
    n_blk = cg // LRU_BLOCK_DIM
    n_groups = d_lru // cg
    seg_len = _segment_len(n_ctx + seq)
    assert n_ctx % GATE_ROWS == 0 and seq % GATE_ROWS == 0 and GATE_ROWS % CONV_PIECE == 0
    assert seg_len % SCAN_STEPS == 0 and seg_len % FINAL_STEPS == 0
    kern = functools.partial(_lru_kernel, seq=seq, n_ctx=n_ctx, seg_len=seg_len)
    ctx_blk0 = n_lat // n_ctx
    gl_blk0 = d_att // cg
    scan_rows = LRU_SEGS * seg_len
    return pl.pallas_call(
        kern,
        grid=(b, n_groups),
        in_specs=[pl.BlockSpec((seq, cg), lambda bi, g: (bi, g)),
                  pl.BlockSpec((n_ctx, cg), lambda bi, g: (ctx_blk0 + bi, g)),
                  pl.BlockSpec((CONV_WIDTH, cg), lambda bi, g: (0, g)),
                  pl.BlockSpec((1, cg), lambda bi, g: (0, g)),
                  pl.BlockSpec((2, n_blk, LRU_BLOCK_DIM, 2 * LRU_BLOCK_DIM), lambda bi, g: (0, g, 0, 0)),
                  pl.BlockSpec((2, n_blk, 1, 2 * LRU_BLOCK_DIM), lambda bi, g: (0, g, 0, 0)),
                  pl.BlockSpec((2, cg), lambda bi, g: (0, g)),
                  pl.BlockSpec((seq, cg), lambda bi, g: (bi, gl_blk0 + g)),
                  pl.BlockSpec((1, cg), lambda bi, g: (0, g))],
        out_specs=[pl.BlockSpec((seq, cg), lambda bi, g: (bi, g)),
                   pl.BlockSpec((seq, LANES), lambda bi, g: (bi, 0))],
        out_shape=[jax.ShapeDtypeStruct((n_lat, d_lru), BF16),
                   jax.ShapeDtypeStruct((n_lat, LANES), F32)],
        scratch_shapes=[pltpu.VMEM((n_ctx + 2 * CONV_PAD_ROWS, cg), F32),
                        pltpu.VMEM((2 * n_ctx + seq, cg), F32),
                        pltpu.VMEM((2, n_blk, scan_rows, LRU_BLOCK_DIM), F32),
                        pltpu.VMEM((2, n_blk, scan_rows, LRU_BLOCK_DIM), F32),
                        pltpu.VMEM((2, n_blk, scan_rows, LRU_BLOCK_DIM), F32)],
        compiler_params=_params("parallel", "arbitrary"),
        name="rglru",
    )(hu, xl, conv_w, conv_b, w_cat, bias_cat, lam, gates, norm_w)


def _out_kernel(ma_ref, yl_ref, ss_ref, wa_ref, wl_ref, x_ref, gate_ref, o_ref, wab_ref, wlb_ref,
                *, tiles_per_batch, d_lru):
    @pl.when(pl.program_id(1) == 0)
    def _():
        wab_ref[...] = wa_ref[...].astype(BF16)
        wlb_ref[...] = wl_ref[...].astype(BF16)

    bi = pl.program_id(1) // tiles_per_batch
    gate = gate_ref[pl.ds(bi, 1), :]
    sumsq = ss_ref[:, 0:LANES]
    for g in range(1, ss_ref.shape[1] // LANES):
        sumsq = sumsq + ss_ref[:, g * LANES:(g + 1) * LANES]
    inv_rms = lax.rsqrt(sumsq * (1.0 / d_lru) + EPS)
    inv_rms = jnp.concatenate([inv_rms] * (o_ref.shape[1] // LANES), axis=1)
    acc = jnp.dot(ma_ref[...], wab_ref[...], preferred_element_type=F32)
    acc = acc + inv_rms * jnp.dot(yl_ref[...], wlb_ref[...], preferred_element_type=F32)
    o_ref[...] = x_ref[...] + gate * acc


def _out_proj(mix_att, y_lru, ss, w_out, x2, mod, seq, gate_col0):
    m, d_att = mix_att.shape
    d_lru = y_lru.shape[1]
    assert d_att == d_lru
    n = w_out.shape[1]
    tm, tn, w_buffers = _proj_tiles(
        n, d_att + d_lru,
        lambda tm, tn: 2 * tm * (d_att + d_lru) * 2 + 4 * tm * tn * 4 + 2 * tm * LANES * 4)
    assert m % tm == 0 and seq % tm == 0 and gate_col0 % tn == 0
    gb = gate_col0 // tn
    kern = functools.partial(_out_kernel, tiles_per_batch=seq // tm, d_lru=d_lru)
    return pl.pallas_call(
        kern,
        grid=(n // tn, m // tm),
        in_specs=[pl.BlockSpec((tm, d_att), lambda j, i: (i, 0)),
                  pl.BlockSpec((tm, d_lru), lambda j, i: (i, 0)),
                  pl.BlockSpec((tm, ss.shape[1]), lambda j, i: (i, 0)),
                  _weight_spec(d_att, tn, w_buffers, lambda j, i: (0, j)),
                  _weight_spec(d_lru, tn, w_buffers, lambda j, i: (1, j)),
                  pl.BlockSpec((tm, tn), lambda j, i: (i, j)),
                  pl.BlockSpec((MOD_ROWS, tn), lambda j, i: (0, gb + j))],
        out_specs=pl.BlockSpec((tm, tn), lambda j, i: (i, j)),
        out_shape=jax.ShapeDtypeStruct((m, n), F32),
        scratch_shapes=[pltpu.VMEM((d_att, tn), BF16), pltpu.VMEM((d_lru, tn), BF16)],
        compiler_params=_params("parallel", "arbitrary"),
        name="out_proj",
    )(mix_att, y_lru, ss, w_out, w_out, x2, mod)


def _rope_tables(seq):
    pos = jnp.arange(seq)
    row = (pos // GRID_W).astype(F32)
    col = (pos % GRID_W).astype(F32)
    n_freq = HEAD_DIM // 4
    freqs = ROPE_THETA ** (-jnp.arange(n_freq, dtype=F32) / n_freq)
    ang_r = row[:, None] * freqs
    ang_c = col[:, None] * freqs
    cos = jnp.concatenate([jnp.cos(ang_r), jnp.cos(ang_r), jnp.cos(ang_c), jnp.cos(ang_c)], axis=1)
    sin = jnp.concatenate([-jnp.sin(ang_r), jnp.sin(ang_r), -jnp.sin(ang_c), jnp.sin(ang_c)], axis=1)
    return cos, sin


def kernel(x, c, ctx, c_ctx, w_ada, b_ada, norm_w, w_in, q_norm_w, k_norm_w, conv_w, conv_b,
           lru_wa, lru_ba, lru_wx, lru_bx, lru_lambda, out_norm_att, out_norm_lru, w_out):
    assert w_ada.shape[0] == 1, "single-layer kernel: only the latent stream is produced"
    b, seq, d = x.shape
    n_ctx = ctx.shape[1]
    d_att = out_norm_att.shape[1]
    d_lru = out_norm_lru.shape[1]
    d_kv = (w_in.shape[2] - 2 * d_att - 2 * d_lru) // 2
    n_lat = b * seq
    n_all = n_lat + b * n_ctx
    assert b + 1 <= MOD_ROWS
    k_col0, v_col0, ga_col0 = d_att, d_att + d_kv, d_att + 2 * d_kv
    xl_col0 = ga_col0 + d_att
    gl_col0 = xl_col0 + d_lru

    x2 = x.reshape(n_lat, d)
    c2 = ctx.reshape(b * n_ctx, d)

    cvec = jnp.concatenate([c, c_ctx[None, :], jnp.zeros((MOD_ROWS - b - 1, d), F32)], axis=0)
    mod = _modulation(cvec, w_ada[0], b_ada)

    h = _prenorm(x2, c2, norm_w, mod, seq)
    w_in0 = w_in[0]

    cos, sin = _rope_tables(seq)
    qn = _proj_heads("proj_q", h, w_in0, n_lat, 0, d_att, 0, cos, sin, q_norm_w, seq, n_lat,
                     HEAD_DIM ** -0.5 * LOG2_E)
    assert v_col0 == k_col0 + d_kv
    kv = _proj_heads("proj_kv", h, w_in0, n_all, k_col0, 2 * d_kv, d_kv, cos, sin, k_norm_w, seq, n_lat, 1.0)
    xl = _proj_plain("proj_xl", h, w_in0, n_all, [(xl_col0, d_lru)], F32)
    gates, hu = _proj_conv_side("proj_gates", h, w_in0, n_lat, [(ga_col0, d_att), (gl_col0, d_lru)],
                                xl, conv_w[0], conv_b, seq)

    mix_att = _attention(qn, kv, gates, out_norm_att, b, seq, n_ctx)

    n_blocks = d_lru // LRU_BLOCK_DIM
    w_cat = jnp.concatenate([lru_wa[0], lru_wx[0]], axis=-1).astype(BF16)
    bias_cat = 0.5 * jnp.concatenate([lru_ba[0].reshape(2, n_blocks, 1, LRU_BLOCK_DIM),
                                      lru_bx[0].reshape(2, n_blocks, 1, LRU_BLOCK_DIM)], axis=-1)
    y_lru, ss = _lru(hu, xl, gates, conv_w[0], conv_b, w_cat, bias_cat, lru_lambda[0], out_norm_lru,
                     b, seq, n_ctx, d_att)

    out = _out_proj(mix_att, y_lru, ss, w_out[0], x2, mod, seq, 2 * d)
    return out.reshape(b, seq, d)
```

```python
import functools
import math

import jax
import jax.numpy as jnp
from jax import lax
from jax.experimental import pallas as pl
from jax.experimental.pallas import tpu as pltpu

F32 = jnp.float32
BF16 = jnp.bfloat16

HEAD_DIM = 128
GRID_W = 64
Q_PER_KV = 4
LRU_BLOCK_DIM = 128
CONV_WIDTH = 4
CONV_LEFT = 2
LRU_C = 8.0
ROPE_THETA = 10000.0
EPS = 1e-6
LOG2_E = 1.4426950408889634

LANES = 128
SUBLANES = 8
MXU_WIDTH = 256
MOD_ROWS = 8
V7X_VMEM_BYTES = 64 * 1024 * 1024
VMEM_LIMIT = V7X_VMEM_BYTES - 2 * 1024 * 1024
MATMUL_TEMP_BYTES = 4 * 1024 * 1024


def _params(*sem):
    return pltpu.CompilerParams(dimension_semantics=sem, vmem_limit_bytes=VMEM_LIMIT)


def _silu(x):
    half = 0.5 * x
    return half * jnp.tanh(half) + half


def _rms(x, w):
    ms = jnp.mean(x * x, axis=-1, keepdims=True)
    return x * lax.rsqrt(ms + EPS) * w


def _mod_kernel(c_ref, w_ref, b_ref, o_ref):
    s = _silu(c_ref[...]).astype(BF16)
    o_ref[...] = jnp.dot(s, w_ref[...].astype(BF16), preferred_element_type=F32) + b_ref[...]


def _modulation(cvec, w_ada, b_ada, tn=512):
    d, n = w_ada.shape
    return pl.pallas_call(
        _mod_kernel,
        grid=(n // tn,),
        in_specs=[pl.BlockSpec((MOD_ROWS, d), lambda j: (0, 0)),
                  pl.BlockSpec((d, tn), lambda j: (0, j)),
                  pl.BlockSpec((1, tn), lambda j: (0, j))],
        out_specs=pl.BlockSpec((MOD_ROWS, tn), lambda j: (0, j)),
        out_shape=jax.ShapeDtypeStruct((MOD_ROWS, n), F32),
        compiler_params=_params("parallel"),
        name="modulation",
    )(cvec, w_ada, b_ada)


def _prenorm_kernel(x_ref, c_ref, nw_ref, mod_ref, o_ref, *, d, n_x_tiles, tiles_per_batch, ctx_row):
    t = pl.program_id(0)

    def emit(src_ref, row):
        x = src_ref[...]
        shift = mod_ref[pl.ds(row, 1), 0:d]
        gain = nw_ref[...] * (1.0 + mod_ref[pl.ds(row, 1), d:2 * d])
        inv_rms = lax.rsqrt(jnp.mean(x * x, axis=-1, keepdims=True) + EPS)
        o_ref[...] = (x * inv_rms * gain + shift).astype(BF16)

    @pl.when(t < n_x_tiles)
    def _():
        emit(x_ref, t // tiles_per_batch)

    @pl.when(t >= n_x_tiles)
    def _():
        emit(c_ref, ctx_row)


def _prenorm(x2, c2, norm_w, mod, seq, tr=512):
    mx, d = x2.shape
    mc = c2.shape[0]
    nx, nc = mx // tr, mc // tr
    kern = functools.partial(_prenorm_kernel, d=d, n_x_tiles=nx, tiles_per_batch=seq // tr,
                             ctx_row=mx // seq)
    return pl.pallas_call(
        kern,
        grid=(nx + nc,),
        in_specs=[pl.BlockSpec((tr, d), lambda t: (jnp.minimum(t, nx - 1), 0)),
                  pl.BlockSpec((tr, d), lambda t: (jnp.maximum(t - nx, 0), 0)),
                  pl.BlockSpec((1, d), lambda t: (0, 0)),
                  pl.BlockSpec(mod.shape, lambda t: (0, 0))],
        out_specs=pl.BlockSpec((tr, d), lambda t: (t, 0)),
        out_shape=jax.ShapeDtypeStruct((mx + mc, d), BF16),
        compiler_params=_params("parallel"),
        name="prenorm",
    )(x2, c2, norm_w, mod)


WIDE_TN = 1024
NARROW_TN = 512


def _proj_tiles(n_cols, k, resident_bytes):
    shapes = [(1024, NARROW_TN)]
    if n_cols % WIDE_TN == 0:
        shapes.insert(0, (512, WIDE_TN))
    for tm, tn in shapes:
        if resident_bytes(tm, tn) + k * tn * 2 + 2 * k * tn * 4 + MATMUL_TEMP_BYTES <= VMEM_LIMIT:
            return tm, tn, 2
    return (*shapes[0], 1)


def _weight_spec(k, tn, buffers, index_map):
    if buffers == 1:
        return pl.BlockSpec((k, tn), index_map, pipeline_mode=pl.Buffered(1))
    return pl.BlockSpec((k, tn), index_map)


def _cast_weights_once(w_ref, wb_ref):
    @pl.when(pl.program_id(1) == 0)
    def _():
        wb_ref[...] = w_ref[...].astype(BF16)


def _rope(y, cos, sin_signed):
    lane = lax.broadcasted_iota(jnp.int32, y.shape, 1)
    first_half = (lane & (HEAD_DIM // 4)) == 0
    partner = jnp.where(first_half, pltpu.roll(y, HEAD_DIM - HEAD_DIM // 4, 1),
                        pltpu.roll(y, HEAD_DIM // 4, 1))
    return y * cos + partner * sin_signed


def _proj_heads_kernel(h_ref, w_ref, cos_ref, sin_ref, nw_ref, o_ref, wb_ref, acc0_ref, acc1_ref,
                       *, n_row_tiles, head_cols, out_scale):
    i = pl.program_id(1)
    _cast_weights_once(w_ref, wb_ref)
    accs = (acc0_ref, acc1_ref)

    def matmul_into(acc_ref):
        acc_ref[...] = jnp.dot(h_ref[...], wb_ref[...], preferred_element_type=F32)

    def finish_from(acc_ref):
        norm_w = nw_ref[...] * out_scale
        for hh in range(head_cols // HEAD_DIM):
            sl = slice(hh * HEAD_DIM, (hh + 1) * HEAD_DIM)
            o_ref[:, sl] = _rope(_rms(acc_ref[:, sl], norm_w), cos_ref[...], sin_ref[...]).astype(BF16)
        if head_cols < o_ref.shape[1]:
            o_ref[:, head_cols:] = acc_ref[:, head_cols:].astype(BF16)

    @pl.when(i == 0)
    def _():
        matmul_into(accs[0])

    for parity in range(2):
        @pl.when((i >= 1) & (i < n_row_tiles) & (i % 2 == parity))
        def _(parity=parity):
            matmul_into(accs[parity])
            finish_from(accs[1 - parity])

    @pl.when(i == n_row_tiles)
    def _():
        finish_from(accs[(n_row_tiles - 1) % 2])


def _proj_heads(name, h, w_in, n_rows, col0, n_cols, plain_cols, cos, sin, norm_w, seq, rope_rows, out_scale):
    k = h.shape[1]
    tm, tn, w_buffers = _proj_tiles(
        n_cols, k, lambda tm, tn: 2 * tm * k * 2 + 2 * tm * tn * 4 + 2 * tm * tn * 2 + 4 * tm * HEAD_DIM * 4)
    assert n_rows % tm == 0 and seq % tm == 0 and rope_rows % tm == 0 and col0 % tn == 0
    assert plain_cols == 0 or n_cols == tn
    head_cols = tn - plain_cols
    n_row_tiles = n_rows // tm
    rope_tiles = rope_rows // tm
    spt = seq // tm
    cos = jnp.concatenate([cos, jnp.ones((tm, HEAD_DIM), F32)], axis=0)
    sin = jnp.concatenate([sin, jnp.zeros((tm, HEAD_DIM), F32)], axis=0)
    kern = functools.partial(_proj_heads_kernel, n_row_tiles=n_row_tiles, head_cols=head_cols,
                             out_scale=out_scale)
    done = lambda i: jnp.maximum(i - 1, 0)
    table = lambda j, i: (jnp.where(done(i) < rope_tiles, done(i) % spt, spt), 0)
    return pl.pallas_call(
        kern,
        grid=(n_cols // tn, n_row_tiles + 1),
        in_specs=[pl.BlockSpec((tm, k), lambda j, i: (jnp.minimum(i, n_row_tiles - 1), 0)),
                  _weight_spec(k, tn, w_buffers, lambda j, i: (0, col0 // tn + j)),
                  pl.BlockSpec((tm, HEAD_DIM), table),
                  pl.BlockSpec((tm, HEAD_DIM), table),
                  pl.BlockSpec((1, HEAD_DIM), lambda j, i: (0, 0))],
        out_specs=pl.BlockSpec((tm, tn), lambda j, i: (done(i), j)),
        out_shape=jax.ShapeDtypeStruct((n_rows, n_cols), BF16),
        scratch_shapes=[pltpu.VMEM((k, tn), BF16),
                        pltpu.VMEM((tm, tn), F32),
                        pltpu.VMEM((tm, tn), F32)],
        compiler_params=_params("parallel", "arbitrary"),
        name=name,
    )(h, w_in, cos, sin, norm_w)


CONV_PAD_ROWS = 8
CONV_PIECE = 32


def _half_conv_rows(window, half_w, half_b):
    halo = window.shape[0]
    acc = half_b
    for j in range(CONV_WIDTH):
        off = j - CONV_LEFT
        shifted = window if off == 0 else pltpu.roll(window, (-off) % halo, 0)
        acc = acc + half_w[j] * shifted[CONV_PAD_ROWS:CONV_PAD_ROWS + CONV_PIECE, :]
    return acc


def _proj_conv_side_kernel(h_ref, w_ref, x_ref, top_ref, bot_ref, cw_ref, cb_ref, o_ref, u_ref, wb_ref,
                           *, tiles_per_seq):
    i = pl.program_id(1)
    _cast_weights_once(w_ref, wb_ref)
    o_ref[...] = jnp.dot(h_ref[...], wb_ref[...], preferred_element_type=F32)
    tm, cw = x_ref.shape
    half_w = [0.5 * cw_ref[j:j + 1, :] for j in range(CONV_WIDTH)]
    half_b = 0.5 * cb_ref[...]
    in_seq = i % tiles_per_seq
    top = jnp.where(in_seq == 0, 0.0, top_ref[...])
    bot = jnp.where(in_seq == tiles_per_seq - 1, 0.0, bot_ref[...])
    for p in range(tm // CONV_PIECE):
        lo, hi = p * CONV_PIECE - CONV_PAD_ROWS, (p + 1) * CONV_PIECE + CONV_PAD_ROWS
        parts = [x_ref[max(lo, 0):min(hi, tm), :]]
        if lo < 0:
            parts.insert(0, top)
        if hi > tm:
            parts.append(bot)
        window = parts[0] if len(parts) == 1 else jnp.concatenate(parts, axis=0)
        conv = _half_conv_rows(window, half_w, half_b)
        conv = pltpu.roll(pltpu.roll(conv, LANES // 2, 1), cw - LANES // 2, 1)
        u_ref[p * CONV_PIECE:(p + 1) * CONV_PIECE, :] = conv


def _proj_conv_side(name, h, w_in, n_rows, col_ranges, x, conv_w, conv_b, seq):
    k = h.shape[1]
    d_x = x.shape[1]
    n_cols = sum(width for _, width in col_ranges)
    conv_cols = lambda tn: d_x * tn // n_cols
    tm, tn, w_buffers = _proj_tiles(
        math.gcd(*[c for r in col_ranges for c in r]), k,
        lambda tm, tn: 2 * tm * k * 2 + 2 * tm * tn * 4 + 4 * tm * conv_cols(tn) * 4)
    tiles = [(start // tn, width // tn) for start, width in col_ranges]
    n_col_tiles = n_cols // tn
    cw = conv_cols(tn)
    assert n_rows % tm == 0 and seq % tm == 0 and tm % CONV_PIECE == 0 and cw * n_col_tiles == d_x
    assert x.shape[0] >= n_rows + CONV_PAD_ROWS
    halo_blocks = tm // CONV_PAD_ROWS

    def col_tile(j):
        first, idx = 0, tiles[0][0] + j
        for r in range(1, len(tiles)):
            first += tiles[r - 1][1]
            idx = jnp.where(j >= first, tiles[r][0] + j - first, idx)
        return idx

    kern = functools.partial(_proj_conv_side_kernel, tiles_per_seq=seq // tm)
    return pl.pallas_call(
        kern,
        grid=(n_col_tiles, n_rows // tm),
        in_specs=[pl.BlockSpec((tm, k), lambda j, i: (i, 0)),
                  _weight_spec(k, tn, w_buffers, lambda j, i: (0, col_tile(j))),
                  pl.BlockSpec((tm, cw), lambda j, i: (i, j)),
                  pl.BlockSpec((CONV_PAD_ROWS, cw), lambda j, i: (jnp.maximum(i * halo_blocks - 1, 0), j)),
                  pl.BlockSpec((CONV_PAD_ROWS, cw), lambda j, i: ((i + 1) * halo_blocks, j)),
                  pl.BlockSpec((CONV_WIDTH, cw), lambda j, i: (0, j)),
                  pl.BlockSpec((1, cw), lambda j, i: (0, j))],
        out_specs=[pl.BlockSpec((tm, tn), lambda j, i: (i, j)),
                   pl.BlockSpec((tm, cw), lambda j, i: (i, j))],
        out_shape=[jax.ShapeDtypeStruct((n_rows, n_cols), F32),
                   jax.ShapeDtypeStruct((n_rows, d_x), F32)],
        scratch_shapes=[pltpu.VMEM((k, tn), BF16)],
        compiler_params=_params("parallel", "arbitrary"),
        name=name,
    )(h, w_in, x, x, x, conv_w, conv_b)


def _proj_plain_kernel(h_ref, w_ref, o_ref, wb_ref):
    _cast_weights_once(w_ref, wb_ref)
    o_ref[...] = jnp.dot(h_ref[...], wb_ref[...], preferred_element_type=F32).astype(o_ref.dtype)


def _proj_plain(name, h, w_in, n_rows, col_ranges, out_dtype):
    k = h.shape[1]
    out_bytes = jnp.dtype(out_dtype).itemsize
    tm, tn, w_buffers = _proj_tiles(math.gcd(*[c for r in col_ranges for c in r]), k,
                                    lambda tm, tn: 2 * tm * k * 2 + 2 * tm * tn * out_bytes)
    assert n_rows % tm == 0
    tiles = [(start // tn, width // tn) for start, width in col_ranges]

    def col_tile(j):
        first, idx = 0, tiles[0][0] + j
        for r in range(1, len(tiles)):
            first += tiles[r - 1][1]
            idx = jnp.where(j >= first, tiles[r][0] + j - first, idx)
        return idx

    n_col_tiles = sum(n for _, n in tiles)
    return pl.pallas_call(
        _proj_plain_kernel,
        grid=(n_col_tiles, n_rows // tm),
        in_specs=[pl.BlockSpec((tm, k), lambda j, i: (i, 0)),
                  _weight_spec(k, tn, w_buffers, lambda j, i: (0, col_tile(j)))],
        out_specs=pl.BlockSpec((tm, tn), lambda j, i: (i, j)),
        out_shape=jax.ShapeDtypeStruct((n_rows, n_col_tiles * tn), out_dtype),
        scratch_shapes=[pltpu.VMEM((k, tn), BF16)],
        compiler_params=_params("parallel", "arbitrary"),
        name=name,
    )(h, w_in)


V_EXTRA_ROWS = 16


def _attn_kernel(q_ref, kl_ref, vl_ref, kc_ref, vc_ref, ga_ref, nw_ref, o_ref,
                 k_ref, vt_ref, st_ref, att_ref, *, n_q_tiles):
    tq = q_ref.shape[0]
    seq, n_ctx = kl_ref.shape[0], kc_ref.shape[0]
    n_kv = kl_ref.shape[1] // HEAD_DIM
    nt = (((1,), (1,)), ((), ()))
    i = pl.program_id(1)
    n_keys = seq + n_ctx

    @pl.when(i == 0)
    def _():
        k_ref[0:seq, :] = kl_ref[...]
        k_ref[seq:n_keys, :] = kc_ref[...]
        row = lax.broadcasted_iota(jnp.int32, (V_EXTRA_ROWS, n_keys), 0)
        tail = jnp.where(row == 0, 1.0, 0.0).astype(BF16)
        for kv in range(n_kv):
            ks = slice(kv * HEAD_DIM, (kv + 1) * HEAD_DIM)
            vt_ref[kv, 0:HEAD_DIM, 0:seq] = vl_ref[:, ks].T
            vt_ref[kv, 0:HEAD_DIM, seq:n_keys] = vc_ref[:, ks].T
            vt_ref[kv, HEAD_DIM:HEAD_DIM + V_EXTRA_ROWS, :] = tail

    def head_slices(kv):
        return [slice((kv * Q_PER_KV + g) * HEAD_DIM, (kv * Q_PER_KV + g + 1) * HEAD_DIM)
                for g in range(Q_PER_KV)]

    def scores(kv, slot):
        q4 = jnp.concatenate([q_ref[:, hs] for hs in head_slices(kv)], axis=0)
        st_ref[slot] = lax.dot_general(k_ref[:, kv * HEAD_DIM:(kv + 1) * HEAD_DIM], q4, nt,
                                       preferred_element_type=F32)

    def softmax_values(kv, slot, tile_slot):
        m = jnp.max(st_ref[slot], axis=0, keepdims=True)
        pt = jnp.exp2(st_ref[slot] - m).astype(BF16)
        res = jnp.dot(vt_ref[kv], pt, preferred_element_type=F32)
        ot = res[:HEAD_DIM, :] / res[HEAD_DIM:HEAD_DIM + 1, :]
        for g, hs in enumerate(head_slices(kv)):
            att_ref[tile_slot, :, hs] = ot[:, g * tq:(g + 1) * tq].T

    def attend(tile_slot):
        scores(0, 0)
        for kv in range(n_kv):
            if kv + 1 < n_kv:
                scores(kv + 1, (kv + 1) % 2)
            softmax_values(kv, kv % 2, tile_slot)

    def finish(tile_slot):
        att = att_ref[tile_slot]
        inv_rms = lax.rsqrt(jnp.mean(att * att, axis=-1, keepdims=True) + EPS)
        o_ref[...] = (att * inv_rms * (_silu(ga_ref[...]) * nw_ref[...])).astype(BF16)

    @pl.when(i == 0)
    def _():
        attend(0)

    @pl.when((i >= 1) & (i < n_q_tiles))
    def _():
        finish((i - 1) % 2)
        attend(i % 2)

    @pl.when(i == n_q_tiles)
    def _():
        finish((n_q_tiles - 1) % 2)


def _attention(qn, kv, gates, norm_w, b, seq, n_ctx, tq=256):
    n_lat, d_att = qn.shape
    d_kv = kv.shape[1] // 2
    qt = seq // tq
    ctx_blk0 = n_lat // n_ctx
    this = lambda bi, i: (bi * qt + jnp.minimum(i, qt - 1), 0)
    done = lambda bi, i: (bi * qt + jnp.maximum(i - 1, 0), 0)
    return pl.pallas_call(
        functools.partial(_attn_kernel, n_q_tiles=qt),
        grid=(b, qt + 1),
        in_specs=[pl.BlockSpec((tq, d_att), this),
                  pl.BlockSpec((seq, d_kv), lambda bi, i: (bi, 0), pipeline_mode=pl.Buffered(1)),
                  pl.BlockSpec((seq, d_kv), lambda bi, i: (bi, 1), pipeline_mode=pl.Buffered(1)),
                  pl.BlockSpec((n_ctx, d_kv), lambda bi, i: (ctx_blk0 + bi, 0), pipeline_mode=pl.Buffered(1)),
                  pl.BlockSpec((n_ctx, d_kv), lambda bi, i: (ctx_blk0 + bi, 1), pipeline_mode=pl.Buffered(1)),
                  pl.BlockSpec((tq, d_att), done),
                  pl.BlockSpec((1, d_att), lambda bi, i: (0, 0))],
        out_specs=pl.BlockSpec((tq, d_att), done),
        out_shape=jax.ShapeDtypeStruct((n_lat, d_att), BF16),
        scratch_shapes=[pltpu.VMEM((seq + n_ctx, d_kv), BF16),
                        pltpu.VMEM((d_kv // HEAD_DIM, HEAD_DIM + V_EXTRA_ROWS, seq + n_ctx), BF16),
                        pltpu.VMEM((2, seq + n_ctx, Q_PER_KV * tq), F32),
                        pltpu.VMEM((2, tq, d_att), F32)],
        compiler_params=_params("parallel", "arbitrary"),
        name="attention",
    )(qn, kv, kv, kv, kv, gates, norm_w)


LRU_SEGS = SUBLANES
GATE_ROWS = 256
SCAN_STEPS = 4
FINAL_STEPS = 2


def _softplus(z):
    return jnp.maximum(z, 0.0) + jnp.log1p(jnp.exp(-jnp.abs(z)))


def _segment_len(total):
    seg = -(-total // LRU_SEGS)
    return seg + (SUBLANES // 2 - seg) % SUBLANES


def _scan8(a, bv, reverse):
    row = lax.broadcasted_iota(jnp.int32, a.shape, 0)
    for s in (1, 2, 4):
        if reverse:
            keep = row < SUBLANES - s
            shift = SUBLANES - s
        else:
            keep = row >= s
            shift = s
        a_prev = jnp.where(keep, pltpu.roll(a, shift, 0), 1.0)
        b_prev = jnp.where(keep, pltpu.roll(bv, shift, 0), 0.0)
        bv = bv + a * b_prev
        a = a * a_prev
    return a, bv


def _lru_kernel(ul_ref, xc_ref, cw_ref, cb_ref, w_ref, bias_ref, lam_ref, gl_ref, nw_ref, y_ref, ss_ref,
                xp_ref, u_ref, a_ref, b_ref, hs_ref, *, seq, n_ctx, seg_len):
    cg = y_ref.shape[1]
    n_blk = cg // LRU_BLOCK_DIM
    total = n_ctx + seq
    padded = LRU_SEGS * seg_len
    n_chunks = total // GATE_ROWS
    gate_unroll = next(u for u in (9, 3, 2, 1) if n_chunks % u == 0)

    @pl.when(pl.program_id(1) == 0)
    def _():
        ss_ref[...] = jnp.zeros(ss_ref.shape, F32)

    for d in range(2):
        for n in range(n_blk):
            a_ref[d, n, total:padded, :] = jnp.ones((padded - total, LRU_BLOCK_DIM), F32)
            b_ref[d, n, total:padded, :] = jnp.zeros((padded - total, LRU_BLOCK_DIM), F32)
    zeros_pad = jnp.zeros((CONV_PAD_ROWS, cg), F32)
    halo = CONV_PIECE + 2 * CONV_PAD_ROWS
    half_w = [0.5 * cw_ref[j:j + 1, :] for j in range(CONV_WIDTH)]
    half_b = 0.5 * cb_ref[...]
    xp_ref[0:CONV_PAD_ROWS, :] = zeros_pad
    xp_ref[CONV_PAD_ROWS:CONV_PAD_ROWS + n_ctx, :] = xc_ref[...]
    xp_ref[CONV_PAD_ROWS + n_ctx:2 * CONV_PAD_ROWS + n_ctx, :] = zeros_pad
    for p in range(n_ctx // CONV_PIECE):
        piece = _half_conv_rows(xp_ref[p * CONV_PIECE:p * CONV_PIECE + halo, :], half_w, half_b)
        for u0 in (0, n_ctx + seq):
            u_ref[u0 + p * CONV_PIECE:u0 + (p + 1) * CONV_PIECE, :] = piece
    u_ref[n_ctx:n_ctx + seq, :] = ul_ref[...]

    for d in range(2):
        reverse = d == 1
        seq_row0 = n_ctx if reverse else 0
        half_decay = [0.5 * LRU_C * _softplus(-lam_ref[d:d + 1, n * LRU_BLOCK_DIM:(n + 1) * LRU_BLOCK_DIM])
                      for n in range(n_blk)]

        def gate_body(cb, _, d=d, seq_row0=seq_row0, half_decay=half_decay):
            for cc in range(gate_unroll):
                s0 = pl.multiple_of((cb * gate_unroll + cc) * GATE_ROWS, GATE_ROWS)
                hu = u_ref[pl.ds(pl.multiple_of(seq_row0 + s0, SUBLANES), GATE_ROWS), :]
                hub = hu.astype(BF16)
                for n in range(n_blk):
                    sl = slice(n * LRU_BLOCK_DIM, (n + 1) * LRU_BLOCK_DIM)
                    pre = jnp.dot(hub[:, sl], w_ref[d, n], preferred_element_type=F32) + bias_ref[d, n]
                    t_r = jnp.tanh(pre[:, :LRU_BLOCK_DIM])
                    t_i = jnp.tanh(pre[:, LRU_BLOCK_DIM:])
                    neg_log_a = t_r * half_decay[n] + half_decay[n]
                    a = jnp.exp2(neg_log_a * (-LOG2_E))
                    a_ref[d, n, pl.ds(s0, GATE_ROWS), :] = a
                    w = jnp.tanh(neg_log_a) * (1.0 + a * a)
                    mult = jnp.where(w > 0.0, w * lax.rsqrt(w), 0.0)
                    b_ref[d, n, pl.ds(s0, GATE_ROWS), :] = mult * ((t_i + 1.0) * hu[:, sl])
            return 0

        lax.fori_loop(0, n_chunks // gate_unroll, gate_body, 0)

    def step_rows(tb, k, steps, d):
        i = (seg_len // steps - 1 - tb) * steps + steps - 1 - k if d == 1 else tb * steps + k
        return pl.ds(i, LRU_SEGS, stride=seg_len)

    def local_body(tb, carry):
        hs, decs = [list(c) for c in carry[0]], [list(c) for c in carry[1]]
        for k in range(SCAN_STEPS):
            for d in range(2):
                idx = step_rows(tb, k, SCAN_STEPS, d)
                for n in range(n_blk):
                    a = a_ref[d, n, idx, :]
                    hs[d][n] = a * hs[d][n] + b_ref[d, n, idx, :]
                    decs[d][n] = decs[d][n] * a
        return tuple(map(tuple, hs)), tuple(map(tuple, decs))

    zero = tuple(tuple(jnp.zeros((LRU_SEGS, LRU_BLOCK_DIM), F32) for _ in range(n_blk)) for _ in range(2))
    one = tuple(tuple(jnp.ones((LRU_SEGS, LRU_BLOCK_DIM), F32) for _ in range(n_blk)) for _ in range(2))
    h_end, dec_end = lax.fori_loop(0, seg_len // SCAN_STEPS, local_body, (zero, one))

    starts = []
    for d in range(2):
        starts.append([])
        for n in range(n_blk):
            _, state = _scan8(dec_end[d][n], h_end[d][n], d == 1)
            row = lax.broadcasted_iota(jnp.int32, state.shape, 0)
            if d == 1:
                starts[d].append(jnp.where(row < LRU_SEGS - 1, pltpu.roll(state, LRU_SEGS - 1, 0), 0.0))
            else:
                starts[d].append(jnp.where(row >= 1, pltpu.roll(state, 1, 0), 0.0))

    def final_body(tb, hs):
        hs = [list(h) for h in hs]
        loaded = [(d, idx, [a_ref[d, n, idx, :] for n in range(n_blk)], [b_ref[d, n, idx, :] for n in range(n_blk)])
                  for k in range(FINAL_STEPS) for d in range(2)
                  for idx in [step_rows(tb, k, FINAL_STEPS, d)]]
        for d, idx, a, bv in loaded:
            for n in range(n_blk):
                hs[d][n] = a[n] * hs[d][n] + bv[n]
                hs_ref[d, n, idx, :] = hs[d][n]
        return tuple(map(tuple, hs))

    lax.fori_loop(0, seg_len // FINAL_STEPS, final_body, tuple(map(tuple, starts)))

    for r0 in range(0, seq, GATE_ROWS):
        rows = slice(r0, r0 + GATE_ROWS)
        lru = jnp.concatenate(
            [hs_ref[0, n, n_ctx + r0:n_ctx + r0 + GATE_ROWS, :] + hs_ref[1, n, r0:r0 + GATE_ROWS, :]
             for n in range(n_blk)], axis=1)
        y_ref[rows, :] = (lru * nw_ref[...] * _silu(gl_ref[rows, :])).astype(BF16)
        ss_ref[rows, :] += jnp.broadcast_to(jnp.sum(lru * lru, axis=-1, keepdims=True), (GATE_ROWS, LANES))


def _lru(hu, xl, gates, conv_w, conv_b, w_cat, bias_cat, lam, norm_w, b, seq, n_ctx, d_att, cg=512):
    n_lat = b * seq
    d_lru = xl.shape[1]
    n_blk = cg // LRU_BLOCK_DIM
    n_groups = d_lru // cg
    seg_len = _segment_len(n_ctx + seq)
    assert n_ctx % GATE_ROWS == 0 and seq % GATE_ROWS == 0 and GATE_ROWS % CONV_PIECE == 0
    assert seg_len % SCAN_STEPS == 0 and seg_len % FINAL_STEPS == 0
    kern = functools.partial(_lru_kernel, seq=seq, n_ctx=n_ctx, seg_len=seg_len)
    ctx_blk0 = n_lat // n_ctx
    gl_blk0 = d_att // cg
    scan_rows = LRU_SEGS * seg_len
    return pl.pallas_call(
        kern,
        grid=(b, n_groups),
        in_specs=[pl.BlockSpec((seq, cg), lambda bi, g: (bi, g)),
                  pl.BlockSpec((n_ctx, cg), lambda bi, g: (ctx_blk0 + bi, g)),
                  pl.BlockSpec((CONV_WIDTH, cg), lambda bi, g: (0, g)),
                  pl.BlockSpec((1, cg), lambda bi, g: (0, g)),
                  pl.BlockSpec((2, n_blk, LRU_BLOCK_DIM, 2 * LRU_BLOCK_DIM), lambda bi, g: (0, g, 0, 0)),
                  pl.BlockSpec((2, n_blk, 1, 2 * LRU_BLOCK_DIM), lambda bi, g: (0, g, 0, 0)),
                  pl.BlockSpec((2, cg), lambda bi, g: (0, g)),
                  pl.BlockSpec((seq, cg), lambda bi, g: (bi, gl_blk0 + g)),
                  pl.BlockSpec((1, cg), lambda bi, g: (0, g))],
        out_specs=[pl.BlockSpec((seq, cg), lambda bi, g: (bi, g)),
                   pl.BlockSpec((seq, LANES), lambda bi, g: (bi, 0))],
        out_shape=[jax.ShapeDtypeStruct((n_lat, d_lru), BF16),
                   jax.ShapeDtypeStruct((n_lat, LANES), F32)],
        scratch_shapes=[pltpu.VMEM((n_ctx + 2 * CONV_PAD_ROWS, cg), F32),
                        pltpu.VMEM((2 * n_ctx + seq, cg), F32),
                        pltpu.VMEM((2, n_blk, scan_rows, LRU_BLOCK_DIM), F32),
                        pltpu.VMEM((2, n_blk, scan_rows, LRU_BLOCK_DIM), F32),
                        pltpu.VMEM((2, n_blk, scan_rows, LRU_BLOCK_DIM), F32)],
        compiler_params=_params("parallel", "arbitrary"),
        name="rglru",
    )(hu, xl, conv_w, conv_b, w_cat, bias_cat, lam, gates, norm_w)


def _out_kernel(ma_ref, yl_ref, ss_ref, wa_ref, wl_ref, x_ref, gate_ref, o_ref, wab_ref, wlb_ref,
                *, tiles_per_batch, d_lru):
    @pl.when(pl.program_id(1) == 0)
    def _():
        wab_ref[...] = wa_ref[...].astype(BF16)
        wlb_ref[...] = wl_ref[...].astype(BF16)

    bi = pl.program_id(1) // tiles_per_batch
    gate = gate_ref[pl.ds(bi, 1), :]
    sumsq = ss_ref[:, 0:LANES]
    for g in range(1, ss_ref.shape[1] // LANES):
        sumsq = sumsq + ss_ref[:, g * LANES:(g + 1) * LANES]
    inv_rms = lax.rsqrt(sumsq * (1.0 / d_lru) + EPS)
    inv_rms = jnp.concatenate([inv_rms] * (o_ref.shape[1] // LANES), axis=1)
    acc = jnp.dot(ma_ref[...], wab_ref[...], preferred_element_type=F32)
    acc = acc + inv_rms * jnp.dot(yl_ref[...], wlb_ref[...], preferred_element_type=F32)
    o_ref[...] = x_ref[...] + gate * acc


def _out_proj(mix_att, y_lru, ss, w_out, x2, mod, seq, gate_col0):
    m, d_att = mix_att.shape
    d_lru = y_lru.shape[1]
    assert d_att == d_lru
    n = w_out.shape[1]
    tm, tn, w_buffers = _proj_tiles(
        n, d_att + d_lru,
        lambda tm, tn: 2 * tm * (d_att + d_lru) * 2 + 4 * tm * tn * 4 + 2 * tm * LANES * 4)
    assert m % tm == 0 and seq % tm == 0 and gate_col0 % tn == 0
    gb = gate_col0 // tn
    kern = functools.partial(_out_kernel, tiles_per_batch=seq // tm, d_lru=d_lru)
    return pl.pallas_call(
        kern,
        grid=(n // tn, m // tm),
        in_specs=[pl.BlockSpec((tm, d_att), lambda j, i: (i, 0)),
                  pl.BlockSpec((tm, d_lru), lambda j, i: (i, 0)),
                  pl.BlockSpec((tm, ss.shape[1]), lambda j, i: (i, 0)),
                  _weight_spec(d_att, tn, w_buffers, lambda j, i: (0, j)),
                  _weight_spec(d_lru, tn, w_buffers, lambda j, i: (1, j)),
                  pl.BlockSpec((tm, tn), lambda j, i: (i, j)),
                  pl.BlockSpec((MOD_ROWS, tn), lambda j, i: (0, gb + j))],
        out_specs=pl.BlockSpec((tm, tn), lambda j, i: (i, j)),
        out_shape=jax.ShapeDtypeStruct((m, n), F32),
        scratch_shapes=[pltpu.VMEM((d_att, tn), BF16), pltpu.VMEM((d_lru, tn), BF16)],
        compiler_params=_params("parallel", "arbitrary"),
        name="out_proj",
    )(mix_att, y_lru, ss, w_out, w_out, x2, mod)


def _rope_tables(seq):
    pos = jnp.arange(seq)
    row = (pos // GRID_W).astype(F32)
    col = (pos % GRID_W).astype(F32)
    n_freq = HEAD_DIM // 4
    freqs = ROPE_THETA ** (-jnp.arange(n_freq, dtype=F32) / n_freq)
    ang_r = row[:, None] * freqs
    ang_c = col[:, None] * freqs
    cos = jnp.concatenate([jnp.cos(ang_r), jnp.cos(ang_r), jnp.cos(ang_c), jnp.cos(ang_c)], axis=1)
    sin = jnp.concatenate([-jnp.sin(ang_r), jnp.sin(ang_r), -jnp.sin(ang_c), jnp.sin(ang_c)], axis=1)
    return cos, sin


def kernel(x, c, ctx, c_ctx, w_ada, b_ada, norm_w, w_in, q_norm_w, k_norm_w, conv_w, conv_b,
           lru_wa, lru_ba, lru_wx, lru_bx, lru_lambda, out_norm_att, out_norm_lru, w_out):
    assert w_ada.shape[0] == 1, "single-layer kernel: only the latent stream is produced"
    b, seq, d = x.shape
    n_ctx = ctx.shape[1]
    d_att = out_norm_att.shape[1]
    d_lru = out_norm_lru.shape[1]
    d_kv = (w_in.shape[2] - 2 * d_att - 2 * d_lru) // 2
    n_lat = b * seq
    n_all = n_lat + b * n_ctx
    assert b + 1 <= MOD_ROWS
    k_col0, v_col0, ga_col0 = d_att, d_att + d_kv, d_att + 2 * d_kv
    xl_col0 = ga_col0 + d_att
    gl_col0 = xl_col0 + d_lru

    x2 = x.reshape(n_lat, d)
    c2 = ctx.reshape(b * n_ctx, d)

    cvec = jnp.concatenate([c, c_ctx[None, :], jnp.zeros((MOD_ROWS - b - 1, d), F32)], axis=0)
    mod = _modulation(cvec, w_ada[0], b_ada)

    h = _prenorm(x2, c2, norm_w, mod, seq)
    w_in0 = w_in[0]

    cos, sin = _rope_tables(seq)
    qn = _proj_heads("proj_q", h, w_in0, n_lat, 0, d_att, 0, cos, sin, q_norm_w, seq, n_lat,
                     HEAD_DIM ** -0.5 * LOG2_E)
    assert v_col0 == k_col0 + d_kv
    kv = _proj_heads("proj_kv", h, w_in0, n_all, k_col0, 2 * d_kv, d_kv, cos, sin, k_norm_w, seq, n_lat, 1.0)
    xl = _proj_plain("proj_xl", h, w_in0, n_all, [(xl_col0, d_lru)], F32)
    gates, hu = _proj_conv_side("proj_gates", h, w_in0, n_lat, [(ga_col0, d_att), (gl_col0, d_lru)],
                                xl, conv_w[0], conv_b, seq)

    mix_att = _attention(qn, kv, gates, out_norm_att, b, seq, n_ctx)

    n_blocks = d_lru // LRU_BLOCK_DIM
    w_cat = jnp.concatenate([lru_wa[0], lru_wx[0]], axis=-1).astype(BF16)
    bias_cat = 0.5 * jnp.concatenate([lru_ba[0].reshape(2, n_blocks, 1, LRU_BLOCK_DIM),
                                      lru_bx[0].reshape(2, n_blocks, 1, LRU_BLOCK_DIM)], axis=-1)
    y_lru, ss = _lru(hu, xl, gates, conv_w[0], conv_b, w_cat, bias_cat, lru_lambda[0], out_norm_lru,
                     b, seq, n_ctx, d_att)

    out = _out_proj(mix_att, y_lru, ss, w_out[0], x2, mod, seq, 2 * d)
    return out.reshape(b, seq, d)
```
